```python
import math
import jax, jax.numpy as jnp
from jax import lax
import numpy as np

D_MODEL = 1024
BATCH = 32
SEQ = 256
DEPTH = 4
DEC_BATCH = 2
DEC_SEQ = 1024
PAST_LEN = 512

GRID_W = 64
N_DIR = 2
N_EVEN = (DEPTH + 1) // 2
N_ODD = DEPTH // 2
A_HEADS = 4
A_DK = 128
A_DV = 128
A_CONV = 5
A_CHUNK = 64
B_HEADS = 8
B_DK = 64
B_DV = 64
B_CHUNK = 64
ROPE_FREQS = B_DK // 4
ROPE_BASE = 10000.0
C_GROUP = 16
C_GROUPS = 32
C_STATE = 64
C_WIDTH = C_GROUPS * C_GROUP
S5_MAX_RE = -1e-4
D_HEADS = 4
D_DK = 64
D_DV = 128
D_RANK = 16
D_TAU = 16.0
D_CHUNK = 16
N_EXPERTS = 16
EXPERT_FF = 1024
EC_CAPACITY_FACTOR = 2
RMS_EPS = 1e-6

EVEN_SPLITS = (A_HEADS * A_DK, A_HEADS * A_DK, A_HEADS * A_DV, A_HEADS * A_DV, N_DIR * A_HEADS, N_DIR * A_HEADS,
               B_HEADS * B_DK, B_HEADS * B_DK, B_HEADS * B_DV, B_HEADS * B_DV)
EVEN_IN = sum(EVEN_SPLITS)
ODD_SPLITS = (C_WIDTH, D_HEADS * D_DK, D_HEADS * D_DK, D_HEADS * D_DV, D_HEADS * D_DV, N_DIR * D_RANK)
ODD_IN = sum(ODD_SPLITS)
MIX_WIDTH_EVEN = A_HEADS * A_DV + B_HEADS * B_DV
MIX_WIDTH_ODD = C_WIDTH + D_HEADS * D_DV

kernel_name = 'hybrid_bidir_diffusion_step'


def rms_norm(x, gain):
    xf = x.astype(jnp.float32)
    y = xf * lax.rsqrt(jnp.mean(xf * xf, axis=-1, keepdims=True) + RMS_EPS)
    return (y * gain.astype(jnp.float32)).astype(x.dtype)


def ada_modulation(cond, w_ada, b_ada):
    m = (jax.nn.silu(cond) @ w_ada + b_ada)[:, None, :]
    return jnp.split(m, 6, axis=-1)


def split_cols(t, sizes):
    return jnp.split(t, np.cumsum(sizes)[:-1].tolist(), axis=-1)


def to_heads(t, n_heads):
    b, l, _ = t.shape
    return t.reshape(b, l, n_heads, -1).transpose(0, 2, 1, 3)


def from_heads(t):
    b, h, l, d = t.shape
    return t.transpose(0, 2, 1, 3).reshape(b, l, h * d)


def dir_heads(t, n_heads):
    b, l, _ = t.shape
    return t.reshape(b, l, N_DIR, n_heads, -1).transpose(2, 0, 3, 1, 4)


def flip_seq(t):
    return jnp.flip(t, axis=2)


def l2_normalize(t):
    return t * lax.rsqrt(jnp.sum(t * t, axis=-1, keepdims=True) + 1e-6)


def head_rms_norm(o, gain):
    return o * lax.rsqrt(jnp.mean(o * o, axis=-1, keepdims=True) + RMS_EPS) * gain.astype(jnp.float32)


def head_layer_norm(o):
    mu = jnp.mean(o, axis=-1, keepdims=True)
    var = jnp.mean(jnp.square(o - mu), axis=-1, keepdims=True)
    return (o - mu) * lax.rsqrt(var + RMS_EPS)


def short_conv_silu(x, w):
    pad = w.shape[0] // 2
    y = lax.conv_general_dilated(x, w[:, None, :].astype(x.dtype), window_strides=(1,), padding=[(pad, pad)],
                                 dimension_numbers=('NWC', 'WIO', 'NWC'), feature_group_count=x.shape[-1])
    return jax.nn.silu(y)


def grid_rope_angles(n_tokens):
    rows = n_tokens // GRID_W
    row = jnp.repeat(jnp.arange(rows, dtype=jnp.float32), GRID_W)
    col = jnp.tile(jnp.arange(GRID_W, dtype=jnp.float32), rows)
    freq = ROPE_BASE ** (-jnp.arange(ROPE_FREQS, dtype=jnp.float32) / ROPE_FREQS)
    return jnp.stack([row[:, None] * freq, col[:, None] * freq], axis=1)


def apply_rope_2d(x, ang):
    xr = x.reshape(*x.shape[:-1], 2, 2, ROPE_FREQS)
    x1, x2 = xr[..., 0, :], xr[..., 1, :]
    cos, sin = jnp.cos(ang), jnp.sin(ang)
    return jnp.stack([x1 * cos - x2 * sin, x1 * sin + x2 * cos], axis=-2).reshape(x.shape)


def gated_delta_chunked(q, k, v, beta, g, s0):
    bsz, nh, n_tok, dk = q.shape
    dv = v.shape[-1]
    cs = A_CHUNK
    nc = n_tok // cs
    q = (q * dk ** -0.5).reshape(bsz, nh, nc, cs, dk)
    k = k.reshape(bsz, nh, nc, cs, dk)
    v = v.reshape(bsz, nh, nc, cs, dv)
    beta = beta.reshape(bsz, nh, nc, cs, 1)
    g = jnp.cumsum(g.reshape(bsz, nh, nc, cs), axis=-1)
    pos = jnp.arange(cs)
    tri = pos[:, None] >= pos[None, :]
    strict = pos[:, None] > pos[None, :]
    decay = jnp.exp(jnp.where(tri, g[..., :, None] - g[..., None, :], -jnp.inf))
    k_beta = k * beta
    lmat = jnp.where(strict, jnp.einsum('bhnid,bhnjd->bhnij', k_beta, k) * decay, 0.0)
    eye = jnp.eye(cs, dtype=q.dtype)
    t_inv = lax.linalg.triangular_solve(eye + lmat, jnp.broadcast_to(eye, lmat.shape), left_side=True,
                                        lower=True, unit_diagonal=True)
    u = jnp.einsum('bhnij,bhnje->bhnie', t_inv, v * beta)
    w = jnp.einsum('bhnij,bhnjd->bhnid', t_inv, k_beta * jnp.exp(g)[..., None])
    attn = jnp.einsum('bhnid,bhnjd->bhnij', q, k) * decay
    q_in = q * jnp.exp(g)[..., None]
    g_last = g[..., -1]
    k_out = k * jnp.exp(g_last[..., None] - g)[..., None]

    def step(s, xs):
        q_c, k_c, u_c, w_c, a_c, gl = xs
        v_new = u_c - jnp.einsum('bhid,bhde->bhie', w_c, s)
        o = jnp.einsum('bhid,bhde->bhie', q_c, s) + jnp.einsum('bhij,bhje->bhie', a_c, v_new)
        s = s * jnp.exp(gl)[..., None, None] + jnp.einsum('bhjd,bhje->bhde', k_c, v_new)
        return s, o

    xs = tuple(jnp.moveaxis(t, 2, 0) for t in (q_in, k_out, u, w, attn, g_last))
    s_fin, o = lax.scan(step, s0, xs)
    return jnp.moveaxis(o, 0, 2).reshape(bsz, nh, n_tok, dv), s_fin


def retention_chunked(q, k, v, log_gamma, s0):
    bsz, nh, n_tok, dk = q.shape
    dv = v.shape[-1]
    cs = B_CHUNK
    nc = n_tok // cs
    q = (q * dk ** -0.5).reshape(bsz, nh, nc, cs, dk)
    k = k.reshape(bsz, nh, nc, cs, dk)
    v = v.reshape(bsz, nh, nc, cs, dv)
    pos = jnp.arange(cs, dtype=jnp.float32)
    tri = pos[:, None] >= pos[None, :]
    lg = log_gamma[:, None, None]
    dmat = jnp.exp(jnp.where(tri, (pos[:, None] - pos[None, :]) * lg, -jnp.inf))
    xi = jnp.exp((pos + 1.0) * log_gamma[:, None])
    zeta = jnp.exp((cs - 1.0 - pos) * log_gamma[:, None])
    g_chunk = jnp.exp(cs * log_gamma)
    scores = jnp.einsum('bhnid,bhnjd->bhnij', q, k) * dmat[:, None]
    intra = jnp.einsum('bhnij,bhnje->bhnie', scores, v)
    k_sc = k * zeta[:, None, :, None]

    def step(s, xs):
        q_c, k_c, v_c = xs
        o = jnp.einsum('bhid,bhde->bhie', q_c, s) * xi[:, :, None]
        s = s * g_chunk[:, None, None] + jnp.einsum('bhjd,bhje->bhde', k_c, v_c)
        return s, o

    xs = tuple(jnp.moveaxis(t, 2, 0) for t in (q, k_sc, v))
    s_fin, cross = lax.scan(step, s0, xs)
    o = intra + jnp.moveaxis(cross, 0, 2)
    return o.reshape(bsz, nh, n_tok, dv), s_fin


def gla_chunked(q, k, v, log_a, s0):
    bsz, nh, n_tok, dk = q.shape
    dv = v.shape[-1]
    cs = D_CHUNK
    nc = n_tok // cs
    q = (q * dk ** -0.5).reshape(bsz, nh, nc, cs, dk)
    k = k.reshape(bsz, nh, nc, cs, dk)
    v = v.reshape(bsz, nh, nc, cs, dv)
    b = jnp.cumsum(log_a.reshape(bsz, nh, nc, cs, dk), axis=3)
    pos = jnp.arange(cs)
    tri = pos[:, None] >= pos[None, :]
    decay = jnp.exp(jnp.where(tri[:, :, None], b[..., :, None, :] - b[..., None, :, :], -jnp.inf))
    scores = jnp.einsum('bhnid,bhnjd,bhnijd->bhnij', q, k, decay)
    intra = jnp.einsum('bhnij,bhnje->bhnie', scores, v)
    b_last = b[:, :, :, -1]
    q_in = q * jnp.exp(b)
    k_out = k * jnp.exp(b_last[:, :, :, None] - b)

    def step(s, xs):
        q_c, k_c, v_c, bl = xs
        o = jnp.einsum('bhid,bhde->bhie', q_c, s)
        s = s * jnp.exp(bl)[..., None] + jnp.einsum('bhjd,bhje->bhde', k_c, v_c)
        return s, o

    xs = tuple(jnp.moveaxis(t, 2, 0) for t in (q_in, k_out, v, b_last))
    s_fin, cross = lax.scan(step, s0, xs)
    o = intra + jnp.moveaxis(cross, 0, 2)
    return o.reshape(bsz, nh, n_tok, dv), s_fin


def linear_recurrence_combine(e1, e2):
    a1, b1 = e1
    a2, b2 = e2
    return a1 * a2, a2 * b1 + b2


def s5_scan(u, a_re, a_im, log_dt, b_re, b_im, s0):
    lam = lax.complex(jnp.minimum(a_re.astype(jnp.float32), S5_MAX_RE), a_im.astype(jnp.float32))
    dt = jnp.exp(log_dt.astype(jnp.float32))[:, None]
    lam_bar = jnp.exp(lam * dt)
    b_bar = ((lam_bar - 1.0) / lam)[..., None] * lax.complex(b_re.astype(jnp.float32), b_im.astype(jnp.float32))
    bu = jnp.einsum('gpi,blgi->blgp', b_bar, u.astype(jnp.complex64))
    bu = bu.at[:, 0].add(lam_bar * s0)
    decay = jnp.broadcast_to(lam_bar, bu.shape)
    _, states = lax.associative_scan(linear_recurrence_combine, (decay, bu), axis=1)
    return states, states[:, -1]


def even_mixer(h, w_in, conv_w, a_log, dt_bias, gdn_gain, ret_log_decay, ret_gain, w_out, s0_gdn, s0_ret, rope_ang):
    f32 = jnp.float32
    proj = h @ w_in
    qa, ka, va, za, ba, aa, qb, kb, vb, gb = split_cols(proj, EVEN_SPLITS)
    qkv = short_conv_silu(jnp.concatenate([qa, ka, va], axis=-1), conv_w)
    qa, ka, va = split_cols(qkv, (A_HEADS * A_DK, A_HEADS * A_DK, A_HEADS * A_DV))
    qa = l2_normalize(to_heads(qa.astype(f32), A_HEADS))
    ka = l2_normalize(to_heads(ka.astype(f32), A_HEADS))
    va = to_heads(va.astype(f32), A_HEADS)
    beta = jax.nn.sigmoid(dir_heads(ba.astype(f32), A_HEADS)[..., 0])
    g = -jnp.exp(a_log.astype(f32))[:, None, :, None] * jax.nn.softplus(
        dir_heads(aa.astype(f32), A_HEADS)[..., 0] + dt_bias.astype(f32)[:, None, :, None])
    o_f, s_f = gated_delta_chunked(qa, ka, va, beta[0], g[0], s0_gdn[:, 0])
    o_b, s_b = gated_delta_chunked(flip_seq(qa), flip_seq(ka), flip_seq(va), flip_seq(beta[1]), flip_seq(g[1]), s0_gdn[:, 1])
    out_a = from_heads(head_rms_norm(o_f + flip_seq(o_b), gdn_gain)).astype(h.dtype) * jax.nn.silu(za)
    qb = to_heads(qb.astype(f32), B_HEADS)
    kb = to_heads(kb.astype(f32), B_HEADS)
    vb = to_heads(vb.astype(f32), B_HEADS)
    if rope_ang is not None:
        qb = apply_rope_2d(qb, rope_ang)
        kb = apply_rope_2d(kb, rope_ang)
    log_gamma = -jnp.exp(ret_log_decay.astype(f32))
    r_f, t_f = retention_chunked(qb, kb, vb, log_gamma[0], s0_ret[:, 0])
    r_b, t_b = retention_chunked(flip_seq(qb), flip_seq(kb), flip_seq(vb), log_gamma[1], s0_ret[:, 1])
    out_b = (from_heads(head_layer_norm(r_f + flip_seq(r_b))) * ret_gain.astype(f32)).astype(h.dtype) * jax.nn.silu(gb)
    out = jnp.concatenate([out_a, out_b], axis=-1) @ w_out
    return out, jnp.stack([s_f, s_b], axis=1), jnp.stack([t_f, t_b], axis=1)


def odd_mixer(h, w_in, a_re, a_im, log_dt, b_re, b_im, c_re, c_im, d_skip, w_glu, w_gate2, b_gate, gla_gain, w_out,
              s0_s5, s0_gla):
    f32 = jnp.float32
    bsz, n_tok, _ = h.shape
    proj = h @ w_in
    u, qd, kd, vd, gd, alr = split_cols(proj, ODD_SPLITS)
    uf = u.astype(f32).reshape(bsz, n_tok, C_GROUPS, C_GROUP)
    s0 = lax.complex(s0_s5[..., 0], s0_s5[..., 1])
    st_f, fin_f = s5_scan(uf, a_re[0], a_im[0], log_dt[0], b_re, b_im, s0[:, 0])
    st_b, fin_b = s5_scan(jnp.flip(uf, axis=1), a_re[1], a_im[1], log_dt[1], b_re, b_im, s0[:, 1])
    states = st_f + jnp.flip(st_b, axis=1)
    cmat = lax.complex(c_re.astype(f32), c_im.astype(f32))
    y = jnp.einsum('gip,blgp->blgi', cmat, states).real + d_skip.astype(f32).reshape(C_GROUPS, C_GROUP) * uf
    y = jax.nn.gelu(y.reshape(bsz, n_tok, C_WIDTH)).astype(h.dtype)
    out_c = y * jax.nn.sigmoid(y @ w_glu)
    fin = jnp.stack([fin_f, fin_b], axis=1)
    s5_state = jnp.stack([fin.real, fin.imag], axis=-1)
    q = to_heads(qd.astype(f32), D_HEADS)
    k = to_heads(kd.astype(f32), D_HEADS)
    v = to_heads(vd.astype(f32), D_HEADS)
    alr = alr.astype(f32).reshape(bsz, n_tok, N_DIR, D_RANK)
    logits = jnp.einsum('blnr,nrk->nblk', alr, w_gate2.astype(f32)) + b_gate.astype(f32)[:, None, None, :]
    log_a = (jax.nn.log_sigmoid(logits) / D_TAU).reshape(N_DIR, bsz, n_tok, D_HEADS, D_DK).transpose(0, 1, 3, 2, 4)
    o_f, s_f = gla_chunked(q, k, v, log_a[0], s0_gla[:, 0])
    o_b, s_b = gla_chunked(flip_seq(q), flip_seq(k), flip_seq(v), flip_seq(log_a[1]), s0_gla[:, 1])
    out_d = from_heads(head_rms_norm(o_f + flip_seq(o_b), gla_gain)).astype(h.dtype) * jax.nn.silu(gd)
    out = jnp.concatenate([out_c, out_d], axis=-1) @ w_out
    return out, s5_state, jnp.stack([s_f, s_b], axis=1)


def expert_choice_ffn(h, w_router, w_gate, w_up, w_down):
    bsz, n_tok, _ = h.shape
    cap = EC_CAPACITY_FACTOR * n_tok // N_EXPERTS
    aff = jax.nn.softmax(jnp.einsum('bld,de->ble', h, w_router).astype(jnp.float32), axis=-1)
    gate, idx = lax.top_k(jnp.swapaxes(aff, 1, 2), cap)
    bidx = jnp.arange(bsz)[:, None, None]
    xs = h[bidx, idx]
    hid = jax.nn.silu(jnp.einsum('becd,edf->becf', xs, w_gate)) * jnp.einsum('becd,edf->becf', xs, w_up)
    ys = jnp.einsum('becf,efd->becd', hid, w_down) * gate[..., None].astype(h.dtype)
    return jnp.zeros_like(h).at[bidx, idx].add(ys)


def trunk(x, cond, rope_ang, st_gdn, st_ret, st_s5, st_gla, p):
    f32 = jnp.float32
    out_gdn, out_ret, out_s5, out_gla = [], [], [], []
    for layer in range(DEPTH):
        sh1, sc1, g1, sh2, sc2, g2 = ada_modulation(cond, p['ada_w'][layer], p['ada_b'][layer])
        h = rms_norm(x, p['norm_mix'][layer]) * (1 + sc1) + sh1
        j = layer // 2
        if layer % 2 == 0:
            mix, s_a, s_b = even_mixer(h, p['ev_w_in'][j], p['ev_conv'][j], p['gdn_a_log'][j], p['gdn_dt_bias'][j],
                                       p['gdn_gain'][j], p['ret_log_decay'][j], p['ret_gain'][j], p['ev_w_out'][j],
                                       st_gdn[:, j].astype(f32), st_ret[:, j].astype(f32), rope_ang)
            out_gdn.append(s_a)
            out_ret.append(s_b)
        else:
            mix, s_c, s_d = odd_mixer(h, p['od_w_in'][j], p['s5_a_re'][j], p['s5_a_im'][j], p['s5_log_dt'][j],
                                      p['s5_b_re'][j], p['s5_b_im'][j], p['s5_c_re'][j], p['s5_c_im'][j],
                                      p['s5_d'][j], p['s5_w_glu'][j], p['gla_w_gate2'][j], p['gla_b_gate'][j],
                                      p['gla_gain'][j], p['od_w_out'][j],
                                      st_s5[:, j].astype(f32), st_gla[:, j].astype(f32))
            out_s5.append(s_c)
            out_gla.append(s_d)
        x = x + g1 * mix
        h = rms_norm(x, p['norm_ffn'][layer]) * (1 + sc2) + sh2
        x = x + g2 * expert_choice_ffn(h, p['moe_router'][layer], p['moe_w_gate'][layer], p['moe_w_up'][layer],
                                       p['moe_w_down'][layer])
    return (rms_norm(x, p['norm_final']), jnp.stack(out_gdn, axis=1), jnp.stack(out_ret, axis=1),
            jnp.stack(out_s5, axis=1), jnp.stack(out_gla, axis=1))


def setup_inputs(seed: int = 0) -> dict:
    key = jax.random.key(seed)
    keys = jax.random.split(key, 64)
    counter = iter(range(64))
    f32 = jnp.float32

    def nrm(shape, scale):
        return jax.random.normal(keys[next(counter)], shape, f32) * scale

    def uni(shape, lo, hi):
        return jax.random.uniform(keys[next(counter)], shape, f32, lo, hi)

    def gain(shape):
        return 1.0 + nrm(shape, 0.01)

    dt_gdn = jnp.exp(uni((N_EVEN, N_DIR, A_HEADS), math.log(1e-3), math.log(1e-1)))
    gamma = 1.0 - 2.0 ** (-5.0 - jnp.arange(B_HEADS, dtype=f32))
    ret_init = jnp.log(-jnp.log(gamma))
    s5_im = jnp.pi * jnp.arange(C_STATE, dtype=f32)
    return {
        'x_prompt': nrm((BATCH, SEQ, D_MODEL), 1.0),
        'x_sample': nrm((DEC_BATCH, DEC_SEQ, D_MODEL), 1.0),
        'state_gdn': nrm((DEC_BATCH, N_EVEN, N_DIR, A_HEADS, A_DK, A_DV), 0.1),
        'state_ret': nrm((DEC_BATCH, N_EVEN, N_DIR, B_HEADS, B_DK, B_DV), 0.5),
        'state_s5': nrm((DEC_BATCH, N_ODD, N_DIR, C_GROUPS, C_STATE, 2), 0.1),
        'state_gla': nrm((DEC_BATCH, N_ODD, N_DIR, D_HEADS, D_DK, D_DV), 0.5),
        'c': nrm((DEC_BATCH, D_MODEL), 1.0),
        'c_ctx': nrm((D_MODEL,), 1.0),
        'ada_w': nrm((DEPTH, D_MODEL, 6 * D_MODEL), 0.5 * D_MODEL ** -0.5),
        'ada_b': nrm((DEPTH, 6 * D_MODEL), 0.01),
        'norm_mix': gain((DEPTH, D_MODEL)),
        'norm_ffn': gain((DEPTH, D_MODEL)),
        'norm_final': gain((D_MODEL,)),
        'ev_w_in': nrm((N_EVEN, D_MODEL, EVEN_IN), D_MODEL ** -0.5),
        'ev_conv': nrm((N_EVEN, A_CONV, 2 * A_HEADS * A_DK + A_HEADS * A_DV), A_CONV ** -0.5),
        'gdn_a_log': jnp.log(uni((N_EVEN, N_DIR, A_HEADS), 1.0, 16.0)),
        'gdn_dt_bias': dt_gdn + jnp.log(-jnp.expm1(-dt_gdn)),
        'gdn_gain': gain((N_EVEN, A_DV)),
        'ret_log_decay': ret_init + nrm((N_EVEN, N_DIR, B_HEADS), 0.05),
        'ret_gain': gain((N_EVEN, B_HEADS * B_DV)),
        'ev_w_out': nrm((N_EVEN, MIX_WIDTH_EVEN, D_MODEL), MIX_WIDTH_EVEN ** -0.5),
        'od_w_in': nrm((N_ODD, D_MODEL, ODD_IN), D_MODEL ** -0.5),
        's5_a_re': -0.5 + nrm((N_ODD, N_DIR, C_GROUPS, C_STATE), 0.01),
        's5_a_im': s5_im + nrm((N_ODD, N_DIR, C_GROUPS, C_STATE), 0.01),
        's5_log_dt': uni((N_ODD, N_DIR, C_GROUPS), math.log(1e-3), math.log(1e-1)),
        's5_b_re': nrm((N_ODD, C_GROUPS, C_STATE, C_GROUP), (2 * C_GROUP) ** -0.5),
        's5_b_im': nrm((N_ODD, C_GROUPS, C_STATE, C_GROUP), (2 * C_GROUP) ** -0.5),
        's5_c_re': nrm((N_ODD, C_GROUPS, C_GROUP, C_STATE), 0.5),
        's5_c_im': nrm((N_ODD, C_GROUPS, C_GROUP, C_STATE), 0.5),
        's5_d': nrm((N_ODD, C_WIDTH), 1.0),
        's5_w_glu': nrm((N_ODD, C_WIDTH, C_WIDTH), C_WIDTH ** -0.5),
        'gla_w_gate2': nrm((N_ODD, N_DIR, D_RANK, D_HEADS * D_DK), D_RANK ** -0.5),
        'gla_b_gate': nrm((N_ODD, N_DIR, D_HEADS * D_DK), 0.1),
        'gla_gain': gain((N_ODD, D_DV)),
        'od_w_out': nrm((N_ODD, MIX_WIDTH_ODD, D_MODEL), MIX_WIDTH_ODD ** -0.5),
        'moe_router': nrm((DEPTH, D_MODEL, N_EXPERTS), D_MODEL ** -0.5),
        'moe_w_gate': nrm((DEPTH, N_EXPERTS, D_MODEL, EXPERT_FF), D_MODEL ** -0.5),
        'moe_w_up': nrm((DEPTH, N_EXPERTS, D_MODEL, EXPERT_FF), D_MODEL ** -0.5),
        'moe_w_down': nrm((DEPTH, N_EXPERTS, EXPERT_FF, D_MODEL), EXPERT_FF ** -0.5),
    }


def reference(x_prompt, x_sample, state_gdn, state_ret, state_s5, state_gla, c, c_ctx, ada_w, ada_b, norm_mix,
              norm_ffn, norm_final, ev_w_in, ev_conv, gdn_a_log, gdn_dt_bias, gdn_gain, ret_log_decay, ret_gain,
              ev_w_out, od_w_in, s5_a_re, s5_a_im, s5_log_dt, s5_b_re, s5_b_im, s5_c_re, s5_c_im, s5_d, s5_w_glu,
              gla_w_gate2, gla_b_gate, gla_gain, od_w_out, moe_router, moe_w_gate, moe_w_up, moe_w_down):
    p = dict(ada_w=ada_w, ada_b=ada_b, norm_mix=norm_mix, norm_ffn=norm_ffn, norm_final=norm_final,
             ev_w_in=ev_w_in, ev_conv=ev_conv, gdn_a_log=gdn_a_log, gdn_dt_bias=gdn_dt_bias, gdn_gain=gdn_gain,
             ret_log_decay=ret_log_decay, ret_gain=ret_gain, ev_w_out=ev_w_out, od_w_in=od_w_in,
             s5_a_re=s5_a_re, s5_a_im=s5_a_im, s5_log_dt=s5_log_dt, s5_b_re=s5_b_re, s5_b_im=s5_b_im,
             s5_c_re=s5_c_re, s5_c_im=s5_c_im, s5_d=s5_d, s5_w_glu=s5_w_glu, gla_w_gate2=gla_w_gate2,
             gla_b_gate=gla_b_gate, gla_gain=gla_gain, od_w_out=od_w_out, moe_router=moe_router,
             moe_w_gate=moe_w_gate, moe_w_up=moe_w_up, moe_w_down=moe_w_down)
    bsz = x_prompt.shape[0]
    f32 = jnp.float32
    zero_gdn = jnp.zeros((bsz, N_EVEN, N_DIR, A_HEADS, A_DK, A_DV), f32)
    zero_ret = jnp.zeros((bsz, N_EVEN, N_DIR, B_HEADS, B_DK, B_DV), f32)
    zero_s5 = jnp.zeros((bsz, N_ODD, N_DIR, C_GROUPS, C_STATE, 2), f32)
    zero_gla = jnp.zeros((bsz, N_ODD, N_DIR, D_HEADS, D_DK, D_DV), f32)
    y_prompt, new_gdn, new_ret, new_s5, new_gla = trunk(x_prompt, c_ctx[None, :], None, zero_gdn, zero_ret,
                                                        zero_s5, zero_gla, p)
    rope_ang = grid_rope_angles(x_sample.shape[1])
    y_sample, _, _, _, _ = trunk(x_sample, c, rope_ang, state_gdn, state_ret, state_s5, state_gla, p)
    return (y_prompt, y_sample, new_gdn, new_ret, new_s5, new_gla)
```

```python
import functools
import math
from typing import NamedTuple

import numpy as np
import jax
import jax.numpy as jnp
from jax import lax
from jax.experimental import pallas as pl
from jax.experimental.pallas import tpu as pltpu

F32 = jnp.float32
BF16 = jnp.bfloat16

D_MODEL = 1024
BATCH = 32
SEQ = 256
DEPTH = 4
DEC_BATCH = 2
DEC_SEQ = 1024
GRID_W = 64
N_DIR = 2
A_HEADS, A_DK, A_DV, A_CONV, A_CHUNK = 4, 128, 128, 5, 64
B_HEADS, B_DK, B_DV, B_CHUNK = 8, 64, 64, 64
ROPE_FREQS = B_DK // 4
ROPE_BASE = 10000.0
C_GROUP, C_GROUPS, C_STATE = 16, 32, 64
C_WIDTH = C_GROUPS * C_GROUP
S5_MAX_RE = -1e-4
D_HEADS, D_DK, D_DV, D_RANK, D_TAU, D_CHUNK = 4, 64, 128, 16, 16.0, 16
N_EXPERTS = 16
EXPERT_FF = 1024
EC_CAPACITY_FACTOR = 2
RMS_EPS = 1e-6

EVEN_SPLITS = (A_HEADS * A_DK, A_HEADS * A_DK, A_HEADS * A_DV, A_HEADS * A_DV, N_DIR * A_HEADS, N_DIR * A_HEADS,
               B_HEADS * B_DK, B_HEADS * B_DK, B_HEADS * B_DV, B_HEADS * B_DV)
ODD_SPLITS = (C_WIDTH, D_HEADS * D_DK, D_HEADS * D_DK, D_HEADS * D_DV, D_HEADS * D_DV, N_DIR * D_RANK)

LANES = 128
TOKEN_TILE = 256
N_COND = 1 + DEC_BATCH
COND_ROWS = 8
MOD_COL_TILE = 1536
VMEM_LIMIT = 56 * 1024 * 1024


class Group(NamedTuple):
    n_seq: int
    seq_len: int
    is_context: bool

    @property
    def tiles_per_seq(self):
        return self.seq_len // TOKEN_TILE

    @property
    def n_tiles(self):
        return self.n_seq * self.tiles_per_seq

    @property
    def capacity(self):
        return EC_CAPACITY_FACTOR * self.seq_len // N_EXPERTS

    def cond_of_tile(self, i):
        return 0 if self.is_context else 1 + i // self.tiles_per_seq

    def cond_of_seq(self, b):
        return 0 if self.is_context else 1 + b


CTX = Group(BATCH, SEQ, True)
DEC = Group(DEC_BATCH, DEC_SEQ, False)


def _round_up(n, m):
    return (n + m - 1) // m * m


def _params(n_axes):
    return pltpu.CompilerParams(dimension_semantics=("arbitrary",) * n_axes, vmem_limit_bytes=VMEM_LIMIT)


def _mod_kernel(c_ref, w_ref, b_ref, o_ref):
    c = c_ref[...]
    s = c * jax.nn.sigmoid(c)
    o_ref[0] = jnp.dot(s.astype(BF16), w_ref[0].astype(BF16), preferred_element_type=F32) + b_ref[0]


def _modulation(cond, ada_w, ada_b):
    n_out = 6 * D_MODEL
    out = pl.pallas_call(
        _mod_kernel,
        grid=(DEPTH, n_out // MOD_COL_TILE),
        in_specs=[pl.BlockSpec((COND_ROWS, D_MODEL), lambda l, j: (0, 0)),
                  pl.BlockSpec((1, D_MODEL, MOD_COL_TILE), lambda l, j: (l, 0, j)),
                  pl.BlockSpec((1, 1, MOD_COL_TILE), lambda l, j: (l, 0, j))],
        out_specs=pl.BlockSpec((1, COND_ROWS, MOD_COL_TILE), lambda l, j: (l, 0, j)),
        out_shape=jax.ShapeDtypeStruct((DEPTH, COND_ROWS, n_out), F32),
        compiler_params=_params(2),
        name="adaln_modulation",
    )(cond, ada_w, ada_b.reshape(DEPTH, 1, n_out))
    out = out[:, :N_COND].reshape(DEPTH, N_COND, 6, D_MODEL)
    return out.transpose(1, 0, 2, 3).reshape(N_COND, DEPTH * 6, D_MODEL)


def _norm_mod(x, gain, shift, scale):
    y = x * lax.rsqrt(jnp.mean(x * x, axis=-1, keepdims=True) + RMS_EPS)
    return (y * gain) * (1.0 + scale) + shift


def _mod_row(mod_ref, layer, k):
    r = 6 * layer + k
    return mod_ref[r:r + 1, :]


def _inproj_kernel(layer, x_ref, mod_ref, gain_ref, w_ref, o_ref):
    h = _norm_mod(x_ref[...], gain_ref[...], _mod_row(mod_ref, layer, 0), _mod_row(mod_ref, layer, 1))
    o_ref[...] = jnp.dot(h.astype(BF16), w_ref[...], preferred_element_type=F32)


def _in_projection(grp, layer, x, mods, gain, w):
    n_tok = grp.n_seq * grp.seq_len
    n_out = w.shape[1]
    return pl.pallas_call(
        functools.partial(_inproj_kernel, layer),
        grid=(grp.n_tiles,),
        in_specs=[pl.BlockSpec((TOKEN_TILE, D_MODEL), lambda i: (i, 0)),
                  pl.BlockSpec((None, DEPTH * 6, D_MODEL), lambda i: (grp.cond_of_tile(i), 0, 0)),
                  pl.BlockSpec((1, D_MODEL), lambda i: (0, 0)),
                  pl.BlockSpec((D_MODEL, n_out), lambda i: (0, 0))],
        out_specs=pl.BlockSpec((TOKEN_TILE, n_out), lambda i: (i, 0)),
        out_shape=jax.ShapeDtypeStruct((n_tok, n_out), F32),
        compiler_params=_params(1),
        name="in_projection",
    )(x, mods, gain, w)


def _outproj_kernel(layer, mix_ref, x_ref, mod_ref, wout_ref, gain_ref, wr_ref, x1_ref, h2_ref, aff_ref):
    y = jnp.dot(mix_ref[...].astype(BF16), wout_ref[...], preferred_element_type=F32)
    x1 = x_ref[...] + _mod_row(mod_ref, layer, 2) * y
    x1_ref[...] = x1
    h2 = _norm_mod(x1, gain_ref[...], _mod_row(mod_ref, layer, 3), _mod_row(mod_ref, layer, 4))
    h2_ref[...] = h2.astype(BF16)
    logits = jnp.dot(h2, wr_ref[...], preferred_element_type=F32, precision=lax.Precision.HIGHEST)
    lane = lax.broadcasted_iota(jnp.int32, logits.shape, 1)
    logits = jnp.where(lane < N_EXPERTS, logits, -jnp.inf)
    e = jnp.exp(logits - jnp.max(logits, axis=-1, keepdims=True))
    aff_ref[...] = e / jnp.sum(e, axis=-1, keepdims=True)


def _out_projection(grp, layer, mix, x, mods, w_out, gain, w_router):
    n_tok = grp.n_seq * grp.seq_len
    tile = lambda width: pl.BlockSpec((TOKEN_TILE, width), lambda i: (i, 0))
    return pl.pallas_call(
        functools.partial(_outproj_kernel, layer),
        grid=(grp.n_tiles,),
        in_specs=[tile(D_MODEL), tile(D_MODEL),
                  pl.BlockSpec((None, DEPTH * 6, D_MODEL), lambda i: (grp.cond_of_tile(i), 0, 0)),
                  pl.BlockSpec((D_MODEL, D_MODEL), lambda i: (0, 0)),
                  pl.BlockSpec((1, D_MODEL), lambda i: (0, 0)),
                  pl.BlockSpec((D_MODEL, LANES), lambda i: (0, 0))],
        out_specs=[tile(D_MODEL), tile(D_MODEL), tile(LANES)],
        out_shape=[jax.ShapeDtypeStruct((n_tok, D_MODEL), F32),
                   jax.ShapeDtypeStruct((n_tok, D_MODEL), BF16),
                   jax.ShapeDtypeStruct((n_tok, LANES), F32)],
        compiler_params=_params(1),
        name="out_projection_router",
    )(mix, x, mods, w_out, gain, w_router)


def _route_kernel(n, cap, aff_ref, h2_ref, xs_ref, pt_ref, gate_ref):
    aff = aff_ref[...]
    aff_t = aff.T
    t_sub = lax.broadcasted_iota(jnp.int32, (n, n), 0)
    t_lane = lax.broadcasted_iota(jnp.int32, (n, n), 1)
    earlier = t_sub < t_lane
    sel_rows = []
    for e in range(N_EXPERTS):
        col = aff[:, e:e + 1]
        row = aff_t[e:e + 1, :]
        beats = (col > row) | ((col == row) & earlier)
        rank = jnp.sum(jnp.where(beats, 1.0, 0.0), axis=0, keepdims=True)
        sel_rows.append(jnp.where(rank < cap, 1.0, 0.0))
    sel = jnp.concatenate(sel_rows, axis=0)
    pos = jnp.dot(sel.astype(BF16), jnp.where(earlier, 1.0, 0.0).astype(BF16), preferred_element_type=F32)
    e_idx = lax.broadcasted_iota(jnp.int32, (N_EXPERTS, n), 0)
    slot = jnp.where(sel > 0.0, pos.astype(jnp.int32) + e_idx * cap, -1)
    slot_pad = jnp.concatenate([slot, jnp.full((LANES - N_EXPERTS, n), -1, jnp.int32)], axis=0)
    slot_t = slot_pad.astype(F32).T.astype(jnp.int32)
    s_lane = lax.broadcasted_iota(jnp.int32, (n, N_EXPERTS * cap), 1)
    pt = jnp.zeros((n, N_EXPERTS * cap), F32)
    c_sub = lax.broadcasted_iota(jnp.int32, (cap, n), 0)
    h2 = h2_ref[...]
    for e in range(N_EXPERTS):
        pt = pt + jnp.where(slot_t[:, e:e + 1] == s_lane, 1.0, 0.0)
        p_e = jnp.where(slot[e:e + 1, :] == c_sub + e * cap, 1.0, 0.0)
        xs_ref[e] = jnp.dot(p_e.astype(BF16), h2, preferred_element_type=F32).astype(BF16)
        gate = jnp.sum(p_e * aff_t[e:e + 1, :], axis=1, keepdims=True)
        gate_ref[e] = jnp.broadcast_to(gate, (cap, LANES))
    pt_ref[...] = pt.astype(BF16)


def _route(grp, aff, h2):
    n, cap = grp.seq_len, grp.capacity
    slots = N_EXPERTS * cap
    return pl.pallas_call(
        functools.partial(_route_kernel, n, cap),
        grid=(grp.n_seq,),
        in_specs=[pl.BlockSpec((n, LANES), lambda b: (b, 0)),
                  pl.BlockSpec((n, D_MODEL), lambda b: (b, 0))],
        out_specs=[pl.BlockSpec((N_EXPERTS, cap, D_MODEL), lambda b: (0, b, 0)),
                   pl.BlockSpec((n, slots), lambda b: (b, 0)),
                   pl.BlockSpec((N_EXPERTS, cap, LANES), lambda b: (0, b, 0))],
        out_shape=[jax.ShapeDtypeStruct((N_EXPERTS, grp.n_seq * cap, D_MODEL), BF16),
                   jax.ShapeDtypeStruct((grp.n_seq * n, slots), BF16),
                   jax.ShapeDtypeStruct((N_EXPERTS, grp.n_seq * cap, LANES), F32)],
        compiler_params=_params(1),
        name="expert_choice_route",
    )(aff, h2)


FF_TILE = 512


def _expert_kernel(xc_ref, xd_ref, gc_ref, gd_ref, wg_ref, wu_ref, wd_ref, yc_ref, yd_ref, accc_ref, accd_ref):
    f = pl.program_id(1)
    wg = wg_ref[...].astype(BF16)
    wu = wu_ref[...].astype(BF16)
    wd = wd_ref[...].astype(BF16)

    def ffn(x_ref, acc_ref):
        x = x_ref[...]
        a = jnp.dot(x, wg, preferred_element_type=F32)
        u = jnp.dot(x, wu, preferred_element_type=F32)
        hid = (a * jax.nn.sigmoid(a)) * u
        y = jnp.dot(hid.astype(BF16), wd, preferred_element_type=F32)

        @pl.when(f == 0)
        def _():
            acc_ref[...] = y

        @pl.when(f != 0)
        def _():
            acc_ref[...] += y

    ffn(xc_ref, accc_ref)
    ffn(xd_ref, accd_ref)

    @pl.when(f == pl.num_programs(1) - 1)
    def _():
        yc_ref[...] = (accc_ref[...] * gc_ref[:, 0:1]).astype(BF16)
        yd_ref[...] = (accd_ref[...] * gd_ref[:, 0:1]).astype(BF16)


def _experts(xs_c, xs_d, gate_c, gate_d, w_gate, w_up, w_down):
    rc, rd = xs_c.shape[1], xs_d.shape[1]
    per_e = lambda rows, width: pl.BlockSpec((None, rows, width), lambda e, f: (e, 0, 0))
    return pl.pallas_call(
        _expert_kernel,
        grid=(N_EXPERTS, EXPERT_FF // FF_TILE),
        in_specs=[per_e(rc, D_MODEL), per_e(rd, D_MODEL), per_e(rc, LANES), per_e(rd, LANES),
                  pl.BlockSpec((None, D_MODEL, FF_TILE), lambda e, f: (e, 0, f)),
                  pl.BlockSpec((None, D_MODEL, FF_TILE), lambda e, f: (e, 0, f)),
                  pl.BlockSpec((None, FF_TILE, D_MODEL), lambda e, f: (e, f, 0))],
        out_specs=[per_e(rc, D_MODEL), per_e(rd, D_MODEL)],
        out_shape=[jax.ShapeDtypeStruct(xs_c.shape, BF16), jax.ShapeDtypeStruct(xs_d.shape, BF16)],
        scratch_shapes=[pltpu.VMEM((rc, D_MODEL), F32), pltpu.VMEM((rd, D_MODEL), F32)],
        compiler_params=_params(2),
        name="expert_swiglu",
    )(xs_c, xs_d, gate_c, gate_d, w_gate, w_up, w_down)


def _combine_kernel(layer, cap, final, x1_ref, mod_ref, pt_ref, ys_ref, gain_ref, x2_ref):
    ys = ys_ref[...].reshape(N_EXPERTS * cap, D_MODEL)
    y = jnp.dot(pt_ref[...], ys, preferred_element_type=F32)
    x2 = x1_ref[...] + _mod_row(mod_ref, layer, 5) * y
    if final:
        x2 = x2 * lax.rsqrt(jnp.mean(x2 * x2, axis=-1, keepdims=True) + RMS_EPS) * gain_ref[...]
    x2_ref[...] = x2


def _combine(grp, layer, final, x1, mods, pt, ys, gain_final):
    n, cap = grp.seq_len, grp.capacity
    return pl.pallas_call(
        functools.partial(_combine_kernel, layer, cap, final),
        grid=(grp.n_seq,),
        in_specs=[pl.BlockSpec((n, D_MODEL), lambda b: (b, 0)),
                  pl.BlockSpec((None, DEPTH * 6, D_MODEL), lambda b: (grp.cond_of_seq(b), 0, 0)),
                  pl.BlockSpec((n, N_EXPERTS * cap), lambda b: (b, 0)),
                  pl.BlockSpec((N_EXPERTS, cap, D_MODEL), lambda b: (0, b, 0)),
                  pl.BlockSpec((1, D_MODEL), lambda b: (0, 0))],
        out_specs=pl.BlockSpec((n, D_MODEL), lambda b: (b, 0)),
        out_shape=jax.ShapeDtypeStruct(x1.shape, F32),
        compiler_params=_params(1),
        name="expert_combine",
    )(x1, mods, pt, ys, gain_final)


def _split_cols(t, sizes):
    return jnp.split(t, np.cumsum(sizes)[:-1].tolist(), axis=-1)


def _to_heads(t, n_heads):
    b, l, _ = t.shape
    return t.reshape(b, l, n_heads, -1).transpose(0, 2, 1, 3)


def _from_heads(t):
    b, h, l, d = t.shape
    return t.transpose(0, 2, 1, 3).reshape(b, l, h * d)


def _dir_heads(t, n_heads):
    b, l, _ = t.shape
    return t.reshape(b, l, N_DIR, n_heads, -1).transpose(2, 0, 3, 1, 4)


def _flip_seq(t):
    return jnp.flip(t, axis=2)


def _l2_normalize(t):
    return t * lax.rsqrt(jnp.sum(t * t, axis=-1, keepdims=True) + 1e-6)


def _head_rms_norm(o, gain):
    return o * lax.rsqrt(jnp.mean(o * o, axis=-1, keepdims=True) + RMS_EPS) * gain.astype(F32)


def _head_layer_norm(o):
    mu = jnp.mean(o, axis=-1, keepdims=True)
    var = jnp.mean(jnp.square(o - mu), axis=-1, keepdims=True)
    return (o - mu) * lax.rsqrt(var + RMS_EPS)


def _short_conv_silu(x, w):
    pad = w.shape[0] // 2
    y = lax.conv_general_dilated(x, w[:, None, :].astype(x.dtype), window_strides=(1,), padding=[(pad, pad)],
                                 dimension_numbers=('NWC', 'WIO', 'NWC'), feature_group_count=x.shape[-1])
    return jax.nn.silu(y)


def _grid_rope_angles(n_tokens):
    rows = n_tokens // GRID_W
    row = jnp.repeat(jnp.arange(rows, dtype=F32), GRID_W)
    col = jnp.tile(jnp.arange(GRID_W, dtype=F32), rows)
    freq = ROPE_BASE ** (-jnp.arange(ROPE_FREQS, dtype=F32) / ROPE_FREQS)
    return jnp.stack([row[:, None] * freq, col[:, None] * freq], axis=1)


def _apply_rope_2d(x, ang):
    xr = x.reshape(*x.shape[:-1], 2, 2, ROPE_FREQS)
    x1, x2 = xr[..., 0, :], xr[..., 1, :]
    cos, sin = jnp.cos(ang), jnp.sin(ang)
    return jnp.stack([x1 * cos - x2 * sin, x1 * sin + x2 * cos], axis=-2).reshape(x.shape)


def _gated_delta_chunked(q, k, v, beta, g, s0):
    bsz, nh, n_tok, dk = q.shape
    dv = v.shape[-1]
    cs = A_CHUNK
    nc = n_tok // cs
    q = (q * dk ** -0.5).reshape(bsz, nh, nc, cs, dk)
    k = k.reshape(bsz, nh, nc, cs, dk)
    v = v.reshape(bsz, nh, nc, cs, dv)
    beta = beta.reshape(bsz, nh, nc, cs, 1)
    g = jnp.cumsum(g.reshape(bsz, nh, nc, cs), axis=-1)
    pos = jnp.arange(cs)
    tri = pos[:, None] >= pos[None, :]
    strict = pos[:, None] > pos[None, :]
    decay = jnp.exp(jnp.where(tri, g[..., :, None] - g[..., None, :], -jnp.inf))
    k_beta = k * beta
    lmat = jnp.where(strict, jnp.einsum('bhnid,bhnjd->bhnij', k_beta, k) * decay, 0.0)
    eye = jnp.eye(cs, dtype=q.dtype)
    t_inv = lax.linalg.triangular_solve(eye + lmat, jnp.broadcast_to(eye, lmat.shape), left_side=True,
                                        lower=True, unit_diagonal=True)
    u = jnp.einsum('bhnij,bhnje->bhnie', t_inv, v * beta)
    w = jnp.einsum('bhnij,bhnjd->bhnid', t_inv, k_beta * jnp.exp(g)[..., None])
    attn = jnp.einsum('bhnid,bhnjd->bhnij', q, k) * decay
    q_in = q * jnp.exp(g)[..., None]
    g_last = g[..., -1]
    k_out = k * jnp.exp(g_last[..., None] - g)[..., None]

    def step(s, xs):
        q_c, k_c, u_c, w_c, a_c, gl = xs
        v_new = u_c - jnp.einsum('bhid,bhde->bhie', w_c, s)
        o = jnp.einsum('bhid,bhde->bhie', q_c, s) + jnp.einsum('bhij,bhje->bhie', a_c, v_new)
        s = s * jnp.exp(gl)[..., None, None] + jnp.einsum('bhjd,bhje->bhde', k_c, v_new)
        return s, o

    xs = tuple(jnp.moveaxis(t, 2, 0) for t in (q_in, k_out, u, w, attn, g_last))
    s_fin, o = lax.scan(step, s0, xs)
    return jnp.moveaxis(o, 0, 2).reshape(bsz, nh, n_tok, dv), s_fin


def _retention_chunked(q, k, v, log_gamma, s0):
    bsz, nh, n_tok, dk = q.shape
    dv = v.shape[-1]
    cs = B_CHUNK
    nc = n_tok // cs
    q = (q * dk ** -0.5).reshape(bsz, nh, nc, cs, dk)
    k = k.reshape(bsz, nh, nc, cs, dk)
    v = v.reshape(bsz, nh, nc, cs, dv)
    pos = jnp.arange(cs, dtype=F32)
    tri = pos[:, None] >= pos[None, :]
    lg = log_gamma[:, None, None]
    dmat = jnp.exp(jnp.where(tri, (pos[:, None] - pos[None, :]) * lg, -jnp.inf))
    xi = jnp.exp((pos + 1.0) * log_gamma[:, None])
    zeta = jnp.exp((cs - 1.0 - pos) * log_gamma[:, None])
    g_chunk = jnp.exp(cs * log_gamma)
    scores = jnp.einsum('bhnid,bhnjd->bhnij', q, k) * dmat[:, None]
    intra = jnp.einsum('bhnij,bhnje->bhnie', scores, v)
    k_sc = k * zeta[:, None, :, None]

    def step(s, xs):
        q_c, k_c, v_c = xs
        o = jnp.einsum('bhid,bhde->bhie', q_c, s) * xi[:, :, None]
        s = s * g_chunk[:, None, None] + jnp.einsum('bhjd,bhje->bhde', k_c, v_c)
        return s, o

    xs = tuple(jnp.moveaxis(t, 2, 0) for t in (q, k_sc, v))
    s_fin, cross = lax.scan(step, s0, xs)
    o = intra + jnp.moveaxis(cross, 0, 2)
    return o.reshape(bsz, nh, n_tok, dv), s_fin


def _gla_chunked(q, k, v, log_a, s0):
    bsz, nh, n_tok, dk = q.shape
    dv = v.shape[-1]
    cs = D_CHUNK
    nc = n_tok // cs
    q = (q * dk ** -0.5).reshape(bsz, nh, nc, cs, dk)
    k = k.reshape(bsz, nh, nc, cs, dk)
    v = v.reshape(bsz, nh, nc, cs, dv)
    b = jnp.cumsum(log_a.reshape(bsz, nh, nc, cs, dk), axis=3)
    pos = jnp.arange(cs)
    tri = pos[:, None] >= pos[None, :]
    decay = jnp.exp(jnp.where(tri[:, :, None], b[..., :, None, :] - b[..., None, :, :], -jnp.inf))
    scores = jnp.einsum('bhnid,bhnjd,bhnijd->bhnij', q, k, decay)
    intra = jnp.einsum('bhnij,bhnje->bhnie', scores, v)
    b_last = b[:, :, :, -1]
    q_in = q * jnp.exp(b)
    k_out = k * jnp.exp(b_last[:, :, :, None] - b)

    def step(s, xs):
        q_c, k_c, v_c, bl = xs
        o = jnp.einsum('bhid,bhde->bhie', q_c, s)
        s = s * jnp.exp(bl)[..., None] + jnp.einsum('bhjd,bhje->bhde', k_c, v_c)
        return s, o

    xs = tuple(jnp.moveaxis(t, 2, 0) for t in (q_in, k_out, v, b_last))
    s_fin, cross = lax.scan(step, s0, xs)
    o = intra + jnp.moveaxis(cross, 0, 2)
    return o.reshape(bsz, nh, n_tok, dv), s_fin


def _linear_recurrence_combine(e1, e2):
    a1, b1 = e1
    a2, b2 = e2
    return a1 * a2, a2 * b1 + b2


def _s5_scan(u, a_re, a_im, log_dt, b_re, b_im, s0):
    lam = lax.complex(jnp.minimum(a_re.astype(F32), S5_MAX_RE), a_im.astype(F32))
    dt = jnp.exp(log_dt.astype(F32))[:, None]
    lam_bar = jnp.exp(lam * dt)
    b_bar = ((lam_bar - 1.0) / lam)[..., None] * lax.complex(b_re.astype(F32), b_im.astype(F32))
    bu = jnp.einsum('gpi,blgi->blgp', b_bar, u.astype(jnp.complex64))
    bu = bu.at[:, 0].add(lam_bar * s0)
    decay = jnp.broadcast_to(lam_bar, bu.shape)
    _, states = lax.associative_scan(_linear_recurrence_combine, (decay, bu), axis=1)
    return states, states[:, -1]


def _even_mixer(proj, conv_w, a_log, dt_bias, gdn_gain, ret_log_decay, ret_gain, s0_gdn, s0_ret, rope_ang):
    qa, ka, va, za, ba, aa, qb, kb, vb, gb = _split_cols(proj, EVEN_SPLITS)
    qkv = _short_conv_silu(jnp.concatenate([qa, ka, va], axis=-1), conv_w)
    qa, ka, va = _split_cols(qkv, (A_HEADS * A_DK, A_HEADS * A_DK, A_HEADS * A_DV))
    qa = _l2_normalize(_to_heads(qa, A_HEADS))
    ka = _l2_normalize(_to_heads(ka, A_HEADS))
    va = _to_heads(va, A_HEADS)
    beta = jax.nn.sigmoid(_dir_heads(ba, A_HEADS)[..., 0])
    g = -jnp.exp(a_log)[:, None, :, None] * jax.nn.softplus(
        _dir_heads(aa, A_HEADS)[..., 0] + dt_bias[:, None, :, None])
    o_f, s_f = _gated_delta_chunked(qa, ka, va, beta[0], g[0], s0_gdn[:, 0])
    o_b, s_b = _gated_delta_chunked(_flip_seq(qa), _flip_seq(ka), _flip_seq(va), _flip_seq(beta[1]),
                                    _flip_seq(g[1]), s0_gdn[:, 1])
    out_a = _from_heads(_head_rms_norm(o_f + _flip_seq(o_b), gdn_gain)) * jax.nn.silu(za)
    qb = _to_heads(qb, B_HEADS)
    kb = _to_heads(kb, B_HEADS)
    vb = _to_heads(vb, B_HEADS)
    if rope_ang is not None:
        qb = _apply_rope_2d(qb, rope_ang)
        kb = _apply_rope_2d(kb, rope_ang)
    log_gamma = -jnp.exp(ret_log_decay)
    r_f, t_f = _retention_chunked(qb, kb, vb, log_gamma[0], s0_ret[:, 0])
    r_b, t_b = _retention_chunked(_flip_seq(qb), _flip_seq(kb), _flip_seq(vb), log_gamma[1], s0_ret[:, 1])
    out_b = (_from_heads(_head_layer_norm(r_f + _flip_seq(r_b))) * ret_gain) * jax.nn.silu(gb)
    return jnp.concatenate([out_a, out_b], axis=-1), jnp.stack([s_f, s_b], axis=1), jnp.stack([t_f, t_b], axis=1)


def _odd_mixer(proj, a_re, a_im, log_dt, b_re, b_im, c_re, c_im, d_skip, w_glu, w_gate2, b_gate, gla_gain,
               s0_s5, s0_gla):
    bsz, n_tok, _ = proj.shape
    u, qd, kd, vd, gd, alr = _split_cols(proj, ODD_SPLITS)
    uf = u.reshape(bsz, n_tok, C_GROUPS, C_GROUP)
    s0 = lax.complex(s0_s5[..., 0], s0_s5[..., 1])
    st_f, fin_f = _s5_scan(uf, a_re[0], a_im[0], log_dt[0], b_re, b_im, s0[:, 0])
    st_b, fin_b = _s5_scan(jnp.flip(uf, axis=1), a_re[1], a_im[1], log_dt[1], b_re, b_im, s0[:, 1])
    states = st_f + jnp.flip(st_b, axis=1)
    cmat = lax.complex(c_re, c_im)
    y = jnp.einsum('gip,blgp->blgi', cmat, states).real + d_skip.reshape(C_GROUPS, C_GROUP) * uf
    y = jax.nn.gelu(y.reshape(bsz, n_tok, C_WIDTH))
    out_c = y * jax.nn.sigmoid(y @ w_glu)
    fin = jnp.stack([fin_f, fin_b], axis=1)
    s5_state = jnp.stack([fin.real, fin.imag], axis=-1)
    q = _to_heads(qd, D_HEADS)
    k = _to_heads(kd, D_HEADS)
    v = _to_heads(vd, D_HEADS)
    alr = alr.reshape(bsz, n_tok, N_DIR, D_RANK)
    logits = jnp.einsum('blnr,nrk->nblk', alr, w_gate2) + b_gate[:, None, None, :]
    log_a = (jax.nn.log_sigmoid(logits) / D_TAU).reshape(N_DIR, bsz, n_tok, D_HEADS, D_DK).transpose(0, 1, 3, 2, 4)
    o_f, s_f = _gla_chunked(q, k, v, log_a[0], s0_gla[:, 0])
    o_b, s_b = _gla_chunked(_flip_seq(q), _flip_seq(k), _flip_seq(v), _flip_seq(log_a[1]), s0_gla[:, 1])
    out_d = _from_heads(_head_rms_norm(o_f + _flip_seq(o_b), gla_gain)) * jax.nn.silu(gd)
    return jnp.concatenate([out_c, out_d], axis=-1), s5_state, jnp.stack([s_f, s_b], axis=1)


def _pad_cols(w, n):
    return jnp.pad(w, ((0, 0), (0, n - w.shape[1])))


def kernel(x_prompt, x_sample, state_gdn, state_ret, state_s5, state_gla, c, c_ctx, ada_w, ada_b, norm_mix,
           norm_ffn, norm_final, ev_w_in, ev_conv, gdn_a_log, gdn_dt_bias, gdn_gain, ret_log_decay, ret_gain,
           ev_w_out, od_w_in, s5_a_re, s5_a_im, s5_log_dt, s5_b_re, s5_b_im, s5_c_re, s5_c_im, s5_d, s5_w_glu,
           gla_w_gate2, gla_b_gate, gla_gain, od_w_out, moe_router, moe_w_gate, moe_w_up, moe_w_down):
    cond = jnp.zeros((COND_ROWS, D_MODEL), F32).at[0].set(c_ctx).at[1:N_COND].set(c)
    mods = _modulation(cond, ada_w, ada_b)
    rope_ang = _grid_rope_angles(DEC_SEQ)
    even_in, odd_in = sum(EVEN_SPLITS), sum(ODD_SPLITS)

    xs = {CTX: x_prompt.reshape(BATCH * SEQ, D_MODEL), DEC: x_sample.reshape(DEC_BATCH * DEC_SEQ, D_MODEL)}
    zero = lambda *shape: jnp.zeros(shape, F32)
    st = {CTX: (zero(BATCH, N_DIR, A_HEADS, A_DK, A_DV), zero(BATCH, N_DIR, B_HEADS, B_DK, B_DV),
                zero(BATCH, N_DIR, C_GROUPS, C_STATE, 2), zero(BATCH, N_DIR, D_HEADS, D_DK, D_DV))}
    new_states = {"gdn": [], "ret": [], "s5": [], "gla": []}

    for layer in range(DEPTH):
        j = layer // 2
        even = layer % 2 == 0
        w_in = ev_w_in[j] if even else od_w_in[j]
        n_in = even_in if even else odd_in
        w_in = _pad_cols(w_in, _round_up(n_in, LANES)).astype(BF16)
        w_out = (ev_w_out[j] if even else od_w_out[j]).astype(BF16)
        w_router = _pad_cols(moe_router[layer], LANES)
        routed = {}
        for grp in (CTX, DEC):
            x = xs[grp]
            proj = _in_projection(grp, layer, x, mods, norm_mix[layer][None], w_in)
            proj = proj[:, :n_in].reshape(grp.n_seq, grp.seq_len, n_in)
            if even:
                s0_a, s0_b = (st[CTX][0], st[CTX][1]) if grp.is_context else (state_gdn[:, j], state_ret[:, j])
                mix, s_a, s_b = _even_mixer(proj, ev_conv[j], gdn_a_log[j], gdn_dt_bias[j], gdn_gain[j],
                                            ret_log_decay[j], ret_gain[j], s0_a, s0_b,
                                            None if grp.is_context else rope_ang)
                if grp.is_context:
                    new_states["gdn"].append(s_a)
                    new_states["ret"].append(s_b)
            else:
                s0_c, s0_d = (st[CTX][2], st[CTX][3]) if grp.is_context else (state_s5[:, j], state_gla[:, j])
                mix, s_c, s_d = _odd_mixer(proj, s5_a_re[j], s5_a_im[j], s5_log_dt[j], s5_b_re[j], s5_b_im[j],
                                           s5_c_re[j], s5_c_im[j], s5_d[j], s5_w_glu[j], gla_w_gate2[j],
                                           gla_b_gate[j], gla_gain[j], s0_c, s0_d)
                if grp.is_context:
                    new_states["s5"].append(s_c)
                    new_states["gla"].append(s_d)
            mix = mix.reshape(grp.n_seq * grp.seq_len, D_MODEL)
            x1, h2, aff = _out_projection(grp, layer, mix, x, mods, w_out, norm_ffn[layer][None], w_router)
            xs_g, pt, gate = _route(grp, aff, h2)
            routed[grp] = (x1, xs_g, pt, gate)
        ys_c, ys_d = _experts(routed[CTX][1], routed[DEC][1], routed[CTX][3], routed[DEC][3],
                              moe_w_gate[layer], moe_w_up[layer], moe_w_down[layer])
        for grp, ys in ((CTX, ys_c), (DEC, ys_d)):
            x1, _, pt, _ = routed[grp]
            xs[grp] = _combine(grp, layer, layer == DEPTH - 1, x1, mods, pt, ys, norm_final[None])

    y_prompt = xs[CTX].reshape(BATCH, SEQ, D_MODEL)
    y_sample = xs[DEC].reshape(DEC_BATCH, DEC_SEQ, D_MODEL)
    return (y_prompt, y_sample, jnp.stack(new_states["gdn"], axis=1), jnp.stack(new_states["ret"], axis=1),
            jnp.stack(new_states["s5"], axis=1), jnp.stack(new_states["gla"], axis=1))
```

```python
import functools
import math
from typing import NamedTuple

import numpy as np
import jax
import jax.numpy as jnp
from jax import lax
from jax.experimental import pallas as pl
from jax.experimental.pallas import tpu as pltpu

F32 = jnp.float32
BF16 = jnp.bfloat16

D_MODEL = 1024
BATCH = 32
SEQ = 256
DEPTH = 4
DEC_BATCH = 2
DEC_SEQ = 1024
GRID_W = 64
N_DIR = 2
A_HEADS, A_DK, A_DV, A_CONV, A_CHUNK = 4, 128, 128, 5, 64
B_HEADS, B_DK, B_DV, B_CHUNK = 8, 64, 64, 64
ROPE_FREQS = B_DK // 4
ROPE_BASE = 10000.0
C_GROUP, C_GROUPS, C_STATE = 16, 32, 64
C_WIDTH = C_GROUPS * C_GROUP
S5_MAX_RE = -1e-4
D_HEADS, D_DK, D_DV, D_RANK, D_TAU, D_CHUNK = 4, 64, 128, 16, 16.0, 16
N_EXPERTS = 16
EXPERT_FF = 1024
EC_CAPACITY_FACTOR = 2
RMS_EPS = 1e-6

EVEN_SPLITS = (A_HEADS * A_DK, A_HEADS * A_DK, A_HEADS * A_DV, A_HEADS * A_DV, N_DIR * A_HEADS, N_DIR * A_HEADS,
               B_HEADS * B_DK, B_HEADS * B_DK, B_HEADS * B_DV, B_HEADS * B_DV)
ODD_SPLITS = (C_WIDTH, D_HEADS * D_DK, D_HEADS * D_DK, D_HEADS * D_DV, D_HEADS * D_DV, N_DIR * D_RANK)

LANES = 128
TOKEN_TILE = 256
N_COND = 1 + DEC_BATCH
COND_ROWS = 8
MOD_COL_TILE = 1536
VMEM_LIMIT = 56 * 1024 * 1024


class Group(NamedTuple):
    n_seq: int
    seq_len: int
    is_context: bool

    @property
    def tiles_per_seq(self):
        return self.seq_len // TOKEN_TILE

    @property
    def n_tiles(self):
        return self.n_seq * self.tiles_per_seq

    @property
    def capacity(self):
        return EC_CAPACITY_FACTOR * self.seq_len // N_EXPERTS

    def cond_of_tile(self, i):
        return 0 if self.is_context else 1 + i // self.tiles_per_seq

    def cond_of_seq(self, b):
        return 0 if self.is_context else 1 + b


CTX = Group(BATCH, SEQ, True)
DEC = Group(DEC_BATCH, DEC_SEQ, False)


def _round_up(n, m):
    return (n + m - 1) // m * m


def _params(n_axes):
    return pltpu.CompilerParams(dimension_semantics=("arbitrary",) * n_axes, vmem_limit_bytes=VMEM_LIMIT)


def _mod_kernel(c_ref, w_ref, b_ref, o_ref):
    c = c_ref[...]
    s = c * jax.nn.sigmoid(c)
    o_ref[0] = jnp.dot(s.astype(BF16), w_ref[0].astype(BF16), preferred_element_type=F32) + b_ref[0]


def _modulation(cond, ada_w, ada_b):
    n_out = 6 * D_MODEL
    out = pl.pallas_call(
        _mod_kernel,
        grid=(DEPTH, n_out // MOD_COL_TILE),
        in_specs=[pl.BlockSpec((COND_ROWS, D_MODEL), lambda l, j: (0, 0)),
                  pl.BlockSpec((1, D_MODEL, MOD_COL_TILE), lambda l, j: (l, 0, j)),
                  pl.BlockSpec((1, 1, MOD_COL_TILE), lambda l, j: (l, 0, j))],
        out_specs=pl.BlockSpec((1, COND_ROWS, MOD_COL_TILE), lambda l, j: (l, 0, j)),
        out_shape=jax.ShapeDtypeStruct((DEPTH, COND_ROWS, n_out), F32),
        compiler_params=_params(2),
        name="adaln_modulation",
    )(cond, ada_w, ada_b.reshape(DEPTH, 1, n_out))
    out = out[:, :N_COND].reshape(DEPTH, N_COND, 6, D_MODEL)
    return out.transpose(1, 0, 2, 3).reshape(N_COND, DEPTH * 6, D_MODEL)


def _norm_mod(x, gain, shift, scale):
    y = x * lax.rsqrt(jnp.mean(x * x, axis=-1, keepdims=True) + RMS_EPS)
    return (y * gain) * (1.0 + scale) + shift


def _mod_row(mod_ref, layer, k):
    r = 6 * layer + k
    return mod_ref[r:r + 1, :]


def _inproj_kernel(layer, x_ref, mod_ref, gain_ref, w_ref, o_ref):
    h = _norm_mod(x_ref[...], gain_ref[...], _mod_row(mod_ref, layer, 0), _mod_row(mod_ref, layer, 1))
    o_ref[...] = jnp.dot(h.astype(BF16), w_ref[...], preferred_element_type=F32)


def _in_projection(grp, layer, x, mods, gain, w):
    n_tok = grp.n_seq * grp.seq_len
    n_out = w.shape[1]
    return pl.pallas_call(
        functools.partial(_inproj_kernel, layer),
        grid=(grp.n_tiles,),
        in_specs=[pl.BlockSpec((TOKEN_TILE, D_MODEL), lambda i: (i, 0)),
                  pl.BlockSpec((None, DEPTH * 6, D_MODEL), lambda i: (grp.cond_of_tile(i), 0, 0)),
                  pl.BlockSpec((1, D_MODEL), lambda i: (0, 0)),
                  pl.BlockSpec((D_MODEL, n_out), lambda i: (0, 0))],
        out_specs=pl.BlockSpec((TOKEN_TILE, n_out), lambda i: (i, 0)),
        out_shape=jax.ShapeDtypeStruct((n_tok, n_out), F32),
        compiler_params=_params(1),
        name="in_projection",
    )(x, mods, gain, w)


def _outproj_kernel(layer, mix_ref, x_ref, mod_ref, wout_ref, gain_ref, wr_ref, x1_ref, h2_ref, aff_ref):
    y = jnp.dot(mix_ref[...].astype(BF16), wout_ref[...], preferred_element_type=F32)
    x1 = x_ref[...] + _mod_row(mod_ref, layer, 2) * y
    x1_ref[...] = x1
    h2 = _norm_mod(x1, gain_ref[...], _mod_row(mod_ref, layer, 3), _mod_row(mod_ref, layer, 4))
    h2_ref[...] = h2.astype(BF16)
    logits = jnp.dot(h2, wr_ref[...], preferred_element_type=F32, precision=lax.Precision.HIGHEST)
    lane = lax.broadcasted_iota(jnp.int32, logits.shape, 1)
    logits = jnp.where(lane < N_EXPERTS, logits, -jnp.inf)
    e = jnp.exp(logits - jnp.max(logits, axis=-1, keepdims=True))
    aff_ref[...] = e / jnp.sum(e, axis=-1, keepdims=True)


def _out_projection(grp, layer, mix, x, mods, w_out, gain, w_router):
    n_tok = grp.n_seq * grp.seq_len
    tile = lambda width: pl.BlockSpec((TOKEN_TILE, width), lambda i: (i, 0))
    return pl.pallas_call(
        functools.partial(_outproj_kernel, layer),
        grid=(grp.n_tiles,),
        in_specs=[tile(D_MODEL), tile(D_MODEL),
                  pl.BlockSpec((None, DEPTH * 6, D_MODEL), lambda i: (grp.cond_of_tile(i), 0, 0)),
                  pl.BlockSpec((D_MODEL, D_MODEL), lambda i: (0, 0)),
                  pl.BlockSpec((1, D_MODEL), lambda i: (0, 0)),
                  pl.BlockSpec((D_MODEL, LANES), lambda i: (0, 0))],
        out_specs=[tile(D_MODEL), tile(D_MODEL), tile(LANES)],
        out_shape=[jax.ShapeDtypeStruct((n_tok, D_MODEL), F32),
                   jax.ShapeDtypeStruct((n_tok, D_MODEL), BF16),
                   jax.ShapeDtypeStruct((n_tok, LANES), F32)],
        compiler_params=_params(1),
        name="out_projection_router",
    )(mix, x, mods, w_out, gain, w_router)


def _route_kernel(n, cap, aff_ref, h2_ref, xs_ref, pt_ref, gate_ref):
    aff = aff_ref[...]
    aff_t = aff.T
    t_sub = lax.broadcasted_iota(jnp.int32, (n, n), 0)
    t_lane = lax.broadcasted_iota(jnp.int32, (n, n), 1)
    earlier = t_sub < t_lane
    sel_rows = []
    for e in range(N_EXPERTS):
        col = aff[:, e:e + 1]
        row = aff_t[e:e + 1, :]
        beats = (col > row) | ((col == row) & earlier)
        rank = jnp.sum(jnp.where(beats, 1.0, 0.0), axis=0, keepdims=True)
        sel_rows.append(jnp.where(rank < cap, 1.0, 0.0))
    sel = jnp.concatenate(sel_rows, axis=0)
    pos = jnp.dot(sel.astype(BF16), jnp.where(earlier, 1.0, 0.0).astype(BF16), preferred_element_type=F32)
    e_idx = lax.broadcasted_iota(jnp.int32, (N_EXPERTS, n), 0)
    slot = jnp.where(sel > 0.0, pos.astype(jnp.int32) + e_idx * cap, -1)
    slot_pad = jnp.concatenate([slot, jnp.full((LANES - N_EXPERTS, n), -1, jnp.int32)], axis=0)
    slot_t = slot_pad.astype(F32).T.astype(jnp.int32)
    s_lane = lax.broadcasted_iota(jnp.int32, (n, N_EXPERTS * cap), 1)
    pt = jnp.zeros((n, N_EXPERTS * cap), F32)
    c_sub = lax.broadcasted_iota(jnp.int32, (cap, n), 0)
    h2 = h2_ref[...]
    for e in range(N_EXPERTS):
        pt = pt + jnp.where(slot_t[:, e:e + 1] == s_lane, 1.0, 0.0)
        p_e = jnp.where(slot[e:e + 1, :] == c_sub + e * cap, 1.0, 0.0)
        xs_ref[e] = jnp.dot(p_e.astype(BF16), h2, preferred_element_type=F32).astype(BF16)
        gate = jnp.sum(p_e * aff_t[e:e + 1, :], axis=1, keepdims=True)
        gate_ref[e] = jnp.broadcast_to(gate, (cap, LANES))
    pt_ref[...] = pt.astype(BF16)


def _route(grp, aff, h2):
    n, cap = grp.seq_len, grp.capacity
    slots = N_EXPERTS * cap
    return pl.pallas_call(
        functools.partial(_route_kernel, n, cap),
        grid=(grp.n_seq,),
        in_specs=[pl.BlockSpec((n, LANES), lambda b: (b, 0)),
                  pl.BlockSpec((n, D_MODEL), lambda b: (b, 0))],
        out_specs=[pl.BlockSpec((N_EXPERTS, cap, D_MODEL), lambda b: (0, b, 0)),
                   pl.BlockSpec((n, slots), lambda b: (b, 0)),
                   pl.BlockSpec((N_EXPERTS, cap, LANES), lambda b: (0, b, 0))],
        out_shape=[jax.ShapeDtypeStruct((N_EXPERTS, grp.n_seq * cap, D_MODEL), BF16),
                   jax.ShapeDtypeStruct((grp.n_seq * n, slots), BF16),
                   jax.ShapeDtypeStruct((N_EXPERTS, grp.n_seq * cap, LANES), F32)],
        compiler_params=_params(1),
        name="expert_choice_route",
    )(aff, h2)


FF_TILE = 512


def _expert_kernel(xc_ref, xd_ref, gc_ref, gd_ref, wg_ref, wu_ref, wd_ref, yc_ref, yd_ref, accc_ref, accd_ref):
    f = pl.program_id(1)
    wg = wg_ref[...].astype(BF16)
    wu = wu_ref[...].astype(BF16)
    wd = wd_ref[...].astype(BF16)

    def ffn(x_ref, acc_ref):
        x = x_ref[...]
        a = jnp.dot(x, wg, preferred_element_type=F32)
        u = jnp.dot(x, wu, preferred_element_type=F32)
        hid = (a * jax.nn.sigmoid(a)) * u
        y = jnp.dot(hid.astype(BF16), wd, preferred_element_type=F32)

        @pl.when(f == 0)
        def _():
            acc_ref[...] = y

        @pl.when(f != 0)
        def _():
            acc_ref[...] += y

    ffn(xc_ref, accc_ref)
    ffn(xd_ref, accd_ref)

    @pl.when(f == pl.num_programs(1) - 1)
    def _():
        yc_ref[...] = (accc_ref[...] * gc_ref[:, 0:1]).astype(BF16)
        yd_ref[...] = (accd_ref[...] * gd_ref[:, 0:1]).astype(BF16)


def _experts(xs_c, xs_d, gate_c, gate_d, w_gate, w_up, w_down):
    rc, rd = xs_c.shape[1], xs_d.shape[1]
    per_e = lambda rows, width: pl.BlockSpec((None, rows, width), lambda e, f: (e, 0, 0))
    return pl.pallas_call(
        _expert_kernel,
        grid=(N_EXPERTS, EXPERT_FF // FF_TILE),
        in_specs=[per_e(rc, D_MODEL), per_e(rd, D_MODEL), per_e(rc, LANES), per_e(rd, LANES),
                  pl.BlockSpec((None, D_MODEL, FF_TILE), lambda e, f: (e, 0, f)),
                  pl.BlockSpec((None, D_MODEL, FF_TILE), lambda e, f: (e, 0, f)),
                  pl.BlockSpec((None, FF_TILE, D_MODEL), lambda e, f: (e, f, 0))],
        out_specs=[per_e(rc, D_MODEL), per_e(rd, D_MODEL)],
        out_shape=[jax.ShapeDtypeStruct(xs_c.shape, BF16), jax.ShapeDtypeStruct(xs_d.shape, BF16)],
        scratch_shapes=[pltpu.VMEM((rc, D_MODEL), F32), pltpu.VMEM((rd, D_MODEL), F32)],
        compiler_params=_params(2),
        name="expert_swiglu",
    )(xs_c, xs_d, gate_c, gate_d, w_gate, w_up, w_down)


def _combine_kernel(layer, cap, final, x1_ref, mod_ref, pt_ref, ys_ref, gain_ref, x2_ref):
    ys = ys_ref[...].reshape(N_EXPERTS * cap, D_MODEL)
    y = jnp.dot(pt_ref[...], ys, preferred_element_type=F32)
    x2 = x1_ref[...] + _mod_row(mod_ref, layer, 5) * y
    if final:
        x2 = x2 * lax.rsqrt(jnp.mean(x2 * x2, axis=-1, keepdims=True) + RMS_EPS) * gain_ref[...]
    x2_ref[...] = x2


def _combine(grp, layer, final, x1, mods, pt, ys, gain_final):
    n, cap = grp.seq_len, grp.capacity
    return pl.pallas_call(
        functools.partial(_combine_kernel, layer, cap, final),
        grid=(grp.n_seq,),
        in_specs=[pl.BlockSpec((n, D_MODEL), lambda b: (b, 0)),
                  pl.BlockSpec((None, DEPTH * 6, D_MODEL), lambda b: (grp.cond_of_seq(b), 0, 0)),
                  pl.BlockSpec((n, N_EXPERTS * cap), lambda b: (b, 0)),
                  pl.BlockSpec((N_EXPERTS, cap, D_MODEL), lambda b: (0, b, 0)),
                  pl.BlockSpec((1, D_MODEL), lambda b: (0, 0))],
        out_specs=pl.BlockSpec((n, D_MODEL), lambda b: (b, 0)),
        out_shape=jax.ShapeDtypeStruct(x1.shape, F32),
        compiler_params=_params(1),
        name="expert_combine",
    )(x1, mods, pt, ys, gain_final)


S5_BLOCKS = 2
S5_BLOCK_IN = C_WIDTH // S5_BLOCKS
S5_BLOCK_STATE = C_GROUPS * C_STATE // S5_BLOCKS
S5_SUBLANES = 8
S5_SCAN_COLS = 256
S5_SHIFTS = (1, 2, 4)


def _cmul(ar, ai, br, bi):
    return ar * br - ai * bi, ar * bi + ai * br


def _s5_prep_kernel(are_ref, aim_ref, ldt_ref, bre_ref, bim_ref, cre_ref, cim_ref, bmat_ref, cmat_ref, const_ref):
    n = S5_BLOCK_STATE
    row = lax.broadcasted_iota(jnp.int32, (S5_SUBLANES, n), 0)
    for d in range(N_DIR):
        a_re = jnp.minimum(are_ref[d], S5_MAX_RE)
        a_im = aim_ref[d]
        dt = jnp.exp(ldt_ref[d])
        mag = jnp.exp(a_re * dt)
        l_re = mag * jnp.cos(a_im * dt)
        l_im = mag * jnp.sin(a_im * dt)
        den = a_re * a_re + a_im * a_im
        k_re = ((l_re - 1.0) * a_re + l_im * a_im) / den
        k_im = (l_im * a_re - (l_re - 1.0) * a_im) / den
        b_re, b_im = bre_ref[...], bim_ref[...]
        bb_re, bb_im = _cmul(k_re, k_im, b_re, b_im)
        bmat_ref[d] = jnp.concatenate([bb_re, bb_im], axis=1).astype(BF16)
        pows = [(l_re, l_im)]
        for _ in range(S5_SUBLANES - 1):
            pows.append(_cmul(pows[-1][0], pows[-1][1], l_re, l_im))
        for i, s in enumerate(S5_SHIFTS):
            keep = (row >= s) if d == 0 else (row <= S5_SUBLANES - 1 - s)
            const_ref[d, 2 * i] = jnp.where(keep, pows[s - 1][0], 0.0)
            const_ref[d, 2 * i + 1] = jnp.where(keep, pows[s - 1][1], 0.0)
        lp_re = jnp.zeros((S5_SUBLANES, n), F32)
        lp_im = jnp.zeros((S5_SUBLANES, n), F32)
        for r in range(S5_SUBLANES):
            p = pows[r] if d == 0 else pows[S5_SUBLANES - 1 - r]
            lp_re = jnp.where(row == r, p[0], lp_re)
            lp_im = jnp.where(row == r, p[1], lp_im)
        const_ref[d, 6] = lp_re
        const_ref[d, 7] = lp_im
    cmat_ref[...] = jnp.concatenate([cre_ref[...], -cim_ref[...]], axis=0).astype(BF16)


def _s5_prepare(a_re, a_im, log_dt, b_re, b_im, c_re, c_im):
    n, k_in = S5_BLOCK_STATE, S5_BLOCK_IN
    gpb = C_GROUPS // S5_BLOCKS
    per_state = lambda t: t.reshape(N_DIR, S5_BLOCKS, 1, n)
    ldt = jnp.repeat(log_dt, C_STATE, axis=-1)
    eye = jnp.eye(gpb, dtype=F32)

    def expand_b(b):
        b = b.reshape(S5_BLOCKS, gpb, C_STATE, C_GROUP).transpose(0, 1, 3, 2)
        return (b[:, :, :, None, :] * eye[None, :, None, :, None]).reshape(S5_BLOCKS, k_in, n)

    def expand_c(c):
        c = c.reshape(S5_BLOCKS, gpb, C_GROUP, C_STATE).transpose(0, 1, 3, 2)
        return (c[:, :, :, None, :] * eye[None, :, None, :, None]).reshape(S5_BLOCKS, n, k_in)

    row_spec = pl.BlockSpec((N_DIR, None, 1, n), lambda k: (0, k, 0, 0))
    return pl.pallas_call(
        _s5_prep_kernel,
        grid=(S5_BLOCKS,),
        in_specs=[row_spec, row_spec, row_spec,
                  pl.BlockSpec((None, k_in, n), lambda k: (k, 0, 0)),
                  pl.BlockSpec((None, k_in, n), lambda k: (k, 0, 0)),
                  pl.BlockSpec((None, n, k_in), lambda k: (k, 0, 0)),
                  pl.BlockSpec((None, n, k_in), lambda k: (k, 0, 0))],
        out_specs=[pl.BlockSpec((N_DIR, None, k_in, 2 * n), lambda k: (0, k, 0, 0)),
                   pl.BlockSpec((None, 2 * n, k_in), lambda k: (k, 0, 0)),
                   pl.BlockSpec((N_DIR, None, 8, S5_SUBLANES, n), lambda k: (0, k, 0, 0, 0))],
        out_shape=[jax.ShapeDtypeStruct((N_DIR, S5_BLOCKS, k_in, 2 * n), BF16),
                   jax.ShapeDtypeStruct((S5_BLOCKS, 2 * n, k_in), BF16),
                   jax.ShapeDtypeStruct((N_DIR, S5_BLOCKS, 8, S5_SUBLANES, n), F32)],
        compiler_params=_params(1),
        name="s5_prepare",
    )(per_state(a_re), per_state(a_im), per_state(ldt), expand_b(b_re), expand_b(b_im), expand_c(c_re),
      expand_c(c_im))


def _s5_kernel(seq_len, has_state, u_ref, bmat_ref, cmat_ref, const_ref, *rest):
    if has_state:
        s0_ref, y_ref, fin_ref, xre_ref, xim_ref = rest
    else:
        y_ref, fin_ref, xre_ref, xim_ref = rest
    n = S5_BLOCK_STATE
    n_tiles = seq_len // S5_SUBLANES
    u = u_ref[...].astype(BF16)
    for d in range(N_DIR):
        bu = jnp.dot(u, bmat_ref[d], preferred_element_type=F32)
        xre_ref[d] = bu[:, :n]
        xim_ref[d] = bu[:, n:]

    for cb in range(n // S5_SCAN_COLS):
        cols = slice(cb * S5_SCAN_COLS, (cb + 1) * S5_SCAN_COLS)

        def scan_tile(d, i, carry):
            rows = pl.ds(pl.multiple_of(i * S5_SUBLANES, S5_SUBLANES), S5_SUBLANES)
            xr = xre_ref[d, rows, cols]
            xi = xim_ref[d, rows, cols]
            for k, s in enumerate(S5_SHIFTS):
                shift = s if d == 0 else S5_SUBLANES - s
                pr, pi = _cmul(const_ref[d, 2 * k, :, cols], const_ref[d, 2 * k + 1, :, cols],
                               pltpu.roll(xr, shift, 0), pltpu.roll(xi, shift, 0))
                xr, xi = xr + pr, xi + pi
            cr, ci = _cmul(const_ref[d, 6, :, cols], const_ref[d, 7, :, cols], carry[0], carry[1])
            xr, xi = xr + cr, xi + ci
            xre_ref[d, rows, cols] = xr
            xim_ref[d, rows, cols] = xi
            edge = S5_SUBLANES - 1 if d == 0 else 0
            shape = (S5_SUBLANES, S5_SCAN_COLS)
            return (jnp.broadcast_to(xr[edge:edge + 1], shape), jnp.broadcast_to(xi[edge:edge + 1], shape))

        def body(i, carry):
            return (scan_tile(0, i, carry[0]), scan_tile(1, n_tiles - 1 - i, carry[1]))

        shape = (S5_SUBLANES, S5_SCAN_COLS)
        if has_state:
            init = tuple((jnp.broadcast_to(s0_ref[2 * d:2 * d + 1, cols], shape),
                          jnp.broadcast_to(s0_ref[2 * d + 1:2 * d + 2, cols], shape)) for d in range(N_DIR))
        else:
            init = tuple((jnp.zeros(shape, F32), jnp.zeros(shape, F32)) for _ in range(N_DIR))
        fin = lax.fori_loop(0, n_tiles, body, init)
        for d in range(N_DIR):
            fin_ref[2 * d:2 * d + 1, cols] = fin[d][0][0:1]
            fin_ref[2 * d + 1:2 * d + 2, cols] = fin[d][1][0:1]

    x = jnp.concatenate([xre_ref[0] + xre_ref[1], xim_ref[0] + xim_ref[1]], axis=1).astype(BF16)
    y_ref[...] = jnp.dot(x, cmat_ref[...], preferred_element_type=F32)


def _s5_mixer(grp, proj, prep, s0):
    bmat, cmat, consts = prep
    n, k_in, seq_len = S5_BLOCK_STATE, S5_BLOCK_IN, grp.seq_len
    has_state = s0 is not None
    state_spec = pl.BlockSpec((None, None, 2 * N_DIR, n), lambda b, k: (b, k, 0, 0))
    in_specs = [pl.BlockSpec((seq_len, k_in), lambda b, k: (b, k)),
                pl.BlockSpec((N_DIR, None, k_in, 2 * n), lambda b, k: (0, k, 0, 0)),
                pl.BlockSpec((None, 2 * n, k_in), lambda b, k: (k, 0, 0)),
                pl.BlockSpec((N_DIR, None, 8, S5_SUBLANES, n), lambda b, k: (0, k, 0, 0, 0))]
    args = [proj, bmat, cmat, consts]
    if has_state:
        in_specs.append(state_spec)
        args.append(s0)
    return pl.pallas_call(
        functools.partial(_s5_kernel, seq_len, has_state),
        grid=(grp.n_seq, S5_BLOCKS),
        in_specs=in_specs,
        out_specs=[pl.BlockSpec((seq_len, k_in), lambda b, k: (b, k)), state_spec],
        out_shape=[jax.ShapeDtypeStruct((grp.n_seq * seq_len, C_WIDTH), F32),
                   jax.ShapeDtypeStruct((grp.n_seq, S5_BLOCKS, 2 * N_DIR, n), F32)],
        scratch_shapes=[pltpu.VMEM((N_DIR, seq_len, n), F32), pltpu.VMEM((N_DIR, seq_len, n), F32)],
        compiler_params=_params(2),
        name="s5_scan",
    )(*args)


def _s5_state_to_blocks(s):
    b = s.shape[0]
    s = s.reshape(b, N_DIR, S5_BLOCKS, S5_BLOCK_STATE, 2).transpose(0, 2, 1, 4, 3)
    return s.reshape(b, S5_BLOCKS, 2 * N_DIR, S5_BLOCK_STATE)


def _s5_state_from_blocks(s):
    b = s.shape[0]
    s = s.reshape(b, S5_BLOCKS, N_DIR, 2, S5_BLOCK_STATE).transpose(0, 2, 1, 4, 3)
    return s.reshape(b, N_DIR, C_GROUPS, C_STATE, 2)


GLA_BLOCK = D_CHUNK
GLA_KEY_TILE = 128
ODD_IN_PAD = _round_up(sum(ODD_SPLITS), LANES)
_O_U, _O_Q, _O_K, _O_V, _O_G, _O_A = (int(v) for v in np.cumsum((0,) + ODD_SPLITS[:-1]))
GLA_QK = D_HEADS * D_DK
GLA_V = D_HEADS * D_DV


def _split3(x):
    x1 = x.astype(BF16)
    r1 = x - x1.astype(F32)
    x2 = r1.astype(BF16)
    x3 = (r1 - x2.astype(F32)).astype(BF16)
    return x1, x2, x3


def _dot_01(m01, x):
    p1, p2, p3 = _split3(x)
    dot = lambda p: jnp.dot(m01, p, preferred_element_type=F32)
    return dot(p1) + dot(p2) + dot(p3)


def _dot_nt(a, b):
    return lax.dot_general(a, b, (((1,), (1,)), ((), ())), preferred_element_type=F32)


def _gla_kernel(seq_len, has_state, proj_ref, ypre_ref, dskip_ref, wglu_ref, wg2_ref, bgate_ref, gain_ref,
                hsum_ref, *rest):
    if has_state:
        s0_ref, mix_ref, bcum_ref, la_ref, o_ref, s0pad_ref = rest
    else:
        mix_ref, fin_ref, bcum_ref, la_ref, o_ref = rest
    n_blocks = seq_len // GLA_BLOCK
    n_key_tiles = seq_len // GLA_KEY_TILE

    u = proj_ref[:, _O_U:_O_U + C_WIDTH]
    y = jax.nn.gelu(ypre_ref[...] + dskip_ref[...] * u)
    glu = jnp.dot(y.astype(BF16), wglu_ref[...], preferred_element_type=F32)
    mix_ref[:, 0:C_WIDTH] = y * jax.nn.sigmoid(glu)

    logits = jnp.dot(proj_ref[:, _O_A:_O_A + LANES], wg2_ref[...], preferred_element_type=F32,
                     precision=lax.Precision.HIGHEST) + bgate_ref[...]
    la = (jnp.minimum(logits, 0.0) - jnp.log(1.0 + jnp.exp(-jnp.abs(logits)))) / D_TAU
    la_ref[...] = la
    r_i = lax.broadcasted_iota(jnp.int32, (seq_len, seq_len), 0)
    c_i = lax.broadcasted_iota(jnp.int32, (seq_len, seq_len), 1)
    bcum_ref[0] = _dot_01(jnp.where(r_i >= c_i, 1.0, 0.0).astype(BF16), la[:, :GLA_QK])
    bcum_ref[1] = _dot_01(jnp.where(r_i <= c_i, 1.0, 0.0).astype(BF16), la[:, GLA_QK:])

    if has_state:
        zeros = jnp.zeros((D_DK, D_DV), F32)
        for d in range(N_DIR):
            for h in range(D_HEADS):
                s = s0_ref[d, h]
                s0pad_ref[d, h] = jnp.concatenate([s, zeros] if h % 2 == 0 else [zeros, s], axis=0).astype(BF16)

    row_b = lax.broadcasted_iota(jnp.int32, (GLA_BLOCK, GLA_QK), 0)
    lane_pair = lax.broadcasted_iota(jnp.int32, (GLA_BLOCK, LANES), 1)
    key_lane = lax.broadcasted_iota(jnp.int32, (GLA_BLOCK, GLA_KEY_TILE), 1)
    scale = D_DK ** -0.5

    def block(i, _):
        r0 = pl.multiple_of(i * GLA_BLOCK, GLA_BLOCK)
        rows = pl.ds(r0, GLA_BLOCK)
        q = proj_ref[rows, _O_Q:_O_Q + GLA_QK] * scale
        k = proj_ref[rows, _O_K:_O_K + GLA_QK]
        v = proj_ref[rows, _O_V:_O_V + GLA_V]
        o = [jnp.zeros((GLA_BLOCK, D_DV), F32) for _ in range(D_HEADS)]
        for d in range(N_DIR):
            b = bcum_ref[d, rows, :]
            la_blk = la_ref[rows, d * GLA_QK:(d + 1) * GLA_QK]
            edge = 0 if d == 0 else GLA_BLOCK - 1
            ref = b[edge:edge + 1] - la_blk[edge:edge + 1]
            parts = []
            for j in range(GLA_BLOCK):
                w = q * k[j:j + 1] * jnp.exp(jnp.minimum(b - b[j:j + 1], 0.0))
                keep = (row_b >= j) if d == 0 else (row_b <= j)
                parts.append(jnp.where(keep, w, 0.0))
            w_all = jnp.concatenate(parts, axis=0).astype(BF16)
            s_all = jnp.dot(w_all, hsum_ref[...], preferred_element_type=F32)
            for j in range(GLA_BLOCK):
                sj = s_all[j * GLA_BLOCK:(j + 1) * GLA_BLOCK]
                for h in range(D_HEADS):
                    o[h] = o[h] + sj[:, h * D_DV:(h + 1) * D_DV] * v[j:j + 1, h * D_DV:(h + 1) * D_DV]

            qt = q * jnp.exp(b - ref)
            q_heads = []
            for h in range(D_HEADS):
                pair = qt[:, (h // 2) * LANES:(h // 2 + 1) * LANES]
                mine = (lane_pair < D_DK) if h % 2 == 0 else (lane_pair >= D_DK)
                q_heads.append(jnp.where(mine, pair, 0.0).astype(BF16))

            def key_tile(jt, acc, d=d, ref=ref, q_heads=q_heads):
                j0 = pl.multiple_of(jt * GLA_KEY_TILE, GLA_KEY_TILE)
                rows_j = pl.ds(j0, GLA_KEY_TILE)
                kj = proj_ref[rows_j, _O_K:_O_K + GLA_QK]
                kt = (kj * jnp.exp(jnp.minimum(ref - bcum_ref[d, rows_j, :], 0.0))).astype(BF16)
                valid = (key_lane + j0 < r0) if d == 0 else (key_lane + j0 > r0 + GLA_BLOCK - 1)
                out = []
                for h in range(D_HEADS):
                    s = _dot_nt(q_heads[h], kt[:, (h // 2) * LANES:(h // 2 + 1) * LANES])
                    s = jnp.where(valid, s, 0.0).astype(BF16)
                    vj = proj_ref[rows_j, _O_V + h * D_DV:_O_V + (h + 1) * D_DV].astype(BF16)
                    out.append(acc[h] + jnp.dot(s, vj, preferred_element_type=F32))
                return tuple(out)

            if d == 0:
                lo, hi = 0, (r0 + GLA_KEY_TILE - 1) // GLA_KEY_TILE
            else:
                lo, hi = (r0 + GLA_BLOCK) // GLA_KEY_TILE, n_key_tiles
            o = list(lax.fori_loop(lo, hi, key_tile, tuple(o)))

            if has_state:
                qs = (q * jnp.exp(b)).astype(BF16)
                for h in range(D_HEADS):
                    o[h] = o[h] + jnp.dot(qs[:, (h // 2) * LANES:(h // 2 + 1) * LANES], s0pad_ref[d, h],
                                          preferred_element_type=F32)
        o_ref[rows, :] = jnp.concatenate(o, axis=1)
        return 0

    lax.fori_loop(0, n_blocks, block, 0)

    for h in range(D_HEADS):
        cols = slice(h * D_DV, (h + 1) * D_DV)
        oh = o_ref[:, cols]
        oh = oh * lax.rsqrt(jnp.mean(oh * oh, axis=-1, keepdims=True) + RMS_EPS) * gain_ref[...]
        g = proj_ref[:, _O_G + h * D_DV:_O_G + (h + 1) * D_DV]
        mix_ref[:, C_WIDTH + h * D_DV:C_WIDTH + (h + 1) * D_DV] = oh * (g * jax.nn.sigmoid(g))

    if not has_state:
        k_all = proj_ref[:, _O_K:_O_K + GLA_QK]
        for d in range(N_DIR):
            last = seq_len - 1 if d == 0 else 0
            k_out = k_all * jnp.exp(bcum_ref[d, last:last + 1, :] - bcum_ref[d])
            k_out_t = k_out.T.astype(BF16)
            for h in range(D_HEADS):
                vh = proj_ref[:, _O_V + h * D_DV:_O_V + (h + 1) * D_DV].astype(BF16)
                fin_ref[d, h] = jnp.dot(k_out_t[h * D_DK:(h + 1) * D_DK], vh, preferred_element_type=F32)


def _gla_mixer(grp, proj, ypre, d_skip, w_glu, w_gate2, b_gate, gain, s0):
    seq_len = grp.seq_len
    has_state = s0 is not None
    wg2 = jnp.zeros((LANES, N_DIR * GLA_QK), F32)
    for d in range(N_DIR):
        wg2 = wg2.at[d * D_RANK:(d + 1) * D_RANK, d * GLA_QK:(d + 1) * GLA_QK].set(w_gate2[d])
    hsum = jnp.repeat(jnp.repeat(jnp.eye(D_HEADS, dtype=BF16), D_DK, axis=0), D_DV, axis=1)
    full = lambda *shape: pl.BlockSpec(shape, lambda b: (0,) * len(shape))
    state_spec = pl.BlockSpec((None, N_DIR, D_HEADS, D_DK, D_DV), lambda b: (b, 0, 0, 0, 0))
    in_specs = [pl.BlockSpec((seq_len, ODD_IN_PAD), lambda b: (b, 0)),
                pl.BlockSpec((seq_len, C_WIDTH), lambda b: (b, 0)),
                full(1, C_WIDTH), full(C_WIDTH, C_WIDTH), full(LANES, N_DIR * GLA_QK), full(1, N_DIR * GLA_QK),
                full(1, D_DV), full(GLA_QK, GLA_V)]
    args = [proj, ypre, d_skip[None], w_glu.astype(BF16), wg2, b_gate.reshape(1, N_DIR * GLA_QK), gain[None], hsum]
    mix_spec = pl.BlockSpec((seq_len, D_MODEL), lambda b: (b, 0))
    mix_shape = jax.ShapeDtypeStruct((grp.n_seq * seq_len, D_MODEL), F32)
    scratch = [pltpu.VMEM((N_DIR, seq_len, GLA_QK), F32), pltpu.VMEM((seq_len, N_DIR * GLA_QK), F32),
               pltpu.VMEM((seq_len, GLA_V), F32)]
    if has_state:
        in_specs.append(state_spec)
        args.append(s0)
        out_specs, out_shape = mix_spec, mix_shape
        scratch.append(pltpu.VMEM((N_DIR, D_HEADS, 2 * D_DK, D_DV), BF16))
    else:
        out_specs = [mix_spec, state_spec]
        out_shape = [mix_shape, jax.ShapeDtypeStruct((grp.n_seq, N_DIR, D_HEADS, D_DK, D_DV), F32)]
    out = pl.pallas_call(
        functools.partial(_gla_kernel, seq_len, has_state),
        grid=(grp.n_seq,),
        in_specs=in_specs, out_specs=out_specs, out_shape=out_shape, scratch_shapes=scratch,
        compiler_params=_params(1),
        name="gla_mixer",
    )(*args)
    return (out, None) if has_state else (out[0], out[1])


def _split_cols(t, sizes):
    return jnp.split(t, np.cumsum(sizes)[:-1].tolist(), axis=-1)


def _to_heads(t, n_heads):
    b, l, _ = t.shape
    return t.reshape(b, l, n_heads, -1).transpose(0, 2, 1, 3)


def _from_heads(t):
    b, h, l, d = t.shape
    return t.transpose(0, 2, 1, 3).reshape(b, l, h * d)


def _dir_heads(t, n_heads):
    b, l, _ = t.shape
    return t.reshape(b, l, N_DIR, n_heads, -1).transpose(2, 0, 3, 1, 4)


def _flip_seq(t):
    return jnp.flip(t, axis=2)


def _l2_normalize(t):
    return t * lax.rsqrt(jnp.sum(t * t, axis=-1, keepdims=True) + 1e-6)


def _head_rms_norm(o, gain):
    return o * lax.rsqrt(jnp.mean(o * o, axis=-1, keepdims=True) + RMS_EPS) * gain.astype(F32)


def _head_layer_norm(o):
    mu = jnp.mean(o, axis=-1, keepdims=True)
    var = jnp.mean(jnp.square(o - mu), axis=-1, keepdims=True)
    return (o - mu) * lax.rsqrt(var + RMS_EPS)


def _short_conv_silu(x, w):
    pad = w.shape[0] // 2
    y = lax.conv_general_dilated(x, w[:, None, :].astype(x.dtype), window_strides=(1,), padding=[(pad, pad)],
                                 dimension_numbers=('NWC', 'WIO', 'NWC'), feature_group_count=x.shape[-1])
    return jax.nn.silu(y)


def _grid_rope_angles(n_tokens):
    rows = n_tokens // GRID_W
    row = jnp.repeat(jnp.arange(rows, dtype=F32), GRID_W)
    col = jnp.tile(jnp.arange(GRID_W, dtype=F32), rows)
    freq = ROPE_BASE ** (-jnp.arange(ROPE_FREQS, dtype=F32) / ROPE_FREQS)
    return jnp.stack([row[:, None] * freq, col[:, None] * freq], axis=1)


def _apply_rope_2d(x, ang):
    xr = x.reshape(*x.shape[:-1], 2, 2, ROPE_FREQS)
    x1, x2 = xr[..., 0, :], xr[..., 1, :]
    cos, sin = jnp.cos(ang), jnp.sin(ang)
    return jnp.stack([x1 * cos - x2 * sin, x1 * sin + x2 * cos], axis=-2).reshape(x.shape)


def _gated_delta_chunked(q, k, v, beta, g, s0):
    bsz, nh, n_tok, dk = q.shape
    dv = v.shape[-1]
    cs = A_CHUNK
    nc = n_tok // cs
    q = (q * dk ** -0.5).reshape(bsz, nh, nc, cs, dk)
    k = k.reshape(bsz, nh, nc, cs, dk)
    v = v.reshape(bsz, nh, nc, cs, dv)
    beta = beta.reshape(bsz, nh, nc, cs, 1)
    g = jnp.cumsum(g.reshape(bsz, nh, nc, cs), axis=-1)
    pos = jnp.arange(cs)
    tri = pos[:, None] >= pos[None, :]
    strict = pos[:, None] > pos[None, :]
    decay = jnp.exp(jnp.where(tri, g[..., :, None] - g[..., None, :], -jnp.inf))
    k_beta = k * beta
    lmat = jnp.where(strict, jnp.einsum('bhnid,bhnjd->bhnij', k_beta, k) * decay, 0.0)
    eye = jnp.eye(cs, dtype=q.dtype)
    t_inv = lax.linalg.triangular_solve(eye + lmat, jnp.broadcast_to(eye, lmat.shape), left_side=True,
                                        lower=True, unit_diagonal=True)
    u = jnp.einsum('bhnij,bhnje->bhnie', t_inv, v * beta)
    w = jnp.einsum('bhnij,bhnjd->bhnid', t_inv, k_beta * jnp.exp(g)[..., None])
    attn = jnp.einsum('bhnid,bhnjd->bhnij', q, k) * decay
    q_in = q * jnp.exp(g)[..., None]
    g_last = g[..., -1]
    k_out = k * jnp.exp(g_last[..., None] - g)[..., None]

    def step(s, xs):
        q_c, k_c, u_c, w_c, a_c, gl = xs
        v_new = u_c - jnp.einsum('bhid,bhde->bhie', w_c, s)
        o = jnp.einsum('bhid,bhde->bhie', q_c, s) + jnp.einsum('bhij,bhje->bhie', a_c, v_new)
        s = s * jnp.exp(gl)[..., None, None] + jnp.einsum('bhjd,bhje->bhde', k_c, v_new)
        return s, o

    xs = tuple(jnp.moveaxis(t, 2, 0) for t in (q_in, k_out, u, w, attn, g_last))
    s_fin, o = lax.scan(step, s0, xs)
    return jnp.moveaxis(o, 0, 2).reshape(bsz, nh, n_tok, dv), s_fin


def _retention_chunked(q, k, v, log_gamma, s0):
    bsz, nh, n_tok, dk = q.shape
    dv = v.shape[-1]
    cs = B_CHUNK
    nc = n_tok // cs
    q = (q * dk ** -0.5).reshape(bsz, nh, nc, cs, dk)
    k = k.reshape(bsz, nh, nc, cs, dk)
    v = v.reshape(bsz, nh, nc, cs, dv)
    pos = jnp.arange(cs, dtype=F32)
    tri = pos[:, None] >= pos[None, :]
    lg = log_gamma[:, None, None]
    dmat = jnp.exp(jnp.where(tri, (pos[:, None] - pos[None, :]) * lg, -jnp.inf))
    xi = jnp.exp((pos + 1.0) * log_gamma[:, None])
    zeta = jnp.exp((cs - 1.0 - pos) * log_gamma[:, None])
    g_chunk = jnp.exp(cs * log_gamma)
    scores = jnp.einsum('bhnid,bhnjd->bhnij', q, k) * dmat[:, None]
    intra = jnp.einsum('bhnij,bhnje->bhnie', scores, v)
    k_sc = k * zeta[:, None, :, None]

    def step(s, xs):
        q_c, k_c, v_c = xs
        o = jnp.einsum('bhid,bhde->bhie', q_c, s) * xi[:, :, None]
        s = s * g_chunk[:, None, None] + jnp.einsum('bhjd,bhje->bhde', k_c, v_c)
        return s, o

    xs = tuple(jnp.moveaxis(t, 2, 0) for t in (q, k_sc, v))
    s_fin, cross = lax.scan(step, s0, xs)
    o = intra + jnp.moveaxis(cross, 0, 2)
    return o.reshape(bsz, nh, n_tok, dv), s_fin


def _gla_chunked(q, k, v, log_a, s0):
    bsz, nh, n_tok, dk = q.shape
    dv = v.shape[-1]
    cs = D_CHUNK
    nc = n_tok // cs
    q = (q * dk ** -0.5).reshape(bsz, nh, nc, cs, dk)
    k = k.reshape(bsz, nh, nc, cs, dk)
    v = v.reshape(bsz, nh, nc, cs, dv)
    b = jnp.cumsum(log_a.reshape(bsz, nh, nc, cs, dk), axis=3)
    pos = jnp.arange(cs)
    tri = pos[:, None] >= pos[None, :]
    decay = jnp.exp(jnp.where(tri[:, :, None], b[..., :, None, :] - b[..., None, :, :], -jnp.inf))
    scores = jnp.einsum('bhnid,bhnjd,bhnijd->bhnij', q, k, decay)
    intra = jnp.einsum('bhnij,bhnje->bhnie', scores, v)
    b_last = b[:, :, :, -1]
    q_in = q * jnp.exp(b)
    k_out = k * jnp.exp(b_last[:, :, :, None] - b)

    def step(s, xs):
        q_c, k_c, v_c, bl = xs
        o = jnp.einsum('bhid,bhde->bhie', q_c, s)
        s = s * jnp.exp(bl)[..., None] + jnp.einsum('bhjd,bhje->bhde', k_c, v_c)
        return s, o

    xs = tuple(jnp.moveaxis(t, 2, 0) for t in (q_in, k_out, v, b_last))
    s_fin, cross = lax.scan(step, s0, xs)
    o = intra + jnp.moveaxis(cross, 0, 2)
    return o.reshape(bsz, nh, n_tok, dv), s_fin


def _linear_recurrence_combine(e1, e2):
    a1, b1 = e1
    a2, b2 = e2
    return a1 * a2, a2 * b1 + b2


def _s5_scan(u, a_re, a_im, log_dt, b_re, b_im, s0):
    lam = lax.complex(jnp.minimum(a_re.astype(F32), S5_MAX_RE), a_im.astype(F32))
    dt = jnp.exp(log_dt.astype(F32))[:, None]
    lam_bar = jnp.exp(lam * dt)
    b_bar = ((lam_bar - 1.0) / lam)[..., None] * lax.complex(b_re.astype(F32), b_im.astype(F32))
    bu = jnp.einsum('gpi,blgi->blgp', b_bar, u.astype(jnp.complex64))
    bu = bu.at[:, 0].add(lam_bar * s0)
    decay = jnp.broadcast_to(lam_bar, bu.shape)
    _, states = lax.associative_scan(_linear_recurrence_combine, (decay, bu), axis=1)
    return states, states[:, -1]


def _even_mixer(proj, conv_w, a_log, dt_bias, gdn_gain, ret_log_decay, ret_gain, s0_gdn, s0_ret, rope_ang):
    qa, ka, va, za, ba, aa, qb, kb, vb, gb = _split_cols(proj, EVEN_SPLITS)
    qkv = _short_conv_silu(jnp.concatenate([qa, ka, va], axis=-1), conv_w)
    qa, ka, va = _split_cols(qkv, (A_HEADS * A_DK, A_HEADS * A_DK, A_HEADS * A_DV))
    qa = _l2_normalize(_to_heads(qa, A_HEADS))
    ka = _l2_normalize(_to_heads(ka, A_HEADS))
    va = _to_heads(va, A_HEADS)
    beta = jax.nn.sigmoid(_dir_heads(ba, A_HEADS)[..., 0])
    g = -jnp.exp(a_log)[:, None, :, None] * jax.nn.softplus(
        _dir_heads(aa, A_HEADS)[..., 0] + dt_bias[:, None, :, None])
    o_f, s_f = _gated_delta_chunked(qa, ka, va, beta[0], g[0], s0_gdn[:, 0])
    o_b, s_b = _gated_delta_chunked(_flip_seq(qa), _flip_seq(ka), _flip_seq(va), _flip_seq(beta[1]),
                                    _flip_seq(g[1]), s0_gdn[:, 1])
    out_a = _from_heads(_head_rms_norm(o_f + _flip_seq(o_b), gdn_gain)) * jax.nn.silu(za)
    qb = _to_heads(qb, B_HEADS)
    kb = _to_heads(kb, B_HEADS)
    vb = _to_heads(vb, B_HEADS)
    if rope_ang is not None:
        qb = _apply_rope_2d(qb, rope_ang)
        kb = _apply_rope_2d(kb, rope_ang)
    log_gamma = -jnp.exp(ret_log_decay)
    r_f, t_f = _retention_chunked(qb, kb, vb, log_gamma[0], s0_ret[:, 0])
    r_b, t_b = _retention_chunked(_flip_seq(qb), _flip_seq(kb), _flip_seq(vb), log_gamma[1], s0_ret[:, 1])
    out_b = (_from_heads(_head_layer_norm(r_f + _flip_seq(r_b))) * ret_gain) * jax.nn.silu(gb)
    return jnp.concatenate([out_a, out_b], axis=-1), jnp.stack([s_f, s_b], axis=1), jnp.stack([t_f, t_b], axis=1)


def _odd_mixer(proj, a_re, a_im, log_dt, b_re, b_im, c_re, c_im, d_skip, w_glu, w_gate2, b_gate, gla_gain,
               s0_s5, s0_gla):
    bsz, n_tok, _ = proj.shape
    u, qd, kd, vd, gd, alr = _split_cols(proj, ODD_SPLITS)
    uf = u.reshape(bsz, n_tok, C_GROUPS, C_GROUP)
    s0 = lax.complex(s0_s5[..., 0], s0_s5[..., 1])
    st_f, fin_f = _s5_scan(uf, a_re[0], a_im[0], log_dt[0], b_re, b_im, s0[:, 0])
    st_b, fin_b = _s5_scan(jnp.flip(uf, axis=1), a_re[1], a_im[1], log_dt[1], b_re, b_im, s0[:, 1])
    states = st_f + jnp.flip(st_b, axis=1)
    cmat = lax.complex(c_re, c_im)
    y = jnp.einsum('gip,blgp->blgi', cmat, states).real + d_skip.reshape(C_GROUPS, C_GROUP) * uf
    y = jax.nn.gelu(y.reshape(bsz, n_tok, C_WIDTH))
    out_c = y * jax.nn.sigmoid(y @ w_glu)
    fin = jnp.stack([fin_f, fin_b], axis=1)
    s5_state = jnp.stack([fin.real, fin.imag], axis=-1)
    q = _to_heads(qd, D_HEADS)
    k = _to_heads(kd, D_HEADS)
    v = _to_heads(vd, D_HEADS)
    alr = alr.reshape(bsz, n_tok, N_DIR, D_RANK)
    logits = jnp.einsum('blnr,nrk->nblk', alr, w_gate2) + b_gate[:, None, None, :]
    log_a = (jax.nn.log_sigmoid(logits) / D_TAU).reshape(N_DIR, bsz, n_tok, D_HEADS, D_DK).transpose(0, 1, 3, 2, 4)
    o_f, s_f = _gla_chunked(q, k, v, log_a[0], s0_gla[:, 0])
    o_b, s_b = _gla_chunked(_flip_seq(q), _flip_seq(k), _flip_seq(v), _flip_seq(log_a[1]), s0_gla[:, 1])
    out_d = _from_heads(_head_rms_norm(o_f + _flip_seq(o_b), gla_gain)) * jax.nn.silu(gd)
    return jnp.concatenate([out_c, out_d], axis=-1), s5_state, jnp.stack([s_f, s_b], axis=1)


def _pad_cols(w, n):
    return jnp.pad(w, ((0, 0), (0, n - w.shape[1])))


def kernel(x_prompt, x_sample, state_gdn, state_ret, state_s5, state_gla, c, c_ctx, ada_w, ada_b, norm_mix,
           norm_ffn, norm_final, ev_w_in, ev_conv, gdn_a_log, gdn_dt_bias, gdn_gain, ret_log_decay, ret_gain,
           ev_w_out, od_w_in, s5_a_re, s5_a_im, s5_log_dt, s5_b_re, s5_b_im, s5_c_re, s5_c_im, s5_d, s5_w_glu,
           gla_w_gate2, gla_b_gate, gla_gain, od_w_out, moe_router, moe_w_gate, moe_w_up, moe_w_down):
    cond = jnp.zeros((COND_ROWS, D_MODEL), F32).at[0].set(c_ctx).at[1:N_COND].set(c)
    mods = _modulation(cond, ada_w, ada_b)
    rope_ang = _grid_rope_angles(DEC_SEQ)
    even_in, odd_in = sum(EVEN_SPLITS), sum(ODD_SPLITS)

    xs = {CTX: x_prompt.reshape(BATCH * SEQ, D_MODEL), DEC: x_sample.reshape(DEC_BATCH * DEC_SEQ, D_MODEL)}
    zero = lambda *shape: jnp.zeros(shape, F32)
    st = {CTX: (zero(BATCH, N_DIR, A_HEADS, A_DK, A_DV), zero(BATCH, N_DIR, B_HEADS, B_DK, B_DV),
                zero(BATCH, N_DIR, C_GROUPS, C_STATE, 2), zero(BATCH, N_DIR, D_HEADS, D_DK, D_DV))}
    new_states = {"gdn": [], "ret": [], "s5": [], "gla": []}

    for layer in range(DEPTH):
        j = layer // 2
        even = layer % 2 == 0
        w_in = ev_w_in[j] if even else od_w_in[j]
        n_in = even_in if even else odd_in
        w_in = _pad_cols(w_in, _round_up(n_in, LANES)).astype(BF16)
        w_out = (ev_w_out[j] if even else od_w_out[j]).astype(BF16)
        w_router = _pad_cols(moe_router[layer], LANES)
        routed = {}
        if not even:
            s5_prep = _s5_prepare(s5_a_re[j], s5_a_im[j], s5_log_dt[j], s5_b_re[j], s5_b_im[j], s5_c_re[j],
                                  s5_c_im[j])
        for grp in (CTX, DEC):
            x = xs[grp]
            proj = _in_projection(grp, layer, x, mods, norm_mix[layer][None], w_in)
            if not even:
                s0_c = None if grp.is_context else _s5_state_to_blocks(state_s5[:, j])
                s0_d = None if grp.is_context else state_gla[:, j]
                ypre, fin_c = _s5_mixer(grp, proj, s5_prep, s0_c)
                mix, fin_d = _gla_mixer(grp, proj, ypre, s5_d[j], s5_w_glu[j], gla_w_gate2[j], gla_b_gate[j],
                                        gla_gain[j], s0_d)
                if grp.is_context:
                    new_states["s5"].append(_s5_state_from_blocks(fin_c))
                    new_states["gla"].append(fin_d)
            else:
                proj = proj[:, :n_in].reshape(grp.n_seq, grp.seq_len, n_in)
                s0_a, s0_b = (st[CTX][0], st[CTX][1]) if grp.is_context else (state_gdn[:, j], state_ret[:, j])
                mix, s_a, s_b = _even_mixer(proj, ev_conv[j], gdn_a_log[j], gdn_dt_bias[j], gdn_gain[j],
                                            ret_log_decay[j], ret_gain[j], s0_a, s0_b,
                                            None if grp.is_context else rope_ang)
                if grp.is_context:
                    new_states["gdn"].append(s_a)
                    new_states["ret"].append(s_b)
            mix = mix.reshape(grp.n_seq * grp.seq_len, D_MODEL)
            x1, h2, aff = _out_projection(grp, layer, mix, x, mods, w_out, norm_ffn[layer][None], w_router)
            xs_g, pt, gate = _route(grp, aff, h2)
            routed[grp] = (x1, xs_g, pt, gate)
        ys_c, ys_d = _experts(routed[CTX][1], routed[DEC][1], routed[CTX][3], routed[DEC][3],
                              moe_w_gate[layer], moe_w_up[layer], moe_w_down[layer])
        for grp, ys in ((CTX, ys_c), (DEC, ys_d)):
            x1, _, pt, _ = routed[grp]
            xs[grp] = _combine(grp, layer, layer == DEPTH - 1, x1, mods, pt, ys, norm_final[None])

    y_prompt = xs[CTX].reshape(BATCH, SEQ, D_MODEL)
    y_sample = xs[DEC].reshape(DEC_BATCH, DEC_SEQ, D_MODEL)
    return (y_prompt, y_sample, jnp.stack(new_states["gdn"], axis=1), jnp.stack(new_states["ret"], axis=1),
            jnp.stack(new_states["s5"], axis=1), jnp.stack(new_states["gla"], axis=1))
```

```python
import functools
import math
from typing import NamedTuple

import numpy as np
import jax
import jax.numpy as jnp
from jax import lax
from jax.experimental import pallas as pl
from jax.experimental.pallas import tpu as pltpu

F32 = jnp.float32
BF16 = jnp.bfloat16

D_MODEL = 1024
BATCH = 32
SEQ = 256
DEPTH = 4
DEC_BATCH = 2
DEC_SEQ = 1024
GRID_W = 64
N_DIR = 2
A_HEADS, A_DK, A_DV, A_CONV, A_CHUNK = 4, 128, 128, 5, 64
B_HEADS, B_DK, B_DV, B_CHUNK = 8, 64, 64, 64
ROPE_FREQS = B_DK // 4
ROPE_BASE = 10000.0
C_GROUP, C_GROUPS, C_STATE = 16, 32, 64
C_WIDTH = C_GROUPS * C_GROUP
S5_MAX_RE = -1e-4
D_HEADS, D_DK, D_DV, D_RANK, D_TAU, D_CHUNK = 4, 64, 128, 16, 16.0, 16
N_EXPERTS = 16
EXPERT_FF = 1024
EC_CAPACITY_FACTOR = 2
RMS_EPS = 1e-6

EVEN_SPLITS = (A_HEADS * A_DK, A_HEADS * A_DK, A_HEADS * A_DV, A_HEADS * A_DV, N_DIR * A_HEADS, N_DIR * A_HEADS,
               B_HEADS * B_DK, B_HEADS * B_DK, B_HEADS * B_DV, B_HEADS * B_DV)
ODD_SPLITS = (C_WIDTH, D_HEADS * D_DK, D_HEADS * D_DK, D_HEADS * D_DV, D_HEADS * D_DV, N_DIR * D_RANK)

LANES = 128
TOKEN_TILE = 256
N_COND = 1 + DEC_BATCH
COND_ROWS = 8
MOD_COL_TILE = 1536
VMEM_LIMIT = 56 * 1024 * 1024


class Group(NamedTuple):
    n_seq: int
    seq_len: int
    is_context: bool

    @property
    def tiles_per_seq(self):
        return self.seq_len // TOKEN_TILE

    @property
    def n_tiles(self):
        return self.n_seq * self.tiles_per_seq

    @property
    def capacity(self):
        return EC_CAPACITY_FACTOR * self.seq_len // N_EXPERTS

    def cond_of_tile(self, i):
        return 0 if self.is_context else 1 + i // self.tiles_per_seq

    def cond_of_seq(self, b):
        return 0 if self.is_context else 1 + b


CTX = Group(BATCH, SEQ, True)
DEC = Group(DEC_BATCH, DEC_SEQ, False)


def _round_up(n, m):
    return (n + m - 1) // m * m


def _params(n_axes):
    return pltpu.CompilerParams(dimension_semantics=("arbitrary",) * n_axes, vmem_limit_bytes=VMEM_LIMIT)


def _mod_kernel(c_ref, w_ref, b_ref, o_ref):
    c = c_ref[...]
    s = c * jax.nn.sigmoid(c)
    o_ref[0] = jnp.dot(s.astype(BF16), w_ref[0].astype(BF16), preferred_element_type=F32) + b_ref[0]


def _modulation(cond, ada_w, ada_b):
    n_out = 6 * D_MODEL
    out = pl.pallas_call(
        _mod_kernel,
        grid=(DEPTH, n_out // MOD_COL_TILE),
        in_specs=[pl.BlockSpec((COND_ROWS, D_MODEL), lambda l, j: (0, 0)),
                  pl.BlockSpec((1, D_MODEL, MOD_COL_TILE), lambda l, j: (l, 0, j)),
                  pl.BlockSpec((1, 1, MOD_COL_TILE), lambda l, j: (l, 0, j))],
        out_specs=pl.BlockSpec((1, COND_ROWS, MOD_COL_TILE), lambda l, j: (l, 0, j)),
        out_shape=jax.ShapeDtypeStruct((DEPTH, COND_ROWS, n_out), F32),
        compiler_params=_params(2),
        name="adaln_modulation",
    )(cond, ada_w, ada_b.reshape(DEPTH, 1, n_out))
    out = out[:, :N_COND].reshape(DEPTH, N_COND, 6, D_MODEL)
    return out.transpose(1, 0, 2, 3).reshape(N_COND, DEPTH * 6, D_MODEL)


def _norm_mod(x, gain, shift, scale):
    y = x * lax.rsqrt(jnp.mean(x * x, axis=-1, keepdims=True) + RMS_EPS)
    return (y * gain) * (1.0 + scale) + shift


def _mod_row(mod_ref, layer, k):
    r = 6 * layer + k
    return mod_ref[r:r + 1, :]


def _inproj_kernel(layer, x_ref, mod_ref, gain_ref, w_ref, o_ref):
    h = _norm_mod(x_ref[...], gain_ref[...], _mod_row(mod_ref, layer, 0), _mod_row(mod_ref, layer, 1))
    o_ref[...] = jnp.dot(h.astype(BF16), w_ref[...], preferred_element_type=F32)


def _in_projection(grp, layer, x, mods, gain, w):
    n_tok = grp.n_seq * grp.seq_len
    n_out = w.shape[1]
    return pl.pallas_call(
        functools.partial(_inproj_kernel, layer),
        grid=(grp.n_tiles,),
        in_specs=[pl.BlockSpec((TOKEN_TILE, D_MODEL), lambda i: (i, 0)),
                  pl.BlockSpec((None, DEPTH * 6, D_MODEL), lambda i: (grp.cond_of_tile(i), 0, 0)),
                  pl.BlockSpec((1, D_MODEL), lambda i: (0, 0)),
                  pl.BlockSpec((D_MODEL, n_out), lambda i: (0, 0))],
        out_specs=pl.BlockSpec((TOKEN_TILE, n_out), lambda i: (i, 0)),
        out_shape=jax.ShapeDtypeStruct((n_tok, n_out), F32),
        compiler_params=_params(1),
        name="in_projection",
    )(x, mods, gain, w)


def _outproj_kernel(layer, n_parts, *refs):
    mix_refs, wout_refs = refs[:n_parts], refs[n_parts:2 * n_parts]
    x_ref, mod_ref, gain_ref, wr_ref, x1_ref, h2_ref, aff_ref = refs[2 * n_parts:]
    y = jnp.dot(mix_refs[0][...].astype(BF16), wout_refs[0][...], preferred_element_type=F32)
    for m_ref, w_ref in zip(mix_refs[1:], wout_refs[1:]):
        y = y + jnp.dot(m_ref[...].astype(BF16), w_ref[...], preferred_element_type=F32)
    x1 = x_ref[...] + _mod_row(mod_ref, layer, 2) * y
    x1_ref[...] = x1
    h2 = _norm_mod(x1, gain_ref[...], _mod_row(mod_ref, layer, 3), _mod_row(mod_ref, layer, 4))
    h2_ref[...] = h2.astype(BF16)
    logits = jnp.dot(h2, wr_ref[...], preferred_element_type=F32, precision=lax.Precision.HIGHEST)
    lane = lax.broadcasted_iota(jnp.int32, logits.shape, 1)
    logits = jnp.where(lane < N_EXPERTS, logits, -jnp.inf)
    e = jnp.exp(logits - jnp.max(logits, axis=-1, keepdims=True))
    aff_ref[...] = e / jnp.sum(e, axis=-1, keepdims=True)


def _out_projection(grp, layer, mixes, x, mods, w_outs, gain, w_router):
    n_tok = grp.n_seq * grp.seq_len
    tile = lambda width: pl.BlockSpec((TOKEN_TILE, width), lambda i: (i, 0))
    return pl.pallas_call(
        functools.partial(_outproj_kernel, layer, len(mixes)),
        grid=(grp.n_tiles,),
        in_specs=[tile(m.shape[1]) for m in mixes]
                 + [pl.BlockSpec(w.shape, lambda i: (0, 0)) for w in w_outs]
                 + [tile(D_MODEL),
                    pl.BlockSpec((None, DEPTH * 6, D_MODEL), lambda i: (grp.cond_of_tile(i), 0, 0)),
                    pl.BlockSpec((1, D_MODEL), lambda i: (0, 0)),
                    pl.BlockSpec((D_MODEL, LANES), lambda i: (0, 0))],
        out_specs=[tile(D_MODEL), tile(D_MODEL), tile(LANES)],
        out_shape=[jax.ShapeDtypeStruct((n_tok, D_MODEL), F32),
                   jax.ShapeDtypeStruct((n_tok, D_MODEL), BF16),
                   jax.ShapeDtypeStruct((n_tok, LANES), F32)],
        compiler_params=_params(1),
        name="out_projection_router",
    )(*mixes, *w_outs, x, mods, gain, w_router)


def _route_kernel(n, cap, aff_ref, h2_ref, xs_ref, pt_ref, gate_ref):
    aff = aff_ref[...]
    aff_t = aff.T
    t_sub = lax.broadcasted_iota(jnp.int32, (n, n), 0)
    t_lane = lax.broadcasted_iota(jnp.int32, (n, n), 1)
    earlier = t_sub < t_lane
    sel_rows = []
    for e in range(N_EXPERTS):
        col = aff[:, e:e + 1]
        row = aff_t[e:e + 1, :]
        beats = (col > row) | ((col == row) & earlier)
        rank = jnp.sum(jnp.where(beats, 1.0, 0.0), axis=0, keepdims=True)
        sel_rows.append(jnp.where(rank < cap, 1.0, 0.0))
    sel = jnp.concatenate(sel_rows, axis=0)
    pos = jnp.dot(sel.astype(BF16), jnp.where(earlier, 1.0, 0.0).astype(BF16), preferred_element_type=F32)
    e_idx = lax.broadcasted_iota(jnp.int32, (N_EXPERTS, n), 0)
    slot = jnp.where(sel > 0.0, pos.astype(jnp.int32) + e_idx * cap, -1)
    slot_pad = jnp.concatenate([slot, jnp.full((LANES - N_EXPERTS, n), -1, jnp.int32)], axis=0)
    slot_t = slot_pad.astype(F32).T.astype(jnp.int32)
    s_lane = lax.broadcasted_iota(jnp.int32, (n, N_EXPERTS * cap), 1)
    pt = jnp.zeros((n, N_EXPERTS * cap), F32)
    c_sub = lax.broadcasted_iota(jnp.int32, (cap, n), 0)
    h2 = h2_ref[...]
    for e in range(N_EXPERTS):
        pt = pt + jnp.where(slot_t[:, e:e + 1] == s_lane, 1.0, 0.0)
        p_e = jnp.where(slot[e:e + 1, :] == c_sub + e * cap, 1.0, 0.0)
        xs_ref[e] = jnp.dot(p_e.astype(BF16), h2, preferred_element_type=F32).astype(BF16)
        gate = jnp.sum(p_e * aff_t[e:e + 1, :], axis=1, keepdims=True)
        gate_ref[e] = jnp.broadcast_to(gate, (cap, LANES))
    pt_ref[...] = pt.astype(BF16)


def _route(grp, aff, h2):
    n, cap = grp.seq_len, grp.capacity
    slots = N_EXPERTS * cap
    return pl.pallas_call(
        functools.partial(_route_kernel, n, cap),
        grid=(grp.n_seq,),
        in_specs=[pl.BlockSpec((n, LANES), lambda b: (b, 0)),
                  pl.BlockSpec((n, D_MODEL), lambda b: (b, 0))],
        out_specs=[pl.BlockSpec((N_EXPERTS, cap, D_MODEL), lambda b: (0, b, 0)),
                   pl.BlockSpec((n, slots), lambda b: (b, 0)),
                   pl.BlockSpec((N_EXPERTS, cap, LANES), lambda b: (0, b, 0))],
        out_shape=[jax.ShapeDtypeStruct((N_EXPERTS, grp.n_seq * cap, D_MODEL), BF16),
                   jax.ShapeDtypeStruct((grp.n_seq * n, slots), BF16),
                   jax.ShapeDtypeStruct((N_EXPERTS, grp.n_seq * cap, LANES), F32)],
        compiler_params=_params(1),
        name="expert_choice_route",
    )(aff, h2)


FF_TILE = 512


def _expert_kernel(xc_ref, xd_ref, gc_ref, gd_ref, wg_ref, wu_ref, wd_ref, yc_ref, yd_ref, accc_ref, accd_ref):
    f = pl.program_id(1)
    wg = wg_ref[...].astype(BF16)
    wu = wu_ref[...].astype(BF16)
    wd = wd_ref[...].astype(BF16)

    def ffn(x_ref, acc_ref):
        x = x_ref[...]
        a = jnp.dot(x, wg, preferred_element_type=F32)
        u = jnp.dot(x, wu, preferred_element_type=F32)
        hid = (a * jax.nn.sigmoid(a)) * u
        y = jnp.dot(hid.astype(BF16), wd, preferred_element_type=F32)

        @pl.when(f == 0)
        def _():
            acc_ref[...] = y

        @pl.when(f != 0)
        def _():
            acc_ref[...] += y

    ffn(xc_ref, accc_ref)
    ffn(xd_ref, accd_ref)

    @pl.when(f == pl.num_programs(1) - 1)
    def _():
        yc_ref[...] = (accc_ref[...] * gc_ref[:, 0:1]).astype(BF16)
        yd_ref[...] = (accd_ref[...] * gd_ref[:, 0:1]).astype(BF16)


def _experts(xs_c, xs_d, gate_c, gate_d, w_gate, w_up, w_down):
    rc, rd = xs_c.shape[1], xs_d.shape[1]
    per_e = lambda rows, width: pl.BlockSpec((None, rows, width), lambda e, f: (e, 0, 0))
    return pl.pallas_call(
        _expert_kernel,
        grid=(N_EXPERTS, EXPERT_FF // FF_TILE),
        in_specs=[per_e(rc, D_MODEL), per_e(rd, D_MODEL), per_e(rc, LANES), per_e(rd, LANES),
                  pl.BlockSpec((None, D_MODEL, FF_TILE), lambda e, f: (e, 0, f)),
                  pl.BlockSpec((None, D_MODEL, FF_TILE), lambda e, f: (e, 0, f)),
                  pl.BlockSpec((None, FF_TILE, D_MODEL), lambda e, f: (e, f, 0))],
        out_specs=[per_e(rc, D_MODEL), per_e(rd, D_MODEL)],
        out_shape=[jax.ShapeDtypeStruct(xs_c.shape, BF16), jax.ShapeDtypeStruct(xs_d.shape, BF16)],
        scratch_shapes=[pltpu.VMEM((rc, D_MODEL), F32), pltpu.VMEM((rd, D_MODEL), F32)],
        compiler_params=_params(2),
        name="expert_swiglu",
    )(xs_c, xs_d, gate_c, gate_d, w_gate, w_up, w_down)


def _combine_kernel(layer, cap, final, x1_ref, mod_ref, pt_ref, ys_ref, gain_ref, x2_ref):
    ys = ys_ref[...].reshape(N_EXPERTS * cap, D_MODEL)
    y = jnp.dot(pt_ref[...], ys, preferred_element_type=F32)
    x2 = x1_ref[...] + _mod_row(mod_ref, layer, 5) * y
    if final:
        x2 = x2 * lax.rsqrt(jnp.mean(x2 * x2, axis=-1, keepdims=True) + RMS_EPS) * gain_ref[...]
    x2_ref[...] = x2


def _combine(grp, layer, final, x1, mods, pt, ys, gain_final):
    n, cap = grp.seq_len, grp.capacity
    return pl.pallas_call(
        functools.partial(_combine_kernel, layer, cap, final),
        grid=(grp.n_seq,),
        in_specs=[pl.BlockSpec((n, D_MODEL), lambda b: (b, 0)),
                  pl.BlockSpec((None, DEPTH * 6, D_MODEL), lambda b: (grp.cond_of_seq(b), 0, 0)),
                  pl.BlockSpec((n, N_EXPERTS * cap), lambda b: (b, 0)),
                  pl.BlockSpec((N_EXPERTS, cap, D_MODEL), lambda b: (0, b, 0)),
                  pl.BlockSpec((1, D_MODEL), lambda b: (0, 0))],
        out_specs=pl.BlockSpec((n, D_MODEL), lambda b: (b, 0)),
        out_shape=jax.ShapeDtypeStruct(x1.shape, F32),
        compiler_params=_params(1),
        name="expert_combine",
    )(x1, mods, pt, ys, gain_final)


S5_BLOCKS = 2
S5_BLOCK_IN = C_WIDTH // S5_BLOCKS
S5_BLOCK_STATE = C_GROUPS * C_STATE // S5_BLOCKS
S5_SUBLANES = 8
S5_SCAN_COLS = 256
S5_SHIFTS = (1, 2, 4)


def _cmul(ar, ai, br, bi):
    return ar * br - ai * bi, ar * bi + ai * br


def _s5_prep_kernel(are_ref, aim_ref, ldt_ref, bre_ref, bim_ref, cre_ref, cim_ref, bmat_ref, cmat_ref, const_ref):
    n = S5_BLOCK_STATE
    row = lax.broadcasted_iota(jnp.int32, (S5_SUBLANES, n), 0)
    for d in range(N_DIR):
        a_re = jnp.minimum(are_ref[d], S5_MAX_RE)
        a_im = aim_ref[d]
        dt = jnp.exp(ldt_ref[d])
        mag = jnp.exp(a_re * dt)
        l_re = mag * jnp.cos(a_im * dt)
        l_im = mag * jnp.sin(a_im * dt)
        den = a_re * a_re + a_im * a_im
        k_re = ((l_re - 1.0) * a_re + l_im * a_im) / den
        k_im = (l_im * a_re - (l_re - 1.0) * a_im) / den
        b_re, b_im = bre_ref[...], bim_ref[...]
        bb_re, bb_im = _cmul(k_re, k_im, b_re, b_im)
        bmat_ref[d] = jnp.concatenate([bb_re, bb_im], axis=1).astype(BF16)
        pows = [(l_re, l_im)]
        for _ in range(S5_SUBLANES - 1):
            pows.append(_cmul(pows[-1][0], pows[-1][1], l_re, l_im))
        for i, s in enumerate(S5_SHIFTS):
            keep = (row >= s) if d == 0 else (row <= S5_SUBLANES - 1 - s)
            const_ref[d, 2 * i] = jnp.where(keep, pows[s - 1][0], 0.0)
            const_ref[d, 2 * i + 1] = jnp.where(keep, pows[s - 1][1], 0.0)
        lp_re = jnp.zeros((S5_SUBLANES, n), F32)
        lp_im = jnp.zeros((S5_SUBLANES, n), F32)
        for r in range(S5_SUBLANES):
            p = pows[r] if d == 0 else pows[S5_SUBLANES - 1 - r]
            lp_re = jnp.where(row == r, p[0], lp_re)
            lp_im = jnp.where(row == r, p[1], lp_im)
        const_ref[d, 6] = lp_re
        const_ref[d, 7] = lp_im
    cmat_ref[...] = jnp.concatenate([cre_ref[...], -cim_ref[...]], axis=0).astype(BF16)


def _s5_prepare(a_re, a_im, log_dt, b_re, b_im, c_re, c_im):
    n, k_in = S5_BLOCK_STATE, S5_BLOCK_IN
    gpb = C_GROUPS // S5_BLOCKS
    per_state = lambda t: t.reshape(N_DIR, S5_BLOCKS, 1, n)
    ldt = jnp.repeat(log_dt, C_STATE, axis=-1)
    eye = jnp.eye(gpb, dtype=F32)

    def expand_b(b):
        b = b.reshape(S5_BLOCKS, gpb, C_STATE, C_GROUP).transpose(0, 1, 3, 2)
        return (b[:, :, :, None, :] * eye[None, :, None, :, None]).reshape(S5_BLOCKS, k_in, n)

    def expand_c(c):
        c = c.reshape(S5_BLOCKS, gpb, C_GROUP, C_STATE).transpose(0, 1, 3, 2)
        return (c[:, :, :, None, :] * eye[None, :, None, :, None]).reshape(S5_BLOCKS, n, k_in)

    row_spec = pl.BlockSpec((N_DIR, None, 1, n), lambda k: (0, k, 0, 0))
    return pl.pallas_call(
        _s5_prep_kernel,
        grid=(S5_BLOCKS,),
        in_specs=[row_spec, row_spec, row_spec,
                  pl.BlockSpec((None, k_in, n), lambda k: (k, 0, 0)),
                  pl.BlockSpec((None, k_in, n), lambda k: (k, 0, 0)),
                  pl.BlockSpec((None, n, k_in), lambda k: (k, 0, 0)),
                  pl.BlockSpec((None, n, k_in), lambda k: (k, 0, 0))],
        out_specs=[pl.BlockSpec((N_DIR, None, k_in, 2 * n), lambda k: (0, k, 0, 0)),
                   pl.BlockSpec((None, 2 * n, k_in), lambda k: (k, 0, 0)),
                   pl.BlockSpec((N_DIR, None, 8, S5_SUBLANES, n), lambda k: (0, k, 0, 0, 0))],
        out_shape=[jax.ShapeDtypeStruct((N_DIR, S5_BLOCKS, k_in, 2 * n), BF16),
                   jax.ShapeDtypeStruct((S5_BLOCKS, 2 * n, k_in), BF16),
                   jax.ShapeDtypeStruct((N_DIR, S5_BLOCKS, 8, S5_SUBLANES, n), F32)],
        compiler_params=_params(1),
        name="s5_prepare",
    )(per_state(a_re), per_state(a_im), per_state(ldt), expand_b(b_re), expand_b(b_im), expand_c(c_re),
      expand_c(c_im))


def _s5_kernel(seq_len, has_state, u_ref, bmat_ref, cmat_ref, const_ref, *rest):
    if has_state:
        s0_ref, y_ref, fin_ref, xre_ref, xim_ref = rest
    else:
        y_ref, fin_ref, xre_ref, xim_ref = rest
    n = S5_BLOCK_STATE
    n_tiles = seq_len // S5_SUBLANES
    u = u_ref[...].astype(BF16)
    for d in range(N_DIR):
        bu = jnp.dot(u, bmat_ref[d], preferred_element_type=F32)
        xre_ref[d] = bu[:, :n]
        xim_ref[d] = bu[:, n:]

    for cb in range(n // S5_SCAN_COLS):
        cols = slice(cb * S5_SCAN_COLS, (cb + 1) * S5_SCAN_COLS)

        def scan_tile(d, i, carry):
            rows = pl.ds(pl.multiple_of(i * S5_SUBLANES, S5_SUBLANES), S5_SUBLANES)
            xr = xre_ref[d, rows, cols]
            xi = xim_ref[d, rows, cols]
            for k, s in enumerate(S5_SHIFTS):
                shift = s if d == 0 else S5_SUBLANES - s
                pr, pi = _cmul(const_ref[d, 2 * k, :, cols], const_ref[d, 2 * k + 1, :, cols],
                               pltpu.roll(xr, shift, 0), pltpu.roll(xi, shift, 0))
                xr, xi = xr + pr, xi + pi
            cr, ci = _cmul(const_ref[d, 6, :, cols], const_ref[d, 7, :, cols], carry[0], carry[1])
            xr, xi = xr + cr, xi + ci
            xre_ref[d, rows, cols] = xr
            xim_ref[d, rows, cols] = xi
            edge = S5_SUBLANES - 1 if d == 0 else 0
            shape = (S5_SUBLANES, S5_SCAN_COLS)
            return (jnp.broadcast_to(xr[edge:edge + 1], shape), jnp.broadcast_to(xi[edge:edge + 1], shape))

        def body(i, carry):
            return (scan_tile(0, i, carry[0]), scan_tile(1, n_tiles - 1 - i, carry[1]))

        shape = (S5_SUBLANES, S5_SCAN_COLS)
        if has_state:
            init = tuple((jnp.broadcast_to(s0_ref[2 * d:2 * d + 1, cols], shape),
                          jnp.broadcast_to(s0_ref[2 * d + 1:2 * d + 2, cols], shape)) for d in range(N_DIR))
        else:
            init = tuple((jnp.zeros(shape, F32), jnp.zeros(shape, F32)) for _ in range(N_DIR))
        fin = lax.fori_loop(0, n_tiles, body, init)
        for d in range(N_DIR):
            fin_ref[2 * d:2 * d + 1, cols] = fin[d][0][0:1]
            fin_ref[2 * d + 1:2 * d + 2, cols] = fin[d][1][0:1]

    x = jnp.concatenate([xre_ref[0] + xre_ref[1], xim_ref[0] + xim_ref[1]], axis=1).astype(BF16)
    y_ref[...] = jnp.dot(x, cmat_ref[...], preferred_element_type=F32)


def _s5_mixer(grp, proj, prep, s0):
    bmat, cmat, consts = prep
    n, k_in, seq_len = S5_BLOCK_STATE, S5_BLOCK_IN, grp.seq_len
    has_state = s0 is not None
    state_spec = pl.BlockSpec((None, None, 2 * N_DIR, n), lambda b, k: (b, k, 0, 0))
    in_specs = [pl.BlockSpec((seq_len, k_in), lambda b, k: (b, k)),
                pl.BlockSpec((N_DIR, None, k_in, 2 * n), lambda b, k: (0, k, 0, 0)),
                pl.BlockSpec((None, 2 * n, k_in), lambda b, k: (k, 0, 0)),
                pl.BlockSpec((N_DIR, None, 8, S5_SUBLANES, n), lambda b, k: (0, k, 0, 0, 0))]
    args = [proj, bmat, cmat, consts]
    if has_state:
        in_specs.append(state_spec)
        args.append(s0)
    return pl.pallas_call(
        functools.partial(_s5_kernel, seq_len, has_state),
        grid=(grp.n_seq, S5_BLOCKS),
        in_specs=in_specs,
        out_specs=[pl.BlockSpec((seq_len, k_in), lambda b, k: (b, k)), state_spec],
        out_shape=[jax.ShapeDtypeStruct((grp.n_seq * seq_len, C_WIDTH), F32),
                   jax.ShapeDtypeStruct((grp.n_seq, S5_BLOCKS, 2 * N_DIR, n), F32)],
        scratch_shapes=[pltpu.VMEM((N_DIR, seq_len, n), F32), pltpu.VMEM((N_DIR, seq_len, n), F32)],
        compiler_params=_params(2),
        name="s5_scan",
    )(*args)


def _s5_state_to_blocks(s):
    b = s.shape[0]
    s = s.reshape(b, N_DIR, S5_BLOCKS, S5_BLOCK_STATE, 2).transpose(0, 2, 1, 4, 3)
    return s.reshape(b, S5_BLOCKS, 2 * N_DIR, S5_BLOCK_STATE)


def _s5_state_from_blocks(s):
    b = s.shape[0]
    s = s.reshape(b, S5_BLOCKS, N_DIR, 2, S5_BLOCK_STATE).transpose(0, 2, 1, 4, 3)
    return s.reshape(b, N_DIR, C_GROUPS, C_STATE, 2)


GLA_BLOCK = D_CHUNK
GLA_KEY_TILE = 128
ODD_IN_PAD = _round_up(sum(ODD_SPLITS), LANES)
_O_U, _O_Q, _O_K, _O_V, _O_G, _O_A = (int(v) for v in np.cumsum((0,) + ODD_SPLITS[:-1]))
GLA_QK = D_HEADS * D_DK
GLA_V = D_HEADS * D_DV


def _split3(x):
    x1 = x.astype(BF16)
    r1 = x - x1.astype(F32)
    x2 = r1.astype(BF16)
    x3 = (r1 - x2.astype(F32)).astype(BF16)
    return x1, x2, x3


def _dot_01(m01, x):
    p1, p2, p3 = _split3(x)
    dot = lambda p: jnp.dot(m01, p, preferred_element_type=F32)
    return dot(p1) + dot(p2) + dot(p3)


def _dot_nt(a, b):
    return lax.dot_general(a, b, (((1,), (1,)), ((), ())), preferred_element_type=F32)


def _gla_kernel(seq_len, has_state, proj_ref, ypre_ref, dskip_ref, wglu_ref, wg2_ref, bgate_ref, gain_ref,
                hsum_ref, *rest):
    if has_state:
        s0_ref, mix_ref, bcum_ref, la_ref, o_ref, s0pad_ref = rest
    else:
        mix_ref, fin_ref, bcum_ref, la_ref, o_ref = rest
    n_blocks = seq_len // GLA_BLOCK
    n_key_tiles = seq_len // GLA_KEY_TILE

    u = proj_ref[:, _O_U:_O_U + C_WIDTH]
    y = jax.nn.gelu(ypre_ref[...] + dskip_ref[...] * u)
    glu = jnp.dot(y.astype(BF16), wglu_ref[...], preferred_element_type=F32)
    mix_ref[:, 0:C_WIDTH] = y * jax.nn.sigmoid(glu)

    logits = jnp.dot(proj_ref[:, _O_A:_O_A + LANES], wg2_ref[...], preferred_element_type=F32,
                     precision=lax.Precision.HIGHEST) + bgate_ref[...]
    la = (jnp.minimum(logits, 0.0) - jnp.log(1.0 + jnp.exp(-jnp.abs(logits)))) / D_TAU
    la_ref[...] = la
    r_i = lax.broadcasted_iota(jnp.int32, (seq_len, seq_len), 0)
    c_i = lax.broadcasted_iota(jnp.int32, (seq_len, seq_len), 1)
    bcum_ref[0] = _dot_01(jnp.where(r_i >= c_i, 1.0, 0.0).astype(BF16), la[:, :GLA_QK])
    bcum_ref[1] = _dot_01(jnp.where(r_i <= c_i, 1.0, 0.0).astype(BF16), la[:, GLA_QK:])

    if has_state:
        zeros = jnp.zeros((D_DK, D_DV), F32)
        for d in range(N_DIR):
            for h in range(D_HEADS):
                s = s0_ref[d, h]
                s0pad_ref[d, h] = jnp.concatenate([s, zeros] if h % 2 == 0 else [zeros, s], axis=0).astype(BF16)

    row_b = lax.broadcasted_iota(jnp.int32, (GLA_BLOCK, GLA_QK), 0)
    lane_pair = lax.broadcasted_iota(jnp.int32, (GLA_BLOCK, LANES), 1)
    key_lane = lax.broadcasted_iota(jnp.int32, (GLA_BLOCK, GLA_KEY_TILE), 1)
    scale = D_DK ** -0.5

    def block(i, _):
        r0 = pl.multiple_of(i * GLA_BLOCK, GLA_BLOCK)
        rows = pl.ds(r0, GLA_BLOCK)
        q = proj_ref[rows, _O_Q:_O_Q + GLA_QK] * scale
        k = proj_ref[rows, _O_K:_O_K + GLA_QK]
        v = proj_ref[rows, _O_V:_O_V + GLA_V]
        o = [jnp.zeros((GLA_BLOCK, D_DV), F32) for _ in range(D_HEADS)]
        for d in range(N_DIR):
            b = bcum_ref[d, rows, :]
            la_blk = la_ref[rows, d * GLA_QK:(d + 1) * GLA_QK]
            edge = 0 if d == 0 else GLA_BLOCK - 1
            ref = b[edge:edge + 1] - la_blk[edge:edge + 1]
            parts = []
            for j in range(GLA_BLOCK):
                w = q * k[j:j + 1] * jnp.exp(jnp.minimum(b - b[j:j + 1], 0.0))
                keep = (row_b >= j) if d == 0 else (row_b <= j)
                parts.append(jnp.where(keep, w, 0.0))
            w_all = jnp.concatenate(parts, axis=0).astype(BF16)
            s_all = jnp.dot(w_all, hsum_ref[...], preferred_element_type=F32)
            for j in range(GLA_BLOCK):
                sj = s_all[j * GLA_BLOCK:(j + 1) * GLA_BLOCK]
                for h in range(D_HEADS):
                    o[h] = o[h] + sj[:, h * D_DV:(h + 1) * D_DV] * v[j:j + 1, h * D_DV:(h + 1) * D_DV]

            qt = q * jnp.exp(b - ref)
            q_heads = []
            for h in range(D_HEADS):
                pair = qt[:, (h // 2) * LANES:(h // 2 + 1) * LANES]
                mine = (lane_pair < D_DK) if h % 2 == 0 else (lane_pair >= D_DK)
                q_heads.append(jnp.where(mine, pair, 0.0).astype(BF16))

            def key_tile(jt, acc, d=d, ref=ref, q_heads=q_heads):
                j0 = pl.multiple_of(jt * GLA_KEY_TILE, GLA_KEY_TILE)
                rows_j = pl.ds(j0, GLA_KEY_TILE)
                kj = proj_ref[rows_j, _O_K:_O_K + GLA_QK]
                kt = (kj * jnp.exp(jnp.minimum(ref - bcum_ref[d, rows_j, :], 0.0))).astype(BF16)
                valid = (key_lane + j0 < r0) if d == 0 else (key_lane + j0 > r0 + GLA_BLOCK - 1)
                out = []
                for h in range(D_HEADS):
                    s = _dot_nt(q_heads[h], kt[:, (h // 2) * LANES:(h // 2 + 1) * LANES])
                    s = jnp.where(valid, s, 0.0).astype(BF16)
                    vj = proj_ref[rows_j, _O_V + h * D_DV:_O_V + (h + 1) * D_DV].astype(BF16)
                    out.append(acc[h] + jnp.dot(s, vj, preferred_element_type=F32))
                return tuple(out)

            if d == 0:
                lo, hi = 0, (r0 + GLA_KEY_TILE - 1) // GLA_KEY_TILE
            else:
                lo, hi = (r0 + GLA_BLOCK) // GLA_KEY_TILE, n_key_tiles
            o = list(lax.fori_loop(lo, hi, key_tile, tuple(o)))

            if has_state:
                qs = (q * jnp.exp(b)).astype(BF16)
                for h in range(D_HEADS):
                    o[h] = o[h] + jnp.dot(qs[:, (h // 2) * LANES:(h // 2 + 1) * LANES], s0pad_ref[d, h],
                                          preferred_element_type=F32)
        o_ref[rows, :] = jnp.concatenate(o, axis=1)
        return 0

    lax.fori_loop(0, n_blocks, block, 0)

    for h in range(D_HEADS):
        cols = slice(h * D_DV, (h + 1) * D_DV)
        oh = o_ref[:, cols]
        oh = oh * lax.rsqrt(jnp.mean(oh * oh, axis=-1, keepdims=True) + RMS_EPS) * gain_ref[...]
        g = proj_ref[:, _O_G + h * D_DV:_O_G + (h + 1) * D_DV]
        mix_ref[:, C_WIDTH + h * D_DV:C_WIDTH + (h + 1) * D_DV] = oh * (g * jax.nn.sigmoid(g))

    if not has_state:
        k_all = proj_ref[:, _O_K:_O_K + GLA_QK]
        for d in range(N_DIR):
            last = seq_len - 1 if d == 0 else 0
            k_out = k_all * jnp.exp(bcum_ref[d, last:last + 1, :] - bcum_ref[d])
            k_out_t = k_out.T.astype(BF16)
            for h in range(D_HEADS):
                vh = proj_ref[:, _O_V + h * D_DV:_O_V + (h + 1) * D_DV].astype(BF16)
                fin_ref[d, h] = jnp.dot(k_out_t[h * D_DK:(h + 1) * D_DK], vh, preferred_element_type=F32)


def _gla_mixer(grp, proj, ypre, d_skip, w_glu, w_gate2, b_gate, gain, s0):
    seq_len = grp.seq_len
    has_state = s0 is not None
    wg2 = jnp.zeros((LANES, N_DIR * GLA_QK), F32)
    for d in range(N_DIR):
        wg2 = wg2.at[d * D_RANK:(d + 1) * D_RANK, d * GLA_QK:(d + 1) * GLA_QK].set(w_gate2[d])
    hsum = jnp.repeat(jnp.repeat(jnp.eye(D_HEADS, dtype=BF16), D_DK, axis=0), D_DV, axis=1)
    full = lambda *shape: pl.BlockSpec(shape, lambda b: (0,) * len(shape))
    state_spec = pl.BlockSpec((None, N_DIR, D_HEADS, D_DK, D_DV), lambda b: (b, 0, 0, 0, 0))
    in_specs = [pl.BlockSpec((seq_len, ODD_IN_PAD), lambda b: (b, 0)),
                pl.BlockSpec((seq_len, C_WIDTH), lambda b: (b, 0)),
                full(1, C_WIDTH), full(C_WIDTH, C_WIDTH), full(LANES, N_DIR * GLA_QK), full(1, N_DIR * GLA_QK),
                full(1, D_DV), full(GLA_QK, GLA_V)]
    args = [proj, ypre, d_skip[None], w_glu.astype(BF16), wg2, b_gate.reshape(1, N_DIR * GLA_QK), gain[None], hsum]
    mix_spec = pl.BlockSpec((seq_len, D_MODEL), lambda b: (b, 0))
    mix_shape = jax.ShapeDtypeStruct((grp.n_seq * seq_len, D_MODEL), F32)
    scratch = [pltpu.VMEM((N_DIR, seq_len, GLA_QK), F32), pltpu.VMEM((seq_len, N_DIR * GLA_QK), F32),
               pltpu.VMEM((seq_len, GLA_V), F32)]
    if has_state:
        in_specs.append(state_spec)
        args.append(s0)
        out_specs, out_shape = mix_spec, mix_shape
        scratch.append(pltpu.VMEM((N_DIR, D_HEADS, 2 * D_DK, D_DV), BF16))
    else:
        out_specs = [mix_spec, state_spec]
        out_shape = [mix_shape, jax.ShapeDtypeStruct((grp.n_seq, N_DIR, D_HEADS, D_DK, D_DV), F32)]
    out = pl.pallas_call(
        functools.partial(_gla_kernel, seq_len, has_state),
        grid=(grp.n_seq,),
        in_specs=in_specs, out_specs=out_specs, out_shape=out_shape, scratch_shapes=scratch,
        compiler_params=_params(1),
        name="gla_mixer",
    )(*args)
    return (out, None) if has_state else (out[0], out[1])


EV_A = A_HEADS * A_DK
EV_B = B_HEADS * B_DK
EV_HALF = 4 * EV_A
EVEN_IN_PAD = 2 * EV_HALF + LANES
GDN_PAIR = 2 * A_CHUNK
GDN_LEVELS = 5


def _even_col_perm():
    off = np.cumsum((0,) + EVEN_SPLITS)
    seg = lambda i: np.arange(off[i], off[i + 1])
    order = [0, 1, 2, 3, 6, 7, 8, 9, 4, 5]
    return np.concatenate([seg(i) for i in order])


def _mm3(a, b):
    a1 = a.astype(BF16)
    a2 = (a - a1.astype(F32)).astype(BF16)
    b1 = b.astype(BF16)
    b2 = (b - b1.astype(F32)).astype(BF16)
    dot = lambda x, y: jnp.dot(x, y, preferred_element_type=F32)
    return dot(a1, b1) + (dot(a1, b2) + dot(a2, b1))


def _softplus(x):
    return jnp.maximum(x, 0.0) + jnp.log(1.0 + jnp.exp(-jnp.abs(x)))


def _silu(x):
    return x * jax.nn.sigmoid(x)


def _gdn_kernel(seq_len, has_state, proj_ref, small_ref, convw_ref, alog_ref, dtb_ref, gain_ref, *rest):
    if has_state:
        s0_ref, out_ref, q_s, k_s, v_s, gb_s, gcum_s, o_s, st_s = rest
    else:
        out_ref, fin_ref, q_s, k_s, v_s, gb_s, gcum_s, o_s, st_s = rest
    n_pairs = seq_len // GDN_PAIR
    width = 3 * EV_A

    x = proj_ref[:, 0:width]
    t_idx = lax.broadcasted_iota(jnp.int32, (seq_len, 1), 0)
    pad = A_CONV // 2
    acc = x * convw_ref[pad:pad + 1, :]
    for s in range(-pad, pad + 1):
        if s == 0:
            continue
        shifted = pltpu.roll(x, (-s) % seq_len, 0)
        inside = (t_idx + s >= 0) & (t_idx + s < seq_len)
        acc = acc + jnp.where(inside, shifted, 0.0) * convw_ref[pad + s:pad + s + 1, :]
    y = _silu(acc)
    for h in range(A_HEADS):
        cols = slice(h * A_DK, (h + 1) * A_DK)
        qh = y[:, h * A_DK:(h + 1) * A_DK]
        kh = y[:, EV_A + h * A_DK:EV_A + (h + 1) * A_DK]
        q_s[:, cols] = qh * lax.rsqrt(jnp.sum(qh * qh, axis=-1, keepdims=True) + 1e-6) * (A_DK ** -0.5)
        k_s[:, cols] = kh * lax.rsqrt(jnp.sum(kh * kh, axis=-1, keepdims=True) + 1e-6)
    v_s[...] = y[:, 2 * EV_A:3 * EV_A]

    small = small_ref[...]
    lane = lax.broadcasted_iota(jnp.int32, small.shape, 1)
    gb = jnp.where(lane < N_DIR * A_HEADS, jax.nn.sigmoid(small),
                   -jnp.exp(alog_ref[...]) * _softplus(small + dtb_ref[...]))
    gb_s[...] = gb
    r_i = lax.broadcasted_iota(jnp.int32, (seq_len, seq_len), 0)
    c_i = lax.broadcasted_iota(jnp.int32, (seq_len, seq_len), 1)
    same = (r_i // A_CHUNK) == (c_i // A_CHUNK)
    gcum_s[0] = _dot_01(jnp.where(same & (r_i >= c_i), 1.0, 0.0).astype(BF16), gb)
    gcum_s[1] = _dot_01(jnp.where(same & (r_i <= c_i), 1.0, 0.0).astype(BF16), gb)

    o_s[...] = jnp.zeros(o_s.shape, F32)
    for d in range(N_DIR):
        for h in range(A_HEADS):
            st_s[d, h] = s0_ref[d, h] if has_state else jnp.zeros((A_DK, A_DV), F32)

    pr = lax.broadcasted_iota(jnp.int32, (GDN_PAIR, GDN_PAIR), 0)
    pc = lax.broadcasted_iota(jnp.int32, (GDN_PAIR, GDN_PAIR), 1)
    p_same = (pr // A_CHUNK) == (pc // A_CHUNK)
    eye = jnp.where(pr == pc, 1.0, 0.0)
    row_p = lax.broadcasted_iota(jnp.int32, (GDN_PAIR, 1), 0)
    zeros_c = jnp.zeros((A_CHUNK, A_DV), F32)

    def pair(p, _):
        for d in range(N_DIR):
            base = pl.multiple_of((p if d == 0 else n_pairs - 1 - p) * GDN_PAIR, GDN_PAIR)
            rows = pl.ds(base, GDN_PAIR)
            gbp = gb_s[rows, :]
            gp = gcum_s[d, rows, :]
            gp_t = gp.T
            tri = (pr >= pc) if d == 0 else (pr <= pc)
            strict = (pr > pc) if d == 0 else (pr < pc)
            for h in range(A_HEADS):
                cols = slice(h * A_DK, (h + 1) * A_DK)
                bi, gi = d * A_HEADS + h, N_DIR * A_HEADS + d * A_HEADS + h
                qp, kp, vp = q_s[rows, cols], k_s[rows, cols], v_s[rows, cols]
                beta = gbp[:, bi:bi + 1]
                g_col = gp[:, gi:gi + 1]
                g_row = gp_t[gi:gi + 1, :]
                decay = jnp.where(p_same & tri, jnp.exp(jnp.minimum(g_col - g_row, 0.0)), 0.0)
                kb = kp * beta
                kp16 = kp.astype(BF16)
                m = jnp.where(p_same & strict, -(_dot_nt(kb.astype(BF16), kp16) * decay), 0.0)
                t_inv = eye + m
                for _ in range(GDN_LEVELS):
                    m = _mm3(m, m)
                    t_inv = t_inv + _mm3(t_inv, m)
                e_g = jnp.exp(g_col)
                rhs = jnp.concatenate([vp * beta, kb * e_g], axis=1).astype(BF16)
                uw = jnp.dot(t_inv.astype(BF16), rhs, preferred_element_type=F32)
                u, w = uw[:, :A_DV], uw[:, A_DV:]
                attn = (_dot_nt(qp.astype(BF16), kp16) * decay).astype(BF16)
                q_in = (qp * e_g).astype(BF16)
                last = A_CHUNK - 1 if d == 0 else 0
                g_last = [g_col[c * A_CHUNK + last:c * A_CHUNK + last + 1] for c in range(2)]
                gl_col = jnp.where(row_p < A_CHUNK, g_last[0], g_last[1])
                k_out_t = (kp * jnp.exp(gl_col - g_col)).T.astype(BF16)
                s = st_s[d, h]
                for c in ((0, 1) if d == 0 else (1, 0)):
                    rs = slice(c * A_CHUNK, (c + 1) * A_CHUNK)
                    s16 = s.astype(BF16)
                    v_new = u[rs] - jnp.dot(w[rs].astype(BF16), s16, preferred_element_type=F32)
                    v_full = jnp.concatenate([v_new, zeros_c] if c == 0 else [zeros_c, v_new], axis=0).astype(BF16)
                    o_c = (jnp.dot(q_in[rs], s16, preferred_element_type=F32)
                           + jnp.dot(attn[rs], v_full, preferred_element_type=F32))
                    s = s * jnp.exp(g_last[c]) + jnp.dot(k_out_t, v_full, preferred_element_type=F32)
                    rows_c = pl.ds(base + c * A_CHUNK, A_CHUNK)
                    o_s[rows_c, cols] = o_s[rows_c, cols] + o_c
                st_s[d, h] = s
        return 0

    lax.fori_loop(0, n_pairs, pair, 0)

    for h in range(A_HEADS):
        cols = slice(h * A_DV, (h + 1) * A_DV)
        o = o_s[:, cols]
        o = o * lax.rsqrt(jnp.mean(o * o, axis=-1, keepdims=True) + RMS_EPS) * gain_ref[...]
        out_ref[:, cols] = o * _silu(proj_ref[:, 3 * EV_A + h * A_DV:3 * EV_A + (h + 1) * A_DV])
    if not has_state:
        for d in range(N_DIR):
            for h in range(A_HEADS):
                fin_ref[d, h] = st_s[d, h]


def _gdn_mixer(grp, proj, conv_w, a_log, dt_bias, gain, s0):
    seq_len = grp.seq_len
    has_state = s0 is not None
    n_small = N_DIR * A_HEADS
    lane_row = lambda v: jnp.zeros((1, LANES), F32).at[0, n_small:2 * n_small].set(v.reshape(n_small))
    convw = jnp.zeros((8, 3 * EV_A), F32).at[:A_CONV].set(conv_w)
    full = lambda *shape: pl.BlockSpec(shape, lambda b: (0,) * len(shape))
    state_spec = pl.BlockSpec((None, N_DIR, A_HEADS, A_DK, A_DV), lambda b: (b, 0, 0, 0, 0))
    in_specs = [pl.BlockSpec((seq_len, EV_HALF), lambda b: (b, 0)),
                pl.BlockSpec((seq_len, LANES), lambda b: (b, 2 * EV_HALF // LANES)),
                full(8, 3 * EV_A), full(1, LANES), full(1, LANES), full(1, A_DV)]
    args = [proj, proj, convw, lane_row(a_log), lane_row(dt_bias), gain[None]]
    out_spec = pl.BlockSpec((seq_len, EV_A), lambda b: (b, 0))
    out_shape = jax.ShapeDtypeStruct((grp.n_seq * seq_len, EV_A), F32)
    if has_state:
        in_specs.append(state_spec)
        args.append(s0)
        out_specs, out_shapes = out_spec, out_shape
    else:
        out_specs = [out_spec, state_spec]
        out_shapes = [out_shape, jax.ShapeDtypeStruct((grp.n_seq, N_DIR, A_HEADS, A_DK, A_DV), F32)]
    tok = lambda w: pltpu.VMEM((seq_len, w), F32)
    out = pl.pallas_call(
        functools.partial(_gdn_kernel, seq_len, has_state),
        grid=(grp.n_seq,),
        in_specs=in_specs, out_specs=out_specs, out_shape=out_shapes,
        scratch_shapes=[tok(EV_A), tok(EV_A), tok(EV_A), tok(LANES), pltpu.VMEM((N_DIR, seq_len, LANES), F32),
                        tok(EV_A), pltpu.VMEM((N_DIR, A_HEADS, A_DK, A_DV), F32)],
        compiler_params=_params(1),
        name="gdn_mixer",
    )(*args)
    return (out, None) if has_state else (out[0], out[1])


def _rope_tables(n_tokens):
    lane = np.arange(EV_B)
    axis = (lane % B_DK) // (2 * ROPE_FREQS)
    half = (lane % (2 * ROPE_FREQS)) // ROPE_FREQS
    freq = ROPE_BASE ** (-(lane % ROPE_FREQS).astype(np.float32) / ROPE_FREQS)
    tok = jnp.arange(n_tokens, dtype=F32)
    pos = jnp.where(jnp.asarray(axis)[None, :] == 0, jnp.floor(tok / GRID_W)[:, None], (tok % GRID_W)[:, None])
    ang = pos * jnp.asarray(freq, F32)[None, :]
    sign = jnp.asarray(np.where(half == 0, -1.0, 1.0), F32)[None, :]
    return jnp.cos(ang), jnp.sin(ang) * sign


def _dot3_lhs(x, m):
    p1, p2, p3 = _split3(x)
    dot = lambda p: jnp.dot(p, m, preferred_element_type=F32)
    return dot(p1) + dot(p2) + dot(p3)


def _ret_kernel(seq_len, has_state, rope, proj_ref, lgd_ref, gain_ref, *rest):
    rest = list(rest)
    cos_ref, sin_ref = (rest.pop(0), rest.pop(0)) if rope else (None, None)
    s0_ref = rest.pop(0) if has_state else None
    out_ref = rest.pop(0)
    fin_ref = None if has_state else rest.pop(0)

    q = proj_ref[:, 0:EV_B]
    k = proj_ref[:, EV_B:2 * EV_B]
    if rope:
        lane = lax.broadcasted_iota(jnp.int32, (seq_len, EV_B), 1)
        first = (lane % (2 * ROPE_FREQS)) < ROPE_FREQS

        def rotate(x):
            partner = jnp.where(first, pltpu.roll(x, EV_B - ROPE_FREQS, 1), pltpu.roll(x, ROPE_FREQS, 1))
            return x * cos_ref[...] + partner * sin_ref[...]

        q, k = rotate(q), rotate(k)
    q = q * (B_DK ** -0.5)
    log_gamma = -jnp.exp(lgd_ref[...])
    i_col = lax.broadcasted_iota(jnp.int32, (seq_len, 1), 0).astype(F32)
    dist = (lax.broadcasted_iota(jnp.int32, (seq_len, seq_len), 0)
            - lax.broadcasted_iota(jnp.int32, (seq_len, seq_len), 1)).astype(F32)
    lane_p = lax.broadcasted_iota(jnp.int32, (1, LANES), 1)
    h_avg = jnp.where((lax.broadcasted_iota(jnp.int32, (LANES, LANES), 0) // B_DV)
                      == (lax.broadcasted_iota(jnp.int32, (LANES, LANES), 1) // B_DV), 1.0 / B_DV, 0.0).astype(BF16)

    for pair in range(B_HEADS // 2):
        cols = slice(pair * LANES, (pair + 1) * LANES)
        qp, kp = q[:, cols], k[:, cols]
        vp = proj_ref[:, 2 * EV_B + pair * LANES:2 * EV_B + (pair + 1) * LANES]
        kp16, vp16 = kp.astype(BF16), vp.astype(BF16)
        o = jnp.zeros((seq_len, LANES), F32)
        lg = [[log_gamma[d:d + 1, 2 * pair + e:2 * pair + e + 1] for e in range(2)] for d in range(N_DIR)]
        for e in range(2):
            mine = (lane_p < B_DK) if e == 0 else (lane_p >= B_DK)
            scores = _dot_nt(jnp.where(mine, qp, 0.0).astype(BF16), kp16)
            weight = (jnp.where(dist >= 0.0, jnp.exp(jnp.maximum(dist, 0.0) * lg[0][e]), 0.0)
                      + jnp.where(dist <= 0.0, jnp.exp(jnp.maximum(-dist, 0.0) * lg[1][e]), 0.0))
            o = o + jnp.dot((scores * weight).astype(BF16), jnp.where(mine, vp, 0.0).astype(BF16),
                            preferred_element_type=F32)
        if has_state:
            for d in range(N_DIR):
                steps = (i_col + 1.0) if d == 0 else (seq_len - i_col)
                xi = jnp.where(lane_p < B_DK, jnp.exp(steps * lg[d][0]), jnp.exp(steps * lg[d][1]))
                zeros = jnp.zeros((B_DK, B_DV), F32)
                s_pair = jnp.concatenate(
                    [jnp.concatenate([s0_ref[d, 2 * pair], zeros], axis=1),
                     jnp.concatenate([zeros, s0_ref[d, 2 * pair + 1]], axis=1)], axis=0)
                o = o + jnp.dot((qp * xi).astype(BF16), s_pair.astype(BF16), preferred_element_type=F32)
        else:
            for d in range(N_DIR):
                steps = (seq_len - 1.0 - i_col) if d == 0 else i_col
                zeta = jnp.where(lane_p < B_DK, jnp.exp(steps * lg[d][0]), jnp.exp(steps * lg[d][1]))
                kz_t = (kp * zeta).T.astype(BF16)
                both = jnp.dot(kz_t, vp16, preferred_element_type=F32)
                fin_ref[d, 2 * pair] = both[:B_DK, :B_DV]
                fin_ref[d, 2 * pair + 1] = both[B_DK:, B_DV:]
        mu = _dot3_lhs(o, h_avg)
        cen = o - mu
        var = _dot3_lhs(cen * cen, h_avg)
        normed = cen * lax.rsqrt(var + RMS_EPS) * gain_ref[:, cols]
        out_ref[:, cols] = normed * _silu(proj_ref[:, 3 * EV_B + pair * LANES:3 * EV_B + (pair + 1) * LANES])


def _ret_mixer(grp, proj, log_decay, gain, s0, rope_tables):
    seq_len = grp.seq_len
    has_state = s0 is not None
    rope = rope_tables is not None
    lgd = jnp.zeros((8, LANES), F32).at[:N_DIR, :B_HEADS].set(log_decay)
    full = lambda *shape: pl.BlockSpec(shape, lambda b: (0,) * len(shape))
    state_spec = pl.BlockSpec((None, N_DIR, B_HEADS, B_DK, B_DV), lambda b: (b, 0, 0, 0, 0))
    in_specs = [pl.BlockSpec((seq_len, EV_HALF), lambda b: (b, 1)), full(8, LANES), full(1, EV_B)]
    args = [proj, lgd, gain[None]]
    if rope:
        in_specs += [full(seq_len, EV_B), full(seq_len, EV_B)]
        args += list(rope_tables)
    out_spec = pl.BlockSpec((seq_len, EV_B), lambda b: (b, 0))
    out_shape = jax.ShapeDtypeStruct((grp.n_seq * seq_len, EV_B), F32)
    if has_state:
        in_specs.append(state_spec)
        args.append(s0)
        out_specs, out_shapes = out_spec, out_shape
    else:
        out_specs = [out_spec, state_spec]
        out_shapes = [out_shape, jax.ShapeDtypeStruct((grp.n_seq, N_DIR, B_HEADS, B_DK, B_DV), F32)]
    out = pl.pallas_call(
        functools.partial(_ret_kernel, seq_len, has_state, rope),
        grid=(grp.n_seq,),
        in_specs=in_specs, out_specs=out_specs, out_shape=out_shapes,
        compiler_params=_params(1),
        name="retention_mixer",
    )(*args)
    return (out, None) if has_state else (out[0], out[1])


def _split_cols(t, sizes):
    return jnp.split(t, np.cumsum(sizes)[:-1].tolist(), axis=-1)


def _to_heads(t, n_heads):
    b, l, _ = t.shape
    return t.reshape(b, l, n_heads, -1).transpose(0, 2, 1, 3)


def _from_heads(t):
    b, h, l, d = t.shape
    return t.transpose(0, 2, 1, 3).reshape(b, l, h * d)


def _dir_heads(t, n_heads):
    b, l, _ = t.shape
    return t.reshape(b, l, N_DIR, n_heads, -1).transpose(2, 0, 3, 1, 4)


def _flip_seq(t):
    return jnp.flip(t, axis=2)


def _l2_normalize(t):
    return t * lax.rsqrt(jnp.sum(t * t, axis=-1, keepdims=True) + 1e-6)


def _head_rms_norm(o, gain):
    return o * lax.rsqrt(jnp.mean(o * o, axis=-1, keepdims=True) + RMS_EPS) * gain.astype(F32)


def _head_layer_norm(o):
    mu = jnp.mean(o, axis=-1, keepdims=True)
    var = jnp.mean(jnp.square(o - mu), axis=-1, keepdims=True)
    return (o - mu) * lax.rsqrt(var + RMS_EPS)


def _short_conv_silu(x, w):
    pad = w.shape[0] // 2
    y = lax.conv_general_dilated(x, w[:, None, :].astype(x.dtype), window_strides=(1,), padding=[(pad, pad)],
                                 dimension_numbers=('NWC', 'WIO', 'NWC'), feature_group_count=x.shape[-1])
    return jax.nn.silu(y)


def _grid_rope_angles(n_tokens):
    rows = n_tokens // GRID_W
    row = jnp.repeat(jnp.arange(rows, dtype=F32), GRID_W)
    col = jnp.tile(jnp.arange(GRID_W, dtype=F32), rows)
    freq = ROPE_BASE ** (-jnp.arange(ROPE_FREQS, dtype=F32) / ROPE_FREQS)
    return jnp.stack([row[:, None] * freq, col[:, None] * freq], axis=1)


def _apply_rope_2d(x, ang):
    xr = x.reshape(*x.shape[:-1], 2, 2, ROPE_FREQS)
    x1, x2 = xr[..., 0, :], xr[..., 1, :]
    cos, sin = jnp.cos(ang), jnp.sin(ang)
    return jnp.stack([x1 * cos - x2 * sin, x1 * sin + x2 * cos], axis=-2).reshape(x.shape)


def _gated_delta_chunked(q, k, v, beta, g, s0):
    bsz, nh, n_tok, dk = q.shape
    dv = v.shape[-1]
    cs = A_CHUNK
    nc = n_tok // cs
    q = (q * dk ** -0.5).reshape(bsz, nh, nc, cs, dk)
    k = k.reshape(bsz, nh, nc, cs, dk)
    v = v.reshape(bsz, nh, nc, cs, dv)
    beta = beta.reshape(bsz, nh, nc, cs, 1)
    g = jnp.cumsum(g.reshape(bsz, nh, nc, cs), axis=-1)
    pos = jnp.arange(cs)
    tri = pos[:, None] >= pos[None, :]
    strict = pos[:, None] > pos[None, :]
    decay = jnp.exp(jnp.where(tri, g[..., :, None] - g[..., None, :], -jnp.inf))
    k_beta = k * beta
    lmat = jnp.where(strict, jnp.einsum('bhnid,bhnjd->bhnij', k_beta, k) * decay, 0.0)
    eye = jnp.eye(cs, dtype=q.dtype)
    t_inv = lax.linalg.triangular_solve(eye + lmat, jnp.broadcast_to(eye, lmat.shape), left_side=True,
                                        lower=True, unit_diagonal=True)
    u = jnp.einsum('bhnij,bhnje->bhnie', t_inv, v * beta)
    w = jnp.einsum('bhnij,bhnjd->bhnid', t_inv, k_beta * jnp.exp(g)[..., None])
    attn = jnp.einsum('bhnid,bhnjd->bhnij', q, k) * decay
    q_in = q * jnp.exp(g)[..., None]
    g_last = g[..., -1]
    k_out = k * jnp.exp(g_last[..., None] - g)[..., None]

    def step(s, xs):
        q_c, k_c, u_c, w_c, a_c, gl = xs
        v_new = u_c - jnp.einsum('bhid,bhde->bhie', w_c, s)
        o = jnp.einsum('bhid,bhde->bhie', q_c, s) + jnp.einsum('bhij,bhje->bhie', a_c, v_new)
        s = s * jnp.exp(gl)[..., None, None] + jnp.einsum('bhjd,bhje->bhde', k_c, v_new)
        return s, o

    xs = tuple(jnp.moveaxis(t, 2, 0) for t in (q_in, k_out, u, w, attn, g_last))
    s_fin, o = lax.scan(step, s0, xs)
    return jnp.moveaxis(o, 0, 2).reshape(bsz, nh, n_tok, dv), s_fin


def _retention_chunked(q, k, v, log_gamma, s0):
    bsz, nh, n_tok, dk = q.shape
    dv = v.shape[-1]
    cs = B_CHUNK
    nc = n_tok // cs
    q = (q * dk ** -0.5).reshape(bsz, nh, nc, cs, dk)
    k = k.reshape(bsz, nh, nc, cs, dk)
    v = v.reshape(bsz, nh, nc, cs, dv)
    pos = jnp.arange(cs, dtype=F32)
    tri = pos[:, None] >= pos[None, :]
    lg = log_gamma[:, None, None]
    dmat = jnp.exp(jnp.where(tri, (pos[:, None] - pos[None, :]) * lg, -jnp.inf))
    xi = jnp.exp((pos + 1.0) * log_gamma[:, None])
    zeta = jnp.exp((cs - 1.0 - pos) * log_gamma[:, None])
    g_chunk = jnp.exp(cs * log_gamma)
    scores = jnp.einsum('bhnid,bhnjd->bhnij', q, k) * dmat[:, None]
    intra = jnp.einsum('bhnij,bhnje->bhnie', scores, v)
    k_sc = k * zeta[:, None, :, None]

    def step(s, xs):
        q_c, k_c, v_c = xs
        o = jnp.einsum('bhid,bhde->bhie', q_c, s) * xi[:, :, None]
        s = s * g_chunk[:, None, None] + jnp.einsum('bhjd,bhje->bhde', k_c, v_c)
        return s, o

    xs = tuple(jnp.moveaxis(t, 2, 0) for t in (q, k_sc, v))
    s_fin, cross = lax.scan(step, s0, xs)
    o = intra + jnp.moveaxis(cross, 0, 2)
    return o.reshape(bsz, nh, n_tok, dv), s_fin


def _gla_chunked(q, k, v, log_a, s0):
    bsz, nh, n_tok, dk = q.shape
    dv = v.shape[-1]
    cs = D_CHUNK
    nc = n_tok // cs
    q = (q * dk ** -0.5).reshape(bsz, nh, nc, cs, dk)
    k = k.reshape(bsz, nh, nc, cs, dk)
    v = v.reshape(bsz, nh, nc, cs, dv)
    b = jnp.cumsum(log_a.reshape(bsz, nh, nc, cs, dk), axis=3)
    pos = jnp.arange(cs)
    tri = pos[:, None] >= pos[None, :]
    decay = jnp.exp(jnp.where(tri[:, :, None], b[..., :, None, :] - b[..., None, :, :], -jnp.inf))
    scores = jnp.einsum('bhnid,bhnjd,bhnijd->bhnij', q, k, decay)
    intra = jnp.einsum('bhnij,bhnje->bhnie', scores, v)
    b_last = b[:, :, :, -1]
    q_in = q * jnp.exp(b)
    k_out = k * jnp.exp(b_last[:, :, :, None] - b)

    def step(s, xs):
        q_c, k_c, v_c, bl = xs
        o = jnp.einsum('bhid,bhde->bhie', q_c, s)
        s = s * jnp.exp(bl)[..., None] + jnp.einsum('bhjd,bhje->bhde', k_c, v_c)
        return s, o

    xs = tuple(jnp.moveaxis(t, 2, 0) for t in (q_in, k_out, v, b_last))
    s_fin, cross = lax.scan(step, s0, xs)
    o = intra + jnp.moveaxis(cross, 0, 2)
    return o.reshape(bsz, nh, n_tok, dv), s_fin


def _linear_recurrence_combine(e1, e2):
    a1, b1 = e1
    a2, b2 = e2
    return a1 * a2, a2 * b1 + b2


def _s5_scan(u, a_re, a_im, log_dt, b_re, b_im, s0):
    lam = lax.complex(jnp.minimum(a_re.astype(F32), S5_MAX_RE), a_im.astype(F32))
    dt = jnp.exp(log_dt.astype(F32))[:, None]
    lam_bar = jnp.exp(lam * dt)
    b_bar = ((lam_bar - 1.0) / lam)[..., None] * lax.complex(b_re.astype(F32), b_im.astype(F32))
    bu = jnp.einsum('gpi,blgi->blgp', b_bar, u.astype(jnp.complex64))
    bu = bu.at[:, 0].add(lam_bar * s0)
    decay = jnp.broadcast_to(lam_bar, bu.shape)
    _, states = lax.associative_scan(_linear_recurrence_combine, (decay, bu), axis=1)
    return states, states[:, -1]


def _even_mixer(proj, conv_w, a_log, dt_bias, gdn_gain, ret_log_decay, ret_gain, s0_gdn, s0_ret, rope_ang):
    qa, ka, va, za, ba, aa, qb, kb, vb, gb = _split_cols(proj, EVEN_SPLITS)
    qkv = _short_conv_silu(jnp.concatenate([qa, ka, va], axis=-1), conv_w)
    qa, ka, va = _split_cols(qkv, (A_HEADS * A_DK, A_HEADS * A_DK, A_HEADS * A_DV))
    qa = _l2_normalize(_to_heads(qa, A_HEADS))
    ka = _l2_normalize(_to_heads(ka, A_HEADS))
    va = _to_heads(va, A_HEADS)
    beta = jax.nn.sigmoid(_dir_heads(ba, A_HEADS)[..., 0])
    g = -jnp.exp(a_log)[:, None, :, None] * jax.nn.softplus(
        _dir_heads(aa, A_HEADS)[..., 0] + dt_bias[:, None, :, None])
    o_f, s_f = _gated_delta_chunked(qa, ka, va, beta[0], g[0], s0_gdn[:, 0])
    o_b, s_b = _gated_delta_chunked(_flip_seq(qa), _flip_seq(ka), _flip_seq(va), _flip_seq(beta[1]),
                                    _flip_seq(g[1]), s0_gdn[:, 1])
    out_a = _from_heads(_head_rms_norm(o_f + _flip_seq(o_b), gdn_gain)) * jax.nn.silu(za)
    qb = _to_heads(qb, B_HEADS)
    kb = _to_heads(kb, B_HEADS)
    vb = _to_heads(vb, B_HEADS)
    if rope_ang is not None:
        qb = _apply_rope_2d(qb, rope_ang)
        kb = _apply_rope_2d(kb, rope_ang)
    log_gamma = -jnp.exp(ret_log_decay)
    r_f, t_f = _retention_chunked(qb, kb, vb, log_gamma[0], s0_ret[:, 0])
    r_b, t_b = _retention_chunked(_flip_seq(qb), _flip_seq(kb), _flip_seq(vb), log_gamma[1], s0_ret[:, 1])
    out_b = (_from_heads(_head_layer_norm(r_f + _flip_seq(r_b))) * ret_gain) * jax.nn.silu(gb)
    return jnp.concatenate([out_a, out_b], axis=-1), jnp.stack([s_f, s_b], axis=1), jnp.stack([t_f, t_b], axis=1)


def _odd_mixer(proj, a_re, a_im, log_dt, b_re, b_im, c_re, c_im, d_skip, w_glu, w_gate2, b_gate, gla_gain,
               s0_s5, s0_gla):
    bsz, n_tok, _ = proj.shape
    u, qd, kd, vd, gd, alr = _split_cols(proj, ODD_SPLITS)
    uf = u.reshape(bsz, n_tok, C_GROUPS, C_GROUP)
    s0 = lax.complex(s0_s5[..., 0], s0_s5[..., 1])
    st_f, fin_f = _s5_scan(uf, a_re[0], a_im[0], log_dt[0], b_re, b_im, s0[:, 0])
    st_b, fin_b = _s5_scan(jnp.flip(uf, axis=1), a_re[1], a_im[1], log_dt[1], b_re, b_im, s0[:, 1])
    states = st_f + jnp.flip(st_b, axis=1)
    cmat = lax.complex(c_re, c_im)
    y = jnp.einsum('gip,blgp->blgi', cmat, states).real + d_skip.reshape(C_GROUPS, C_GROUP) * uf
    y = jax.nn.gelu(y.reshape(bsz, n_tok, C_WIDTH))
    out_c = y * jax.nn.sigmoid(y @ w_glu)
    fin = jnp.stack([fin_f, fin_b], axis=1)
    s5_state = jnp.stack([fin.real, fin.imag], axis=-1)
    q = _to_heads(qd, D_HEADS)
    k = _to_heads(kd, D_HEADS)
    v = _to_heads(vd, D_HEADS)
    alr = alr.reshape(bsz, n_tok, N_DIR, D_RANK)
    logits = jnp.einsum('blnr,nrk->nblk', alr, w_gate2) + b_gate[:, None, None, :]
    log_a = (jax.nn.log_sigmoid(logits) / D_TAU).reshape(N_DIR, bsz, n_tok, D_HEADS, D_DK).transpose(0, 1, 3, 2, 4)
    o_f, s_f = _gla_chunked(q, k, v, log_a[0], s0_gla[:, 0])
    o_b, s_b = _gla_chunked(_flip_seq(q), _flip_seq(k), _flip_seq(v), _flip_seq(log_a[1]), s0_gla[:, 1])
    out_d = _from_heads(_head_rms_norm(o_f + _flip_seq(o_b), gla_gain)) * jax.nn.silu(gd)
    return jnp.concatenate([out_c, out_d], axis=-1), s5_state, jnp.stack([s_f, s_b], axis=1)


def _pad_cols(w, n):
    return jnp.pad(w, ((0, 0), (0, n - w.shape[1])))


def kernel(x_prompt, x_sample, state_gdn, state_ret, state_s5, state_gla, c, c_ctx, ada_w, ada_b, norm_mix,
           norm_ffn, norm_final, ev_w_in, ev_conv, gdn_a_log, gdn_dt_bias, gdn_gain, ret_log_decay, ret_gain,
           ev_w_out, od_w_in, s5_a_re, s5_a_im, s5_log_dt, s5_b_re, s5_b_im, s5_c_re, s5_c_im, s5_d, s5_w_glu,
           gla_w_gate2, gla_b_gate, gla_gain, od_w_out, moe_router, moe_w_gate, moe_w_up, moe_w_down):
    cond = jnp.zeros((COND_ROWS, D_MODEL), F32).at[0].set(c_ctx).at[1:N_COND].set(c)
    mods = _modulation(cond, ada_w, ada_b)
    rope_tables = _rope_tables(DEC_SEQ)
    even_perm = _even_col_perm()

    xs = {CTX: x_prompt.reshape(BATCH * SEQ, D_MODEL), DEC: x_sample.reshape(DEC_BATCH * DEC_SEQ, D_MODEL)}
    new_states = {"gdn": [], "ret": [], "s5": [], "gla": []}

    for layer in range(DEPTH):
        j = layer // 2
        even = layer % 2 == 0
        if even:
            w_in = _pad_cols(ev_w_in[j][:, even_perm], EVEN_IN_PAD).astype(BF16)
            w_out = ev_w_out[j].astype(BF16)
            w_outs = [w_out[:EV_A], w_out[EV_A:]]
        else:
            w_in = _pad_cols(od_w_in[j], ODD_IN_PAD).astype(BF16)
            w_outs = [od_w_out[j].astype(BF16)]
            s5_prep = _s5_prepare(s5_a_re[j], s5_a_im[j], s5_log_dt[j], s5_b_re[j], s5_b_im[j], s5_c_re[j],
                                  s5_c_im[j])
        w_router = _pad_cols(moe_router[layer], LANES)
        routed = {}
        for grp in (CTX, DEC):
            x = xs[grp]
            ctx = grp.is_context
            proj = _in_projection(grp, layer, x, mods, norm_mix[layer][None], w_in)
            if even:
                out_a, fin_a = _gdn_mixer(grp, proj, ev_conv[j], gdn_a_log[j], gdn_dt_bias[j], gdn_gain[j],
                                          None if ctx else state_gdn[:, j])
                out_b, fin_b = _ret_mixer(grp, proj, ret_log_decay[j], ret_gain[j],
                                          None if ctx else state_ret[:, j], None if ctx else rope_tables)
                mixes = [out_a, out_b]
                if ctx:
                    new_states["gdn"].append(fin_a)
                    new_states["ret"].append(fin_b)
            else:
                ypre, fin_c = _s5_mixer(grp, proj, s5_prep, None if ctx else _s5_state_to_blocks(state_s5[:, j]))
                mix, fin_d = _gla_mixer(grp, proj, ypre, s5_d[j], s5_w_glu[j], gla_w_gate2[j], gla_b_gate[j],
                                        gla_gain[j], None if ctx else state_gla[:, j])
                mixes = [mix]
                if ctx:
                    new_states["s5"].append(_s5_state_from_blocks(fin_c))
                    new_states["gla"].append(fin_d)
            x1, h2, aff = _out_projection(grp, layer, mixes, x, mods, w_outs, norm_ffn[layer][None], w_router)
            xs_g, pt, gate = _route(grp, aff, h2)
            routed[grp] = (x1, xs_g, pt, gate)
        ys_c, ys_d = _experts(routed[CTX][1], routed[DEC][1], routed[CTX][3], routed[DEC][3],
                              moe_w_gate[layer], moe_w_up[layer], moe_w_down[layer])
        for grp, ys in ((CTX, ys_c), (DEC, ys_d)):
            x1, _, pt, _ = routed[grp]
            xs[grp] = _combine(grp, layer, layer == DEPTH - 1, x1, mods, pt, ys, norm_final[None])

    y_prompt = xs[CTX].reshape(BATCH, SEQ, D_MODEL)
    y_sample = xs[DEC].reshape(DEC_BATCH, DEC_SEQ, D_MODEL)
    return (y_prompt, y_sample, jnp.stack(new_states["gdn"], axis=1), jnp.stack(new_states["ret"], axis=1),
            jnp.stack(new_states["s5"], axis=1), jnp.stack(new_states["gla"], axis=1))
```

```python
import functools
import math
from typing import NamedTuple

import numpy as np
import jax
import jax.numpy as jnp
from jax import lax
from jax.experimental import pallas as pl
from jax.experimental.pallas import tpu as pltpu

F32 = jnp.float32
BF16 = jnp.bfloat16

D_MODEL = 1024
BATCH = 32
SEQ = 256
DEPTH = 4
DEC_BATCH = 2
DEC_SEQ = 1024
GRID_W = 64
N_DIR = 2
A_HEADS, A_DK, A_DV, A_CONV, A_CHUNK = 4, 128, 128, 5, 64
B_HEADS, B_DK, B_DV, B_CHUNK = 8, 64, 64, 64
ROPE_FREQS = B_DK // 4
ROPE_BASE = 10000.0
C_GROUP, C_GROUPS, C_STATE = 16, 32, 64
C_WIDTH = C_GROUPS * C_GROUP
S5_MAX_RE = -1e-4
D_HEADS, D_DK, D_DV, D_RANK, D_TAU, D_CHUNK = 4, 64, 128, 16, 16.0, 16
N_EXPERTS = 16
EXPERT_FF = 1024
EC_CAPACITY_FACTOR = 2
RMS_EPS = 1e-6

EVEN_SPLITS = (A_HEADS * A_DK, A_HEADS * A_DK, A_HEADS * A_DV, A_HEADS * A_DV, N_DIR * A_HEADS, N_DIR * A_HEADS,
               B_HEADS * B_DK, B_HEADS * B_DK, B_HEADS * B_DV, B_HEADS * B_DV)
ODD_SPLITS = (C_WIDTH, D_HEADS * D_DK, D_HEADS * D_DK, D_HEADS * D_DV, D_HEADS * D_DV, N_DIR * D_RANK)

LANES = 128
TOKEN_TILE = 256
N_COND = 1 + DEC_BATCH
COND_ROWS = 8
MOD_COL_TILE = 1536
VMEM_LIMIT = 56 * 1024 * 1024


class Group(NamedTuple):
    n_seq: int
    seq_len: int
    is_context: bool

    @property
    def tiles_per_seq(self):
        return self.seq_len // TOKEN_TILE

    @property
    def n_tiles(self):
        return self.n_seq * self.tiles_per_seq

    @property
    def capacity(self):
        return EC_CAPACITY_FACTOR * self.seq_len // N_EXPERTS

    def cond_of_tile(self, i):
        return 0 if self.is_context else 1 + i // self.tiles_per_seq

    def cond_of_seq(self, b):
        return 0 if self.is_context else 1 + b


CTX = Group(BATCH, SEQ, True)
DEC = Group(DEC_BATCH, DEC_SEQ, False)


def _round_up(n, m):
    return (n + m - 1) // m * m


def _params(n_axes):
    return pltpu.CompilerParams(dimension_semantics=("arbitrary",) * n_axes, vmem_limit_bytes=VMEM_LIMIT)


def _mm3(a, b):
    a1 = a.astype(BF16)
    a2 = (a - a1.astype(F32)).astype(BF16)
    b1 = b.astype(BF16)
    b2 = (b - b1.astype(F32)).astype(BF16)
    dot = lambda x, y: jnp.dot(x, y, preferred_element_type=F32)
    return dot(a1, b1) + (dot(a1, b2) + dot(a2, b1))


def _mm1(a, b):
    return jnp.dot(a.astype(BF16), b.astype(BF16), preferred_element_type=F32)


def _mod_kernel(c_ref, w_ref, b_ref, o_ref):
    c = c_ref[...]
    s = c * jax.nn.sigmoid(c)
    o_ref[0] = jnp.dot(s.astype(BF16), w_ref[0].astype(BF16), preferred_element_type=F32) + b_ref[0]


def _modulation(cond, ada_w, ada_b):
    n_out = 6 * D_MODEL
    out = pl.pallas_call(
        _mod_kernel,
        grid=(DEPTH, n_out // MOD_COL_TILE),
        in_specs=[pl.BlockSpec((COND_ROWS, D_MODEL), lambda l, j: (0, 0)),
                  pl.BlockSpec((1, D_MODEL, MOD_COL_TILE), lambda l, j: (l, 0, j)),
                  pl.BlockSpec((1, 1, MOD_COL_TILE), lambda l, j: (l, 0, j))],
        out_specs=pl.BlockSpec((1, COND_ROWS, MOD_COL_TILE), lambda l, j: (l, 0, j)),
        out_shape=jax.ShapeDtypeStruct((DEPTH, COND_ROWS, n_out), F32),
        compiler_params=_params(2),
        name="adaln_modulation",
    )(cond, ada_w, ada_b.reshape(DEPTH, 1, n_out))
    out = out[:, :N_COND].reshape(DEPTH, N_COND, 6, D_MODEL)
    return out.transpose(1, 0, 2, 3).reshape(N_COND, DEPTH * 6, D_MODEL)


def _norm_mod(x, gain, shift, scale):
    y = x * lax.rsqrt(jnp.mean(x * x, axis=-1, keepdims=True) + RMS_EPS)
    return (y * gain) * (1.0 + scale) + shift


def _mod_row(mod_ref, layer, k):
    r = 6 * layer + k
    return mod_ref[r:r + 1, :]


def _inproj_kernel(layer, x_ref, mod_ref, gain_ref, w_ref, o_ref):
    h = _norm_mod(x_ref[...], gain_ref[...], _mod_row(mod_ref, layer, 0), _mod_row(mod_ref, layer, 1))
    o_ref[...] = jnp.dot(h.astype(BF16), w_ref[...], preferred_element_type=F32)


def _in_projection(grp, layer, x, mods, gain, w):
    n_tok = grp.n_seq * grp.seq_len
    n_out = w.shape[1]
    return pl.pallas_call(
        functools.partial(_inproj_kernel, layer),
        grid=(grp.n_tiles,),
        in_specs=[pl.BlockSpec((TOKEN_TILE, D_MODEL), lambda i: (i, 0)),
                  pl.BlockSpec((None, DEPTH * 6, D_MODEL), lambda i: (grp.cond_of_tile(i), 0, 0)),
                  pl.BlockSpec((1, D_MODEL), lambda i: (0, 0)),
                  pl.BlockSpec((D_MODEL, n_out), lambda i: (0, 0))],
        out_specs=pl.BlockSpec((TOKEN_TILE, n_out), lambda i: (i, 0)),
        out_shape=jax.ShapeDtypeStruct((n_tok, n_out), F32),
        compiler_params=_params(1),
        name="in_projection",
    )(x, mods, gain, w)


def _outproj_kernel(layer, n_parts, *refs):
    mix_refs, wout_refs = refs[:n_parts], refs[n_parts:2 * n_parts]
    x_ref, mod_ref, gain_ref, wr_ref, x1_ref, h2_ref, aff_ref = refs[2 * n_parts:]
    y = jnp.dot(mix_refs[0][...].astype(BF16), wout_refs[0][...], preferred_element_type=F32)
    for m_ref, w_ref in zip(mix_refs[1:], wout_refs[1:]):
        y = y + jnp.dot(m_ref[...].astype(BF16), w_ref[...], preferred_element_type=F32)
    x1 = x_ref[...] + _mod_row(mod_ref, layer, 2) * y
    x1_ref[...] = x1
    h2 = _norm_mod(x1, gain_ref[...], _mod_row(mod_ref, layer, 3), _mod_row(mod_ref, layer, 4))
    h2_ref[...] = h2.astype(BF16)
    logits = _mm3(h2, wr_ref[...])
    lane = lax.broadcasted_iota(jnp.int32, logits.shape, 1)
    logits = jnp.where(lane < N_EXPERTS, logits, -jnp.inf)
    e = jnp.exp(logits - jnp.max(logits, axis=-1, keepdims=True))
    aff_ref[...] = e / jnp.sum(e, axis=-1, keepdims=True)


def _out_projection(grp, layer, mixes, x, mods, w_outs, gain, w_router):
    n_tok = grp.n_seq * grp.seq_len
    tile = lambda width: pl.BlockSpec((TOKEN_TILE, width), lambda i: (i, 0))
    return pl.pallas_call(
        functools.partial(_outproj_kernel, layer, len(mixes)),
        grid=(grp.n_tiles,),
        in_specs=[tile(m.shape[1]) for m in mixes]
                 + [pl.BlockSpec(w.shape, lambda i: (0, 0)) for w in w_outs]
                 + [tile(D_MODEL),
                    pl.BlockSpec((None, DEPTH * 6, D_MODEL), lambda i: (grp.cond_of_tile(i), 0, 0)),
                    pl.BlockSpec((1, D_MODEL), lambda i: (0, 0)),
                    pl.BlockSpec((D_MODEL, LANES), lambda i: (0, 0))],
        out_specs=[tile(D_MODEL), tile(D_MODEL), tile(LANES)],
        out_shape=[jax.ShapeDtypeStruct((n_tok, D_MODEL), F32),
                   jax.ShapeDtypeStruct((n_tok, D_MODEL), BF16),
                   jax.ShapeDtypeStruct((n_tok, LANES), F32)],
        compiler_params=_params(1),
        name="out_projection_router",
    )(*mixes, *w_outs, x, mods, gain, w_router)


def _route_kernel(n, cap, aff_ref, h2_ref, xs_ref, pt_ref, gate_ref):
    aff = aff_ref[...]
    aff_t = aff.T
    t_sub = lax.broadcasted_iota(jnp.int32, (n, n), 0)
    t_lane = lax.broadcasted_iota(jnp.int32, (n, n), 1)
    earlier = t_sub < t_lane
    sel_rows = []
    for e in range(N_EXPERTS):
        col = aff[:, e:e + 1]
        row = aff_t[e:e + 1, :]
        beats = (col > row) | ((col == row) & earlier)
        rank = jnp.sum(jnp.where(beats, 1.0, 0.0), axis=0, keepdims=True)
        sel_rows.append(jnp.where(rank < cap, 1.0, 0.0))
    sel = jnp.concatenate(sel_rows, axis=0)
    pos = jnp.dot(sel.astype(BF16), jnp.where(earlier, 1.0, 0.0).astype(BF16), preferred_element_type=F32)
    e_idx = lax.broadcasted_iota(jnp.int32, (N_EXPERTS, n), 0)
    slot = jnp.where(sel > 0.0, pos.astype(jnp.int32) + e_idx * cap, -1)
    slot_pad = jnp.concatenate([slot, jnp.full((LANES - N_EXPERTS, n), -1, jnp.int32)], axis=0)
    slot_t = slot_pad.astype(F32).T.astype(jnp.int32)
    s_lane = lax.broadcasted_iota(jnp.int32, (n, N_EXPERTS * cap), 1)
    pt = jnp.zeros((n, N_EXPERTS * cap), F32)
    c_sub = lax.broadcasted_iota(jnp.int32, (cap, n), 0)
    h2 = h2_ref[...]
    for e in range(N_EXPERTS):
        pt = pt + jnp.where(slot_t[:, e:e + 1] == s_lane, 1.0, 0.0)
        p_e = jnp.where(slot[e:e + 1, :] == c_sub + e * cap, 1.0, 0.0)
        xs_ref[e] = jnp.dot(p_e.astype(BF16), h2, preferred_element_type=F32).astype(BF16)
        gate = jnp.sum(p_e * aff_t[e:e + 1, :], axis=1, keepdims=True)
        gate_ref[e] = jnp.broadcast_to(gate, (cap, LANES))
    pt_ref[...] = pt.astype(BF16)


def _route(grp, aff, h2):
    n, cap = grp.seq_len, grp.capacity
    slots = N_EXPERTS * cap
    return pl.pallas_call(
        functools.partial(_route_kernel, n, cap),
        grid=(grp.n_seq,),
        in_specs=[pl.BlockSpec((n, LANES), lambda b: (b, 0)),
                  pl.BlockSpec((n, D_MODEL), lambda b: (b, 0))],
        out_specs=[pl.BlockSpec((N_EXPERTS, cap, D_MODEL), lambda b: (0, b, 0)),
                   pl.BlockSpec((n, slots), lambda b: (b, 0)),
                   pl.BlockSpec((N_EXPERTS, cap, LANES), lambda b: (0, b, 0))],
        out_shape=[jax.ShapeDtypeStruct((N_EXPERTS, grp.n_seq * cap, D_MODEL), BF16),
                   jax.ShapeDtypeStruct((grp.n_seq * n, slots), BF16),
                   jax.ShapeDtypeStruct((N_EXPERTS, grp.n_seq * cap, LANES), F32)],
        compiler_params=_params(1),
        name="expert_choice_route",
    )(aff, h2)


FF_TILE = 512


def _expert_kernel(xc_ref, xd_ref, gc_ref, gd_ref, wg_ref, wu_ref, wd_ref, yc_ref, yd_ref, accc_ref, accd_ref):
    f = pl.program_id(1)
    wg = wg_ref[...].astype(BF16)
    wu = wu_ref[...].astype(BF16)
    wd = wd_ref[...].astype(BF16)

    def ffn(x_ref, acc_ref):
        x = x_ref[...]
        a = jnp.dot(x, wg, preferred_element_type=F32)
        u = jnp.dot(x, wu, preferred_element_type=F32)
        hid = (a * jax.nn.sigmoid(a)) * u
        y = jnp.dot(hid.astype(BF16), wd, preferred_element_type=F32)

        @pl.when(f == 0)
        def _():
            acc_ref[...] = y

        @pl.when(f != 0)
        def _():
            acc_ref[...] += y

    ffn(xc_ref, accc_ref)
    ffn(xd_ref, accd_ref)

    @pl.when(f == pl.num_programs(1) - 1)
    def _():
        yc_ref[...] = (accc_ref[...] * gc_ref[:, 0:1]).astype(BF16)
        yd_ref[...] = (accd_ref[...] * gd_ref[:, 0:1]).astype(BF16)


def _experts(layer, xs_c, xs_d, gate_c, gate_d, w_gate, w_up, w_down):
    rc, rd = xs_c.shape[1], xs_d.shape[1]
    per_e = lambda rows, width: pl.BlockSpec((None, rows, width), lambda e, f: (e, 0, 0))
    return pl.pallas_call(
        _expert_kernel,
        grid=(N_EXPERTS, EXPERT_FF // FF_TILE),
        in_specs=[per_e(rc, D_MODEL), per_e(rd, D_MODEL), per_e(rc, LANES), per_e(rd, LANES),
                  pl.BlockSpec((None, None, D_MODEL, FF_TILE), lambda e, f: (layer, e, 0, f)),
                  pl.BlockSpec((None, None, D_MODEL, FF_TILE), lambda e, f: (layer, e, 0, f)),
                  pl.BlockSpec((None, None, FF_TILE, D_MODEL), lambda e, f: (layer, e, f, 0))],
        out_specs=[per_e(rc, D_MODEL), per_e(rd, D_MODEL)],
        out_shape=[jax.ShapeDtypeStruct(xs_c.shape, BF16), jax.ShapeDtypeStruct(xs_d.shape, BF16)],
        scratch_shapes=[pltpu.VMEM((rc, D_MODEL), F32), pltpu.VMEM((rd, D_MODEL), F32)],
        compiler_params=_params(2),
        name="expert_swiglu",
    )(xs_c, xs_d, gate_c, gate_d, w_gate, w_up, w_down)


def _combine_kernel(layer, cap, final, x1_ref, mod_ref, pt_ref, ys_ref, gain_ref, x2_ref):
    ys = ys_ref[...].reshape(N_EXPERTS * cap, D_MODEL)
    y = jnp.dot(pt_ref[...], ys, preferred_element_type=F32)
    x2 = x1_ref[...] + _mod_row(mod_ref, layer, 5) * y
    if final:
        x2 = x2 * lax.rsqrt(jnp.mean(x2 * x2, axis=-1, keepdims=True) + RMS_EPS) * gain_ref[...]
    x2_ref[...] = x2


def _combine(grp, layer, final, x1, mods, pt, ys, gain_final):
    n, cap = grp.seq_len, grp.capacity
    return pl.pallas_call(
        functools.partial(_combine_kernel, layer, cap, final),
        grid=(grp.n_seq,),
        in_specs=[pl.BlockSpec((n, D_MODEL), lambda b: (b, 0)),
                  pl.BlockSpec((None, DEPTH * 6, D_MODEL), lambda b: (grp.cond_of_seq(b), 0, 0)),
                  pl.BlockSpec((n, N_EXPERTS * cap), lambda b: (b, 0)),
                  pl.BlockSpec((N_EXPERTS, cap, D_MODEL), lambda b: (0, b, 0)),
                  pl.BlockSpec((1, D_MODEL), lambda b: (0, 0))],
        out_specs=pl.BlockSpec((n, D_MODEL), lambda b: (b, 0)),
        out_shape=jax.ShapeDtypeStruct(x1.shape, F32),
        compiler_params=_params(1),
        name="expert_combine",
    )(x1, mods, pt, ys, gain_final)


S5_BLOCKS = 2
S5_BLOCK_IN = C_WIDTH // S5_BLOCKS
S5_BLOCK_STATE = C_GROUPS * C_STATE // S5_BLOCKS
S5_SUBLANES = 8
S5_SCAN_COLS = 256
S5_SHIFTS = (1, 2, 4)


def _cmul(ar, ai, br, bi):
    return ar * br - ai * bi, ar * bi + ai * br


def _s5_prep_kernel(are_ref, aim_ref, ldt_ref, bre_ref, bim_ref, cre_ref, cim_ref, bmat_ref, cmat_ref, const_ref):
    n = S5_BLOCK_STATE
    row = lax.broadcasted_iota(jnp.int32, (S5_SUBLANES, n), 0)
    for d in range(N_DIR):
        a_re = jnp.minimum(are_ref[d], S5_MAX_RE)
        a_im = aim_ref[d]
        dt = jnp.exp(ldt_ref[d])
        mag = jnp.exp(a_re * dt)
        l_re = mag * jnp.cos(a_im * dt)
        l_im = mag * jnp.sin(a_im * dt)
        den = a_re * a_re + a_im * a_im
        k_re = ((l_re - 1.0) * a_re + l_im * a_im) / den
        k_im = (l_im * a_re - (l_re - 1.0) * a_im) / den
        b_re, b_im = bre_ref[...], bim_ref[...]
        bb_re, bb_im = _cmul(k_re, k_im, b_re, b_im)
        bmat_ref[d] = jnp.concatenate([bb_re, bb_im], axis=1).astype(BF16)
        pows = [(l_re, l_im)]
        for _ in range(S5_SUBLANES - 1):
            pows.append(_cmul(pows[-1][0], pows[-1][1], l_re, l_im))
        for i, s in enumerate(S5_SHIFTS):
            keep = (row >= s) if d == 0 else (row <= S5_SUBLANES - 1 - s)
            const_ref[d, 2 * i] = jnp.where(keep, pows[s - 1][0], 0.0)
            const_ref[d, 2 * i + 1] = jnp.where(keep, pows[s - 1][1], 0.0)
        lp_re = jnp.zeros((S5_SUBLANES, n), F32)
        lp_im = jnp.zeros((S5_SUBLANES, n), F32)
        for r in range(S5_SUBLANES):
            p = pows[r] if d == 0 else pows[S5_SUBLANES - 1 - r]
            lp_re = jnp.where(row == r, p[0], lp_re)
            lp_im = jnp.where(row == r, p[1], lp_im)
        const_ref[d, 6] = lp_re
        const_ref[d, 7] = lp_im
    cmat_ref[...] = jnp.concatenate([cre_ref[...], -cim_ref[...]], axis=0).astype(BF16)


def _s5_prepare(a_re, a_im, log_dt, b_re, b_im, c_re, c_im):
    n, k_in = S5_BLOCK_STATE, S5_BLOCK_IN
    gpb = C_GROUPS // S5_BLOCKS
    per_state = lambda t: t.reshape(N_DIR, S5_BLOCKS, 1, n)
    ldt = jnp.repeat(log_dt, C_STATE, axis=-1)
    eye = jnp.eye(gpb, dtype=F32)

    def expand_b(b):
        b = b.reshape(S5_BLOCKS, gpb, C_STATE, C_GROUP).transpose(0, 1, 3, 2)
        return (b[:, :, :, None, :] * eye[None, :, None, :, None]).reshape(S5_BLOCKS, k_in, n)

    def expand_c(c):
        c = c.reshape(S5_BLOCKS, gpb, C_GROUP, C_STATE).transpose(0, 1, 3, 2)
        return (c[:, :, :, None, :] * eye[None, :, None, :, None]).reshape(S5_BLOCKS, n, k_in)

    row_spec = pl.BlockSpec((N_DIR, None, 1, n), lambda k: (0, k, 0, 0))
    return pl.pallas_call(
        _s5_prep_kernel,
        grid=(S5_BLOCKS,),
        in_specs=[row_spec, row_spec, row_spec,
                  pl.BlockSpec((None, k_in, n), lambda k: (k, 0, 0)),
                  pl.BlockSpec((None, k_in, n), lambda k: (k, 0, 0)),
                  pl.BlockSpec((None, n, k_in), lambda k: (k, 0, 0)),
                  pl.BlockSpec((None, n, k_in), lambda k: (k, 0, 0))],
        out_specs=[pl.BlockSpec((N_DIR, None, k_in, 2 * n), lambda k: (0, k, 0, 0)),
                   pl.BlockSpec((None, 2 * n, k_in), lambda k: (k, 0, 0)),
                   pl.BlockSpec((N_DIR, None, 8, S5_SUBLANES, n), lambda k: (0, k, 0, 0, 0))],
        out_shape=[jax.ShapeDtypeStruct((N_DIR, S5_BLOCKS, k_in, 2 * n), BF16),
                   jax.ShapeDtypeStruct((S5_BLOCKS, 2 * n, k_in), BF16),
                   jax.ShapeDtypeStruct((N_DIR, S5_BLOCKS, 8, S5_SUBLANES, n), F32)],
        compiler_params=_params(1),
        name="s5_prepare",
    )(per_state(a_re), per_state(a_im), per_state(ldt), expand_b(b_re), expand_b(b_im), expand_c(c_re),
      expand_c(c_im))


def _s5_kernel(seq_len, has_state, u_ref, bmat_ref, cmat_ref, const_ref, *rest):
    if has_state:
        s0_ref, y_ref, fin_ref, xre_ref, xim_ref = rest
    else:
        y_ref, fin_ref, xre_ref, xim_ref = rest
    n = S5_BLOCK_STATE
    n_tiles = seq_len // S5_SUBLANES
    u = u_ref[...].astype(BF16)
    for d in range(N_DIR):
        bu = jnp.dot(u, bmat_ref[d], preferred_element_type=F32)
        xre_ref[d] = bu[:, :n]
        xim_ref[d] = bu[:, n:]

    for cb in range(n // S5_SCAN_COLS):
        cols = slice(cb * S5_SCAN_COLS, (cb + 1) * S5_SCAN_COLS)

        def scan_tile(d, i, carry):
            rows = pl.ds(pl.multiple_of(i * S5_SUBLANES, S5_SUBLANES), S5_SUBLANES)
            xr = xre_ref[d, rows, cols]
            xi = xim_ref[d, rows, cols]
            for k, s in enumerate(S5_SHIFTS):
                shift = s if d == 0 else S5_SUBLANES - s
                pr, pi = _cmul(const_ref[d, 2 * k, :, cols], const_ref[d, 2 * k + 1, :, cols],
                               pltpu.roll(xr, shift, 0), pltpu.roll(xi, shift, 0))
                xr, xi = xr + pr, xi + pi
            cr, ci = _cmul(const_ref[d, 6, :, cols], const_ref[d, 7, :, cols], carry[0], carry[1])
            xr, xi = xr + cr, xi + ci
            xre_ref[d, rows, cols] = xr
            xim_ref[d, rows, cols] = xi
            edge = S5_SUBLANES - 1 if d == 0 else 0
            shape = (S5_SUBLANES, S5_SCAN_COLS)
            return (jnp.broadcast_to(xr[edge:edge + 1], shape), jnp.broadcast_to(xi[edge:edge + 1], shape))

        def body(i, carry):
            return (scan_tile(0, i, carry[0]), scan_tile(1, n_tiles - 1 - i, carry[1]))

        shape = (S5_SUBLANES, S5_SCAN_COLS)
        if has_state:
            init = tuple((jnp.broadcast_to(s0_ref[2 * d:2 * d + 1, cols], shape),
                          jnp.broadcast_to(s0_ref[2 * d + 1:2 * d + 2, cols], shape)) for d in range(N_DIR))
        else:
            init = tuple((jnp.zeros(shape, F32), jnp.zeros(shape, F32)) for _ in range(N_DIR))
        fin = lax.fori_loop(0, n_tiles, body, init)
        for d in range(N_DIR):
            fin_ref[2 * d:2 * d + 1, cols] = fin[d][0][0:1]
            fin_ref[2 * d + 1:2 * d + 2, cols] = fin[d][1][0:1]

    x = jnp.concatenate([xre_ref[0] + xre_ref[1], xim_ref[0] + xim_ref[1]], axis=1).astype(BF16)
    y_ref[...] = jnp.dot(x, cmat_ref[...], preferred_element_type=F32)


def _s5_mixer(grp, proj, prep, s0):
    bmat, cmat, consts = prep
    n, k_in, seq_len = S5_BLOCK_STATE, S5_BLOCK_IN, grp.seq_len
    has_state = s0 is not None
    state_spec = pl.BlockSpec((None, None, 2 * N_DIR, n), lambda b, k: (b, k, 0, 0))
    in_specs = [pl.BlockSpec((seq_len, k_in), lambda b, k: (b, k)),
                pl.BlockSpec((N_DIR, None, k_in, 2 * n), lambda b, k: (0, k, 0, 0)),
                pl.BlockSpec((None, 2 * n, k_in), lambda b, k: (k, 0, 0)),
                pl.BlockSpec((N_DIR, None, 8, S5_SUBLANES, n), lambda b, k: (0, k, 0, 0, 0))]
    args = [proj, bmat, cmat, consts]
    if has_state:
        in_specs.append(state_spec)
        args.append(s0)
    return pl.pallas_call(
        functools.partial(_s5_kernel, seq_len, has_state),
        grid=(grp.n_seq, S5_BLOCKS),
        in_specs=in_specs,
        out_specs=[pl.BlockSpec((seq_len, k_in), lambda b, k: (b, k)), state_spec],
        out_shape=[jax.ShapeDtypeStruct((grp.n_seq * seq_len, C_WIDTH), F32),
                   jax.ShapeDtypeStruct((grp.n_seq, S5_BLOCKS, 2 * N_DIR, n), F32)],
        scratch_shapes=[pltpu.VMEM((N_DIR, seq_len, n), F32), pltpu.VMEM((N_DIR, seq_len, n), F32)],
        compiler_params=_params(2),
        name="s5_scan",
    )(*args)


def _s5_state_to_blocks(s):
    b = s.shape[0]
    s = s.reshape(b, N_DIR, S5_BLOCKS, S5_BLOCK_STATE, 2).transpose(0, 2, 1, 4, 3)
    return s.reshape(b, S5_BLOCKS, 2 * N_DIR, S5_BLOCK_STATE)


def _s5_state_from_blocks(s):
    b = s.shape[0]
    s = s.reshape(b, S5_BLOCKS, N_DIR, 2, S5_BLOCK_STATE).transpose(0, 2, 1, 4, 3)
    return s.reshape(b, N_DIR, C_GROUPS, C_STATE, 2)


ODD_IN_PAD = _round_up(sum(ODD_SPLITS), LANES)
_O_U, _O_Q, _O_K, _O_V, _O_G, _O_A = (int(v) for v in np.cumsum((0,) + ODD_SPLITS[:-1]))
GLA_QK = D_HEADS * D_DK
GLA_V = D_HEADS * D_DV
GLA_MIN_SUPER = 256
GLA_REF_ROWS = 128


def _split3(x):
    x1 = x.astype(BF16)
    r1 = x - x1.astype(F32)
    x2 = r1.astype(BF16)
    x3 = (r1 - x2.astype(F32)).astype(BF16)
    return x1, x2, x3


def _dot_01(m01, x):
    p1, p2, p3 = _split3(x)
    dot = lambda p: jnp.dot(m01, p, preferred_element_type=F32)
    return dot(p1) + dot(p2) + dot(p3)


def _dot_nt(a, b):
    return lax.dot_general(a, b, (((1,), (1,)), ((), ())), preferred_element_type=F32)


def _one_hot(cond):
    return jnp.where(cond, 1.0, 0.0).astype(BF16)


def _gla_kernel(seq_len, has_state, heads, u_ref, q_ref, k_ref, v_ref, g_ref, a_ref, ypre_ref, dskip_ref, wglu_ref,
                wg2_ref, bgate_ref, gain_ref, *rest):
    if has_state:
        s0_ref, outc_ref, outd_ref, p_s = rest
    else:
        outc_ref, outd_ref, fin_ref, p_s = rest
    n = seq_len

    @pl.when(pl.program_id(1) == 0)
    def _():
        y = jax.nn.gelu(ypre_ref[...] + dskip_ref[...] * u_ref[...])
        glu = jnp.dot(y.astype(BF16), wglu_ref[...], preferred_element_type=F32)
        outc_ref[...] = y * jax.nn.sigmoid(glu)

    q = q_ref[...] * (D_DK ** -0.5)
    k = k_ref[...]
    r_i = lax.broadcasted_iota(jnp.int32, (n, n), 0)
    c_i = lax.broadcasted_iota(jnp.int32, (n, n), 1)
    b = []
    for d in range(N_DIR):
        logits = jnp.dot(a_ref[...], wg2_ref[d], preferred_element_type=F32,
                         precision=lax.Precision.HIGHEST) + bgate_ref[d]
        la = (jnp.minimum(logits, 0.0) - jnp.log(1.0 + jnp.exp(-jnp.abs(logits)))) / D_TAU
        b.append(_dot_01(_one_hot((r_i >= c_i) if d == 0 else (r_i <= c_i)), la))

    lane_p = lax.broadcasted_iota(jnp.int32, (1, LANES), 1)

    def pair_lanes(x, h):
        return x[:, (h // 2) * LANES:(h // 2 + 1) * LANES]

    def head_only(x, h):
        mine = (lane_p < D_DK) if h % 2 == 0 else (lane_p >= D_DK)
        return jnp.where(mine, pair_lanes(x, h), 0.0).astype(BF16)

    for h in range(heads):
        s = _dot_nt(head_only(q, h), pair_lanes(k, h).astype(BF16))
        p_s[h] = jnp.where(r_i == c_i, 2.0 * s, 0.0)

    for lb in range(int(math.log2(n))):
        half = 1 << lb
        width = min(max(2 * half, GLA_MIN_SUPER), n)
        w_r = lax.broadcasted_iota(jnp.int32, (width, width), 0)
        w_c = lax.broadcasted_iota(jnp.int32, (width, width), 1)
        at_level = ((w_r ^ w_c) >> lb) == 1
        later = w_r > w_c
        qt, kt = [], []
        for d in range(N_DIR):
            edge = half - 1 if d == 0 else half
            if n <= GLA_MIN_SUPER:
                sel = c_i == ((r_i >> (lb + 1)) << (lb + 1)) + edge
                ref = _dot_01(_one_hot(sel), b[d])
            else:
                n_ref = max(n >> (lb + 1), GLA_REF_ROWS)
                blk = lax.broadcasted_iota(jnp.int32, (n_ref, n), 0)
                tok = lax.broadcasted_iota(jnp.int32, (n_ref, n), 1)
                per_block = _dot_01(_one_hot(tok == (blk << (lb + 1)) + edge), b[d])
                tok_r = lax.broadcasted_iota(jnp.int32, (n, n_ref), 0)
                blk_c = lax.broadcasted_iota(jnp.int32, (n, n_ref), 1)
                ref = _dot_01(_one_hot(blk_c == (tok_r >> (lb + 1))), per_block)
            qt.append(q * jnp.exp(jnp.minimum(b[d] - ref, 0.0)))
            kt.append(k * jnp.exp(jnp.minimum(ref - b[d], 0.0)))
        for sb in range(n // width):
            rows = slice(sb * width, (sb + 1) * width)
            for h in range(heads):
                s_f = _dot_nt(head_only(qt[0][rows], h), pair_lanes(kt[0][rows], h).astype(BF16))
                s_b = _dot_nt(head_only(qt[1][rows], h), pair_lanes(kt[1][rows], h).astype(BF16))
                p_s[h, rows, rows] = p_s[h, rows, rows] + jnp.where(at_level, jnp.where(later, s_f, s_b), 0.0)

    zeros = jnp.zeros((D_DK, D_DV), F32)
    for h in range(heads):
        vh = v_ref[:, h * D_DV:(h + 1) * D_DV].astype(BF16)
        o = jnp.dot(p_s[h].astype(BF16), vh, preferred_element_type=F32)
        if has_state:
            for d in range(N_DIR):
                qs = pair_lanes(q * jnp.exp(b[d]), h).astype(BF16)
                s0 = s0_ref[d, h]
                s0_pad = jnp.concatenate([s0, zeros] if h % 2 == 0 else [zeros, s0], axis=0).astype(BF16)
                o = o + jnp.dot(qs, s0_pad, preferred_element_type=F32)
        o = o * lax.rsqrt(jnp.mean(o * o, axis=-1, keepdims=True) + RMS_EPS) * gain_ref[...]
        gate = g_ref[:, h * D_DV:(h + 1) * D_DV]
        outd_ref[:, h * D_DV:(h + 1) * D_DV] = o * (gate * jax.nn.sigmoid(gate))

    if not has_state:
        for d in range(N_DIR):
            last = n - 1 if d == 0 else 0
            k_out_t = (k * jnp.exp(b[d][last:last + 1] - b[d])).T.astype(BF16)
            for h in range(heads):
                vh = v_ref[:, h * D_DV:(h + 1) * D_DV].astype(BF16)
                fin_ref[d, h] = jnp.dot(k_out_t[h * D_DK:(h + 1) * D_DK], vh, preferred_element_type=F32)


def _gla_mixer(grp, proj, ypre, d_skip, w_glu, w_gate2, b_gate, gain, s0):
    seq_len = grp.seq_len
    has_state = s0 is not None
    heads = D_HEADS if seq_len <= GLA_MIN_SUPER else 2
    groups = D_HEADS // heads
    n_qk, n_v = heads * D_DK, heads * D_DV
    wg2 = jnp.zeros((N_DIR, LANES, GLA_QK), F32)
    for d in range(N_DIR):
        wg2 = wg2.at[d, d * D_RANK:(d + 1) * D_RANK].set(w_gate2[d])
    full = lambda *shape: pl.BlockSpec(shape, lambda b, g: (0,) * len(shape))
    cols = lambda width, offset: pl.BlockSpec((seq_len, width), lambda b, g: (b, offset // width + g))
    fixed = lambda width, offset: pl.BlockSpec((seq_len, width), lambda b, g: (b, offset // width))
    state_spec = pl.BlockSpec((None, N_DIR, heads, D_DK, D_DV), lambda b, g: (b, 0, g, 0, 0))
    in_specs = [fixed(C_WIDTH, _O_U), cols(n_qk, _O_Q), cols(n_qk, _O_K), cols(n_v, _O_V), cols(n_v, _O_G),
                fixed(LANES, _O_A), fixed(C_WIDTH, 0),
                full(1, C_WIDTH), full(C_WIDTH, C_WIDTH),
                pl.BlockSpec((N_DIR, LANES, n_qk), lambda b, g: (0, 0, g)),
                pl.BlockSpec((N_DIR, 1, n_qk), lambda b, g: (0, 0, g)),
                full(1, D_DV)]
    args = [proj] * 6 + [ypre, d_skip[None], w_glu.astype(BF16), wg2, b_gate.reshape(N_DIR, 1, GLA_QK), gain[None]]
    n_tok = grp.n_seq * seq_len
    out_specs = [pl.BlockSpec((seq_len, C_WIDTH), lambda b, g: (b, 0)),
                 pl.BlockSpec((seq_len, n_v), lambda b, g: (b, g))]
    out_shape = [jax.ShapeDtypeStruct((n_tok, C_WIDTH), F32), jax.ShapeDtypeStruct((n_tok, GLA_V), F32)]
    if has_state:
        in_specs.append(state_spec)
        args.append(s0)
    else:
        out_specs.append(state_spec)
        out_shape.append(jax.ShapeDtypeStruct((grp.n_seq, N_DIR, D_HEADS, D_DK, D_DV), F32))
    out = pl.pallas_call(
        functools.partial(_gla_kernel, seq_len, has_state, heads),
        grid=(grp.n_seq, groups),
        in_specs=in_specs, out_specs=out_specs, out_shape=out_shape,
        scratch_shapes=[pltpu.VMEM((heads, seq_len, seq_len), F32)],
        compiler_params=_params(2),
        name="gla_mixer",
    )(*args)
    return (out[0], out[1], None) if has_state else tuple(out)


EV_A = A_HEADS * A_DK
EV_B = B_HEADS * B_DK
EV_HALF = 4 * EV_A
EVEN_IN_PAD = 2 * EV_HALF + LANES
GDN_PAIR = 2 * A_CHUNK
GDN_LEVELS = 5
GDN_SPLIT_LEVELS = 3


def _even_col_perm():
    off = np.cumsum((0,) + EVEN_SPLITS)
    seg = lambda i: np.arange(off[i], off[i + 1])
    order = [0, 1, 2, 3, 6, 7, 8, 9, 4, 5]
    return np.concatenate([seg(i) for i in order])


def _softplus(x):
    return jnp.maximum(x, 0.0) + jnp.log(1.0 + jnp.exp(-jnp.abs(x)))


def _silu(x):
    return x * jax.nn.sigmoid(x)


def _gdn_kernel(seq_len, has_state, proj_ref, small_ref, convw_ref, alog_ref, dtb_ref, gain_ref, *rest):
    if has_state:
        s0_ref, out_ref, q_s, k_s, v_s, gb_s, gcum_s, o_s, st_s = rest
    else:
        out_ref, fin_ref, q_s, k_s, v_s, gb_s, gcum_s, o_s, st_s = rest
    n_pairs = seq_len // GDN_PAIR
    width = 3 * EV_A

    x = proj_ref[:, 0:width]
    t_idx = lax.broadcasted_iota(jnp.int32, (seq_len, 1), 0)
    pad = A_CONV // 2
    acc = x * convw_ref[pad:pad + 1, :]
    for s in range(-pad, pad + 1):
        if s == 0:
            continue
        shifted = pltpu.roll(x, (-s) % seq_len, 0)
        inside = (t_idx + s >= 0) & (t_idx + s < seq_len)
        acc = acc + jnp.where(inside, shifted, 0.0) * convw_ref[pad + s:pad + s + 1, :]
    y = _silu(acc)
    for h in range(A_HEADS):
        cols = slice(h * A_DK, (h + 1) * A_DK)
        qh = y[:, h * A_DK:(h + 1) * A_DK]
        kh = y[:, EV_A + h * A_DK:EV_A + (h + 1) * A_DK]
        q_s[:, cols] = qh * lax.rsqrt(jnp.sum(qh * qh, axis=-1, keepdims=True) + 1e-6) * (A_DK ** -0.5)
        k_s[:, cols] = kh * lax.rsqrt(jnp.sum(kh * kh, axis=-1, keepdims=True) + 1e-6)
    v_s[...] = y[:, 2 * EV_A:3 * EV_A]

    small = small_ref[...]
    lane = lax.broadcasted_iota(jnp.int32, small.shape, 1)
    gb = jnp.where(lane < N_DIR * A_HEADS, jax.nn.sigmoid(small),
                   -jnp.exp(alog_ref[...]) * _softplus(small + dtb_ref[...]))
    gb_s[...] = gb
    r_i = lax.broadcasted_iota(jnp.int32, (seq_len, seq_len), 0)
    c_i = lax.broadcasted_iota(jnp.int32, (seq_len, seq_len), 1)
    same = (r_i // A_CHUNK) == (c_i // A_CHUNK)
    gcum_s[0] = _dot_01(jnp.where(same & (r_i >= c_i), 1.0, 0.0).astype(BF16), gb)
    gcum_s[1] = _dot_01(jnp.where(same & (r_i <= c_i), 1.0, 0.0).astype(BF16), gb)

    o_s[...] = jnp.zeros(o_s.shape, F32)
    for d in range(N_DIR):
        for h in range(A_HEADS):
            st_s[d, h] = s0_ref[d, h] if has_state else jnp.zeros((A_DK, A_DV), F32)

    pr = lax.broadcasted_iota(jnp.int32, (GDN_PAIR, GDN_PAIR), 0)
    pc = lax.broadcasted_iota(jnp.int32, (GDN_PAIR, GDN_PAIR), 1)
    p_same = (pr // A_CHUNK) == (pc // A_CHUNK)
    eye = jnp.where(pr == pc, 1.0, 0.0)
    row_p = lax.broadcasted_iota(jnp.int32, (GDN_PAIR, 1), 0)
    zeros_c = jnp.zeros((A_CHUNK, A_DV), F32)

    def pair(p, _):
        bodies = []
        for d in range(N_DIR):
            base = pl.multiple_of((p if d == 0 else n_pairs - 1 - p) * GDN_PAIR, GDN_PAIR)
            rows = pl.ds(base, GDN_PAIR)
            gbp = gb_s[rows, :]
            gp = gcum_s[d, rows, :]
            gp_t = gp.T
            tri = (pr >= pc) if d == 0 else (pr <= pc)
            strict = (pr > pc) if d == 0 else (pr < pc)
            for h in range(A_HEADS):
                cols = slice(h * A_DK, (h + 1) * A_DK)
                bi, gi = d * A_HEADS + h, N_DIR * A_HEADS + d * A_HEADS + h
                qp, kp, vp = q_s[rows, cols], k_s[rows, cols], v_s[rows, cols]
                beta = gbp[:, bi:bi + 1]
                g_col = gp[:, gi:gi + 1]
                g_row = gp_t[gi:gi + 1, :]
                decay = jnp.where(p_same & tri, jnp.exp(jnp.minimum(g_col - g_row, 0.0)), 0.0)
                kb = kp * beta
                kp16 = kp.astype(BF16)
                m = jnp.where(p_same & strict, -(_dot_nt(kb.astype(BF16), kp16) * decay), 0.0)
                e_g = jnp.exp(g_col)
                last = A_CHUNK - 1 if d == 0 else 0
                g_last = [g_col[c * A_CHUNK + last:c * A_CHUNK + last + 1] for c in range(2)]
                gl_col = jnp.where(row_p < A_CHUNK, g_last[0], g_last[1])
                bodies.append(dict(
                    d=d, h=h, base=base, cols=cols, m=m, g_last=g_last,
                    rhs=jnp.concatenate([vp * beta, kb * e_g], axis=1).astype(BF16),
                    attn=(_dot_nt(qp.astype(BF16), kp16) * decay).astype(BF16),
                    q_in=(qp * e_g).astype(BF16),
                    k_out_t=(kp * jnp.exp(gl_col - g_col)).T.astype(BF16)))

        ms = [body["m"] for body in bodies]
        t_invs = [eye + m for m in ms]
        for level in range(GDN_LEVELS):
            mm = _mm3 if level < GDN_SPLIT_LEVELS else _mm1
            ms = [mm(m, m) for m in ms]
            t_invs = [t + mm(t, m) for t, m in zip(t_invs, ms)]

        for body, t_inv in zip(bodies, t_invs):
            d, h, base, cols, g_last = body["d"], body["h"], body["base"], body["cols"], body["g_last"]
            uw = jnp.dot(t_inv.astype(BF16), body["rhs"], preferred_element_type=F32)
            u, w = uw[:, :A_DV], uw[:, A_DV:]
            s = st_s[d, h]
            for c in ((0, 1) if d == 0 else (1, 0)):
                rs = slice(c * A_CHUNK, (c + 1) * A_CHUNK)
                s16 = s.astype(BF16)
                v_new = u[rs] - jnp.dot(w[rs].astype(BF16), s16, preferred_element_type=F32)
                v_full = jnp.concatenate([v_new, zeros_c] if c == 0 else [zeros_c, v_new], axis=0).astype(BF16)
                o_c = (jnp.dot(body["q_in"][rs], s16, preferred_element_type=F32)
                       + jnp.dot(body["attn"][rs], v_full, preferred_element_type=F32))
                s = s * jnp.exp(g_last[c]) + jnp.dot(body["k_out_t"], v_full, preferred_element_type=F32)
                rows_c = pl.ds(base + c * A_CHUNK, A_CHUNK)
                o_s[rows_c, cols] = o_s[rows_c, cols] + o_c
            st_s[d, h] = s
        return 0

    lax.fori_loop(0, n_pairs, pair, 0)

    for h in range(A_HEADS):
        cols = slice(h * A_DV, (h + 1) * A_DV)
        o = o_s[:, cols]
        o = o * lax.rsqrt(jnp.mean(o * o, axis=-1, keepdims=True) + RMS_EPS) * gain_ref[...]
        out_ref[:, cols] = o * _silu(proj_ref[:, 3 * EV_A + h * A_DV:3 * EV_A + (h + 1) * A_DV])
    if not has_state:
        for d in range(N_DIR):
            for h in range(A_HEADS):
                fin_ref[d, h] = st_s[d, h]


def _gdn_mixer(grp, proj, conv_w, a_log, dt_bias, gain, s0):
    seq_len = grp.seq_len
    has_state = s0 is not None
    n_small = N_DIR * A_HEADS
    lane_row = lambda v: jnp.zeros((1, LANES), F32).at[0, n_small:2 * n_small].set(v.reshape(n_small))
    convw = jnp.zeros((8, 3 * EV_A), F32).at[:A_CONV].set(conv_w)
    full = lambda *shape: pl.BlockSpec(shape, lambda b: (0,) * len(shape))
    state_spec = pl.BlockSpec((None, N_DIR, A_HEADS, A_DK, A_DV), lambda b: (b, 0, 0, 0, 0))
    in_specs = [pl.BlockSpec((seq_len, EV_HALF), lambda b: (b, 0)),
                pl.BlockSpec((seq_len, LANES), lambda b: (b, 2 * EV_HALF // LANES)),
                full(8, 3 * EV_A), full(1, LANES), full(1, LANES), full(1, A_DV)]
    args = [proj, proj, convw, lane_row(a_log), lane_row(dt_bias), gain[None]]
    out_spec = pl.BlockSpec((seq_len, EV_A), lambda b: (b, 0))
    out_shape = jax.ShapeDtypeStruct((grp.n_seq * seq_len, EV_A), F32)
    if has_state:
        in_specs.append(state_spec)
        args.append(s0)
        out_specs, out_shapes = out_spec, out_shape
    else:
        out_specs = [out_spec, state_spec]
        out_shapes = [out_shape, jax.ShapeDtypeStruct((grp.n_seq, N_DIR, A_HEADS, A_DK, A_DV), F32)]
    tok = lambda w: pltpu.VMEM((seq_len, w), F32)
    out = pl.pallas_call(
        functools.partial(_gdn_kernel, seq_len, has_state),
        grid=(grp.n_seq,),
        in_specs=in_specs, out_specs=out_specs, out_shape=out_shapes,
        scratch_shapes=[tok(EV_A), tok(EV_A), tok(EV_A), tok(LANES), pltpu.VMEM((N_DIR, seq_len, LANES), F32),
                        tok(EV_A), pltpu.VMEM((N_DIR, A_HEADS, A_DK, A_DV), F32)],
        compiler_params=_params(1),
        name="gdn_mixer",
    )(*args)
    return (out, None) if has_state else (out[0], out[1])


def _rope_tables(n_tokens):
    lane = np.arange(EV_B)
    axis = (lane % B_DK) // (2 * ROPE_FREQS)
    half = (lane % (2 * ROPE_FREQS)) // ROPE_FREQS
    freq = ROPE_BASE ** (-(lane % ROPE_FREQS).astype(np.float32) / ROPE_FREQS)
    tok = jnp.arange(n_tokens, dtype=F32)
    pos = jnp.where(jnp.asarray(axis)[None, :] == 0, jnp.floor(tok / GRID_W)[:, None], (tok % GRID_W)[:, None])
    ang = pos * jnp.asarray(freq, F32)[None, :]
    sign = jnp.asarray(np.where(half == 0, -1.0, 1.0), F32)[None, :]
    return jnp.cos(ang), jnp.sin(ang) * sign


def _dot3_lhs(x, m):
    p1, p2, p3 = _split3(x)
    dot = lambda p: jnp.dot(p, m, preferred_element_type=F32)
    return dot(p1) + dot(p2) + dot(p3)


def _ret_kernel(seq_len, has_state, rope, proj_ref, lgd_ref, gain_ref, *rest):
    rest = list(rest)
    cos_ref, sin_ref = (rest.pop(0), rest.pop(0)) if rope else (None, None)
    s0_ref = rest.pop(0) if has_state else None
    out_ref = rest.pop(0)
    fin_ref = None if has_state else rest.pop(0)

    q = proj_ref[:, 0:EV_B]
    k = proj_ref[:, EV_B:2 * EV_B]
    if rope:
        lane = lax.broadcasted_iota(jnp.int32, (seq_len, EV_B), 1)
        first = (lane % (2 * ROPE_FREQS)) < ROPE_FREQS

        def rotate(x):
            partner = jnp.where(first, pltpu.roll(x, EV_B - ROPE_FREQS, 1), pltpu.roll(x, ROPE_FREQS, 1))
            return x * cos_ref[...] + partner * sin_ref[...]

        q, k = rotate(q), rotate(k)
    q = q * (B_DK ** -0.5)
    log_gamma = -jnp.exp(lgd_ref[...])
    i_col = lax.broadcasted_iota(jnp.int32, (seq_len, 1), 0).astype(F32)
    dist = (lax.broadcasted_iota(jnp.int32, (seq_len, seq_len), 0)
            - lax.broadcasted_iota(jnp.int32, (seq_len, seq_len), 1)).astype(F32)
    lane_p = lax.broadcasted_iota(jnp.int32, (1, LANES), 1)
    h_avg = jnp.where((lax.broadcasted_iota(jnp.int32, (LANES, LANES), 0) // B_DV)
                      == (lax.broadcasted_iota(jnp.int32, (LANES, LANES), 1) // B_DV), 1.0 / B_DV, 0.0).astype(BF16)

    for pair in range(B_HEADS // 2):
        cols = slice(pair * LANES, (pair + 1) * LANES)
        qp, kp = q[:, cols], k[:, cols]
        vp = proj_ref[:, 2 * EV_B + pair * LANES:2 * EV_B + (pair + 1) * LANES]
        kp16, vp16 = kp.astype(BF16), vp.astype(BF16)
        o = jnp.zeros((seq_len, LANES), F32)
        lg = [[log_gamma[d:d + 1, 2 * pair + e:2 * pair + e + 1] for e in range(2)] for d in range(N_DIR)]
        for e in range(2):
            mine = (lane_p < B_DK) if e == 0 else (lane_p >= B_DK)
            scores = _dot_nt(jnp.where(mine, qp, 0.0).astype(BF16), kp16)
            weight = (jnp.where(dist >= 0.0, jnp.exp(jnp.maximum(dist, 0.0) * lg[0][e]), 0.0)
                      + jnp.where(dist <= 0.0, jnp.exp(jnp.maximum(-dist, 0.0) * lg[1][e]), 0.0))
            o = o + jnp.dot((scores * weight).astype(BF16), jnp.where(mine, vp, 0.0).astype(BF16),
                            preferred_element_type=F32)
        if has_state:
            for d in range(N_DIR):
                steps = (i_col + 1.0) if d == 0 else (seq_len - i_col)
                xi = jnp.where(lane_p < B_DK, jnp.exp(steps * lg[d][0]), jnp.exp(steps * lg[d][1]))
                zeros = jnp.zeros((B_DK, B_DV), F32)
                s_pair = jnp.concatenate(
                    [jnp.concatenate([s0_ref[d, 2 * pair], zeros], axis=1),
                     jnp.concatenate([zeros, s0_ref[d, 2 * pair + 1]], axis=1)], axis=0)
                o = o + jnp.dot((qp * xi).astype(BF16), s_pair.astype(BF16), preferred_element_type=F32)
        else:
            for d in range(N_DIR):
                steps = (seq_len - 1.0 - i_col) if d == 0 else i_col
                zeta = jnp.where(lane_p < B_DK, jnp.exp(steps * lg[d][0]), jnp.exp(steps * lg[d][1]))
                kz_t = (kp * zeta).T.astype(BF16)
                both = jnp.dot(kz_t, vp16, preferred_element_type=F32)
                fin_ref[d, 2 * pair] = both[:B_DK, :B_DV]
                fin_ref[d, 2 * pair + 1] = both[B_DK:, B_DV:]
        mu = _dot3_lhs(o, h_avg)
        cen = o - mu
        var = _dot3_lhs(cen * cen, h_avg)
        normed = cen * lax.rsqrt(var + RMS_EPS) * gain_ref[:, cols]
        out_ref[:, cols] = normed * _silu(proj_ref[:, 3 * EV_B + pair * LANES:3 * EV_B + (pair + 1) * LANES])


def _ret_mixer(grp, proj, log_decay, gain, s0, rope_tables):
    seq_len = grp.seq_len
    has_state = s0 is not None
    rope = rope_tables is not None
    lgd = jnp.zeros((8, LANES), F32).at[:N_DIR, :B_HEADS].set(log_decay)
    full = lambda *shape: pl.BlockSpec(shape, lambda b: (0,) * len(shape))
    state_spec = pl.BlockSpec((None, N_DIR, B_HEADS, B_DK, B_DV), lambda b: (b, 0, 0, 0, 0))
    in_specs = [pl.BlockSpec((seq_len, EV_HALF), lambda b: (b, 1)), full(8, LANES), full(1, EV_B)]
    args = [proj, lgd, gain[None]]
    if rope:
        in_specs += [full(seq_len, EV_B), full(seq_len, EV_B)]
        args += list(rope_tables)
    out_spec = pl.BlockSpec((seq_len, EV_B), lambda b: (b, 0))
    out_shape = jax.ShapeDtypeStruct((grp.n_seq * seq_len, EV_B), F32)
    if has_state:
        in_specs.append(state_spec)
        args.append(s0)
        out_specs, out_shapes = out_spec, out_shape
    else:
        out_specs = [out_spec, state_spec]
        out_shapes = [out_shape, jax.ShapeDtypeStruct((grp.n_seq, N_DIR, B_HEADS, B_DK, B_DV), F32)]
    out = pl.pallas_call(
        functools.partial(_ret_kernel, seq_len, has_state, rope),
        grid=(grp.n_seq,),
        in_specs=in_specs, out_specs=out_specs, out_shape=out_shapes,
        compiler_params=_params(1),
        name="retention_mixer",
    )(*args)
    return (out, None) if has_state else (out[0], out[1])


def _pad_cols(w, n):
    return jnp.pad(w, ((0, 0), (0, n - w.shape[1])))


def kernel(x_prompt, x_sample, state_gdn, state_ret, state_s5, state_gla, c, c_ctx, ada_w, ada_b, norm_mix,
           norm_ffn, norm_final, ev_w_in, ev_conv, gdn_a_log, gdn_dt_bias, gdn_gain, ret_log_decay, ret_gain,
           ev_w_out, od_w_in, s5_a_re, s5_a_im, s5_log_dt, s5_b_re, s5_b_im, s5_c_re, s5_c_im, s5_d, s5_w_glu,
           gla_w_gate2, gla_b_gate, gla_gain, od_w_out, moe_router, moe_w_gate, moe_w_up, moe_w_down):
    cond = jnp.zeros((COND_ROWS, D_MODEL), F32).at[0].set(c_ctx).at[1:N_COND].set(c)
    mods = _modulation(cond, ada_w, ada_b)
    rope_tables = _rope_tables(DEC_SEQ)
    even_perm = _even_col_perm()

    xs = {CTX: x_prompt.reshape(BATCH * SEQ, D_MODEL), DEC: x_sample.reshape(DEC_BATCH * DEC_SEQ, D_MODEL)}
    new_states = {"gdn": [], "ret": [], "s5": [], "gla": []}

    for layer in range(DEPTH):
        j = layer // 2
        even = layer % 2 == 0
        if even:
            w_in = _pad_cols(ev_w_in[j][:, even_perm], EVEN_IN_PAD).astype(BF16)
            w_out = ev_w_out[j].astype(BF16)
            w_outs = [w_out[:EV_A], w_out[EV_A:]]
        else:
            w_in = _pad_cols(od_w_in[j], ODD_IN_PAD).astype(BF16)
            w_out = od_w_out[j].astype(BF16)
            w_outs = [w_out[:C_WIDTH], w_out[C_WIDTH:]]
            s5_prep = _s5_prepare(s5_a_re[j], s5_a_im[j], s5_log_dt[j], s5_b_re[j], s5_b_im[j], s5_c_re[j],
                                  s5_c_im[j])
        w_router = _pad_cols(moe_router[layer], LANES)
        routed = {}
        for grp in (CTX, DEC):
            x = xs[grp]
            ctx = grp.is_context
            proj = _in_projection(grp, layer, x, mods, norm_mix[layer][None], w_in)
            if even:
                out_a, fin_a = _gdn_mixer(grp, proj, ev_conv[j], gdn_a_log[j], gdn_dt_bias[j], gdn_gain[j],
                                          None if ctx else state_gdn[:, j])
                out_b, fin_b = _ret_mixer(grp, proj, ret_log_decay[j], ret_gain[j],
                                          None if ctx else state_ret[:, j], None if ctx else rope_tables)
                mixes = [out_a, out_b]
                if ctx:
                    new_states["gdn"].append(fin_a)
                    new_states["ret"].append(fin_b)
            else:
                ypre, fin_c = _s5_mixer(grp, proj, s5_prep, None if ctx else _s5_state_to_blocks(state_s5[:, j]))
                out_c, out_d, fin_d = _gla_mixer(grp, proj, ypre, s5_d[j], s5_w_glu[j], gla_w_gate2[j],
                                                 gla_b_gate[j], gla_gain[j], None if ctx else state_gla[:, j])
                mixes = [out_c, out_d]
                if ctx:
                    new_states["s5"].append(_s5_state_from_blocks(fin_c))
                    new_states["gla"].append(fin_d)
            x1, h2, aff = _out_projection(grp, layer, mixes, x, mods, w_outs, norm_ffn[layer][None], w_router)
            xs_g, pt, gate = _route(grp, aff, h2)
            routed[grp] = (x1, xs_g, pt, gate)
        ys_c, ys_d = _experts(layer, routed[CTX][1], routed[DEC][1], routed[CTX][3], routed[DEC][3],
                              moe_w_gate, moe_w_up, moe_w_down)
        for grp, ys in ((CTX, ys_c), (DEC, ys_d)):
            x1, _, pt, _ = routed[grp]
            xs[grp] = _combine(grp, layer, layer == DEPTH - 1, x1, mods, pt, ys, norm_final[None])

    y_prompt = xs[CTX].reshape(BATCH, SEQ, D_MODEL)
    y_sample = xs[DEC].reshape(DEC_BATCH, DEC_SEQ, D_MODEL)
    return (y_prompt, y_sample, jnp.stack(new_states["gdn"], axis=1), jnp.stack(new_states["ret"], axis=1),
            jnp.stack(new_states["s5"], axis=1), jnp.stack(new_states["gla"], axis=1))
```

```python
import functools
import math
from typing import NamedTuple

import numpy as np
import jax
import jax.numpy as jnp
from jax import lax
from jax.experimental import pallas as pl
from jax.experimental.pallas import tpu as pltpu

F32 = jnp.float32
BF16 = jnp.bfloat16

D_MODEL = 1024
BATCH = 32
SEQ = 256
DEPTH = 4
DEC_BATCH = 2
DEC_SEQ = 1024
GRID_W = 64
N_DIR = 2
A_HEADS, A_DK, A_DV, A_CONV, A_CHUNK = 4, 128, 128, 5, 64
B_HEADS, B_DK, B_DV, B_CHUNK = 8, 64, 64, 64
ROPE_FREQS = B_DK // 4
ROPE_BASE = 10000.0
C_GROUP, C_GROUPS, C_STATE = 16, 32, 64
C_WIDTH = C_GROUPS * C_GROUP
S5_MAX_RE = -1e-4
D_HEADS, D_DK, D_DV, D_RANK, D_TAU, D_CHUNK = 4, 64, 128, 16, 16.0, 16
N_EXPERTS = 16
EXPERT_FF = 1024
EC_CAPACITY_FACTOR = 2
RMS_EPS = 1e-6

EVEN_SPLITS = (A_HEADS * A_DK, A_HEADS * A_DK, A_HEADS * A_DV, A_HEADS * A_DV, N_DIR * A_HEADS, N_DIR * A_HEADS,
               B_HEADS * B_DK, B_HEADS * B_DK, B_HEADS * B_DV, B_HEADS * B_DV)
ODD_SPLITS = (C_WIDTH, D_HEADS * D_DK, D_HEADS * D_DK, D_HEADS * D_DV, D_HEADS * D_DV, N_DIR * D_RANK)

LANES = 128
TOKEN_TILE = 256
N_COND = 1 + DEC_BATCH
COND_ROWS = 8
MOD_COL_TILE = 1536
VMEM_LIMIT = 56 * 1024 * 1024


class Group(NamedTuple):
    n_seq: int
    seq_len: int
    is_context: bool

    @property
    def tiles_per_seq(self):
        return self.seq_len // TOKEN_TILE

    @property
    def n_tiles(self):
        return self.n_seq * self.tiles_per_seq

    @property
    def capacity(self):
        return EC_CAPACITY_FACTOR * self.seq_len // N_EXPERTS

    def cond_of_tile(self, i):
        return 0 if self.is_context else 1 + i // self.tiles_per_seq

    def cond_of_seq(self, b):
        return 0 if self.is_context else 1 + b


CTX = Group(BATCH, SEQ, True)
DEC = Group(DEC_BATCH, DEC_SEQ, False)


def _round_up(n, m):
    return (n + m - 1) // m * m


def _params(n_axes):
    return pltpu.CompilerParams(dimension_semantics=("arbitrary",) * n_axes, vmem_limit_bytes=VMEM_LIMIT)


def _mm3(a, b):
    a1 = a.astype(BF16)
    a2 = (a - a1.astype(F32)).astype(BF16)
    b1 = b.astype(BF16)
    b2 = (b - b1.astype(F32)).astype(BF16)
    dot = lambda x, y: jnp.dot(x, y, preferred_element_type=F32)
    return dot(a1, b1) + (dot(a1, b2) + dot(a2, b1))


def _mod_kernel(c_ref, w_ref, b_ref, o_ref):
    c = c_ref[...]
    s = c * jax.nn.sigmoid(c)
    o_ref[0] = jnp.dot(s.astype(BF16), w_ref[0].astype(BF16), preferred_element_type=F32) + b_ref[0]


def _modulation(cond, ada_w, ada_b):
    n_out = 6 * D_MODEL
    out = pl.pallas_call(
        _mod_kernel,
        grid=(DEPTH, n_out // MOD_COL_TILE),
        in_specs=[pl.BlockSpec((COND_ROWS, D_MODEL), lambda l, j: (0, 0)),
                  pl.BlockSpec((1, D_MODEL, MOD_COL_TILE), lambda l, j: (l, 0, j)),
                  pl.BlockSpec((1, 1, MOD_COL_TILE), lambda l, j: (l, 0, j))],
        out_specs=pl.BlockSpec((1, COND_ROWS, MOD_COL_TILE), lambda l, j: (l, 0, j)),
        out_shape=jax.ShapeDtypeStruct((DEPTH, COND_ROWS, n_out), F32),
        compiler_params=_params(2),
        name="adaln_modulation",
    )(cond, ada_w, ada_b.reshape(DEPTH, 1, n_out))
    out = out[:, :N_COND].reshape(DEPTH, N_COND, 6, D_MODEL)
    return out.transpose(1, 0, 2, 3).reshape(N_COND, DEPTH * 6, D_MODEL)


def _norm_mod(x, gain, shift, scale):
    y = x * lax.rsqrt(jnp.mean(x * x, axis=-1, keepdims=True) + RMS_EPS)
    return (y * gain) * (1.0 + scale) + shift


def _mod_row(mod_ref, layer, k):
    r = 6 * layer + k
    return mod_ref[r:r + 1, :]


def _inproj_kernel(layer, x_ref, mod_ref, gain_ref, w_ref, o_ref):
    h = _norm_mod(x_ref[...], gain_ref[...], _mod_row(mod_ref, layer, 0), _mod_row(mod_ref, layer, 1))
    o_ref[...] = jnp.dot(h.astype(BF16), w_ref[...], preferred_element_type=F32)


def _in_projection(grp, layer, x, mods, gain, w):
    n_tok = grp.n_seq * grp.seq_len
    n_out = w.shape[1]
    return pl.pallas_call(
        functools.partial(_inproj_kernel, layer),
        grid=(grp.n_tiles,),
        in_specs=[pl.BlockSpec((TOKEN_TILE, D_MODEL), lambda i: (i, 0)),
                  pl.BlockSpec((None, DEPTH * 6, D_MODEL), lambda i: (grp.cond_of_tile(i), 0, 0)),
                  pl.BlockSpec((1, D_MODEL), lambda i: (0, 0)),
                  pl.BlockSpec((D_MODEL, n_out), lambda i: (0, 0))],
        out_specs=pl.BlockSpec((TOKEN_TILE, n_out), lambda i: (i, 0)),
        out_shape=jax.ShapeDtypeStruct((n_tok, n_out), F32),
        compiler_params=_params(1),
        name="in_projection",
    )(x, mods, gain, w)


def _outproj_kernel(layer, n_parts, *refs):
    mix_refs, wout_refs = refs[:n_parts], refs[n_parts:2 * n_parts]
    x_ref, mod_ref, gain_ref, wr_ref, x1_ref, h2_ref, aff_ref = refs[2 * n_parts:]
    y = jnp.dot(mix_refs[0][...].astype(BF16), wout_refs[0][...], preferred_element_type=F32)
    for m_ref, w_ref in zip(mix_refs[1:], wout_refs[1:]):
        y = y + jnp.dot(m_ref[...].astype(BF16), w_ref[...], preferred_element_type=F32)
    x1 = x_ref[...] + _mod_row(mod_ref, layer, 2) * y
    x1_ref[...] = x1
    h2 = _norm_mod(x1, gain_ref[...], _mod_row(mod_ref, layer, 3), _mod_row(mod_ref, layer, 4))
    h2_ref[...] = h2.astype(BF16)
    logits = _mm3(h2, wr_ref[...])
    lane = lax.broadcasted_iota(jnp.int32, logits.shape, 1)
    logits = jnp.where(lane < N_EXPERTS, logits, -jnp.inf)
    e = jnp.exp(logits - jnp.max(logits, axis=-1, keepdims=True))
    aff_ref[...] = e / jnp.sum(e, axis=-1, keepdims=True)


def _out_projection(grp, layer, mixes, x, mods, w_outs, gain, w_router):
    n_tok = grp.n_seq * grp.seq_len
    tile = lambda width: pl.BlockSpec((TOKEN_TILE, width), lambda i: (i, 0))
    return pl.pallas_call(
        functools.partial(_outproj_kernel, layer, len(mixes)),
        grid=(grp.n_tiles,),
        in_specs=[tile(m.shape[1]) for m in mixes]
                 + [pl.BlockSpec(w.shape, lambda i: (0, 0)) for w in w_outs]
                 + [tile(D_MODEL),
                    pl.BlockSpec((None, DEPTH * 6, D_MODEL), lambda i: (grp.cond_of_tile(i), 0, 0)),
                    pl.BlockSpec((1, D_MODEL), lambda i: (0, 0)),
                    pl.BlockSpec((D_MODEL, LANES), lambda i: (0, 0))],
        out_specs=[tile(D_MODEL), tile(D_MODEL), tile(LANES)],
        out_shape=[jax.ShapeDtypeStruct((n_tok, D_MODEL), F32),
                   jax.ShapeDtypeStruct((n_tok, D_MODEL), BF16),
                   jax.ShapeDtypeStruct((n_tok, LANES), F32)],
        compiler_params=_params(1),
        name="out_projection_router",
    )(*mixes, *w_outs, x, mods, gain, w_router)


def _route_kernel(n, cap, aff_ref, h2_ref, xs_ref, pt_ref, gate_ref):
    aff = aff_ref[...]
    aff_t = aff.T
    t_sub = lax.broadcasted_iota(jnp.int32, (n, n), 0)
    t_lane = lax.broadcasted_iota(jnp.int32, (n, n), 1)
    earlier = t_sub < t_lane
    sel_rows = []
    for e in range(N_EXPERTS):
        col = aff[:, e:e + 1]
        row = aff_t[e:e + 1, :]
        beats = (col > row) | ((col == row) & earlier)
        rank = jnp.sum(jnp.where(beats, 1.0, 0.0), axis=0, keepdims=True)
        sel_rows.append(jnp.where(rank < cap, 1.0, 0.0))
    sel = jnp.concatenate(sel_rows, axis=0)
    pos = jnp.dot(sel.astype(BF16), jnp.where(earlier, 1.0, 0.0).astype(BF16), preferred_element_type=F32)
    e_idx = lax.broadcasted_iota(jnp.int32, (N_EXPERTS, n), 0)
    slot = jnp.where(sel > 0.0, pos.astype(jnp.int32) + e_idx * cap, -1)
    slot_pad = jnp.concatenate([slot, jnp.full((LANES - N_EXPERTS, n), -1, jnp.int32)], axis=0)
    slot_t = slot_pad.astype(F32).T.astype(jnp.int32)
    per_block = LANES // cap
    s_lane = lax.broadcasted_iota(jnp.int32, (n, LANES), 1)
    for blk in range(N_EXPERTS // per_block):
        hit = jnp.zeros((n, LANES), F32)
        for e in range(blk * per_block, (blk + 1) * per_block):
            hit = hit + jnp.where(slot_t[:, e:e + 1] == s_lane + blk * LANES, 1.0, 0.0)
        pt_ref[:, blk * LANES:(blk + 1) * LANES] = hit.astype(BF16)
    c_sub = lax.broadcasted_iota(jnp.int32, (cap, n), 0)
    h2 = h2_ref[...]
    for e in range(N_EXPERTS):
        p_e = jnp.where(slot[e:e + 1, :] == c_sub + e * cap, 1.0, 0.0)
        xs_ref[e] = jnp.dot(p_e.astype(BF16), h2, preferred_element_type=F32).astype(BF16)
        gate = jnp.sum(p_e * aff_t[e:e + 1, :], axis=1, keepdims=True)
        gate_ref[e] = jnp.broadcast_to(gate, (cap, LANES))


def _route(grp, aff, h2):
    n, cap = grp.seq_len, grp.capacity
    slots = N_EXPERTS * cap
    return pl.pallas_call(
        functools.partial(_route_kernel, n, cap),
        grid=(grp.n_seq,),
        in_specs=[pl.BlockSpec((n, LANES), lambda b: (b, 0)),
                  pl.BlockSpec((n, D_MODEL), lambda b: (b, 0))],
        out_specs=[pl.BlockSpec((N_EXPERTS, cap, D_MODEL), lambda b: (0, b, 0)),
                   pl.BlockSpec((n, slots), lambda b: (b, 0)),
                   pl.BlockSpec((N_EXPERTS, cap, LANES), lambda b: (0, b, 0))],
        out_shape=[jax.ShapeDtypeStruct((N_EXPERTS, grp.n_seq * cap, D_MODEL), BF16),
                   jax.ShapeDtypeStruct((grp.n_seq * n, slots), BF16),
                   jax.ShapeDtypeStruct((N_EXPERTS, grp.n_seq * cap, LANES), F32)],
        compiler_params=_params(1),
        name="expert_choice_route",
    )(aff, h2)


FF_TILE = 512


def _expert_kernel(xc_ref, xd_ref, gc_ref, gd_ref, wg_ref, wu_ref, wd_ref, yc_ref, yd_ref, accc_ref, accd_ref):
    f = pl.program_id(1)
    wg = wg_ref[...].astype(BF16)
    wu = wu_ref[...].astype(BF16)
    wd = wd_ref[...].astype(BF16)

    def ffn(x_ref, acc_ref):
        x = x_ref[...]
        a = jnp.dot(x, wg, preferred_element_type=F32)
        u = jnp.dot(x, wu, preferred_element_type=F32)
        hid = (a * jax.nn.sigmoid(a)) * u
        y = jnp.dot(hid.astype(BF16), wd, preferred_element_type=F32)

        @pl.when(f == 0)
        def _():
            acc_ref[...] = y

        @pl.when(f != 0)
        def _():
            acc_ref[...] += y

    ffn(xc_ref, accc_ref)
    ffn(xd_ref, accd_ref)

    @pl.when(f == pl.num_programs(1) - 1)
    def _():
        yc_ref[...] = (accc_ref[...] * gc_ref[:, 0:1]).astype(BF16)
        yd_ref[...] = (accd_ref[...] * gd_ref[:, 0:1]).astype(BF16)


def _experts(layer, xs_c, xs_d, gate_c, gate_d, w_gate, w_up, w_down):
    rc, rd = xs_c.shape[1], xs_d.shape[1]
    per_e = lambda rows, width: pl.BlockSpec((None, rows, width), lambda e, f: (e, 0, 0))
    return pl.pallas_call(
        _expert_kernel,
        grid=(N_EXPERTS, EXPERT_FF // FF_TILE),
        in_specs=[per_e(rc, D_MODEL), per_e(rd, D_MODEL), per_e(rc, LANES), per_e(rd, LANES),
                  pl.BlockSpec((None, None, D_MODEL, FF_TILE), lambda e, f: (layer, e, 0, f)),
                  pl.BlockSpec((None, None, D_MODEL, FF_TILE), lambda e, f: (layer, e, 0, f)),
                  pl.BlockSpec((None, None, FF_TILE, D_MODEL), lambda e, f: (layer, e, f, 0))],
        out_specs=[per_e(rc, D_MODEL), per_e(rd, D_MODEL)],
        out_shape=[jax.ShapeDtypeStruct(xs_c.shape, BF16), jax.ShapeDtypeStruct(xs_d.shape, BF16)],
        scratch_shapes=[pltpu.VMEM((rc, D_MODEL), F32), pltpu.VMEM((rd, D_MODEL), F32)],
        compiler_params=_params(2),
        name="expert_swiglu",
    )(xs_c, xs_d, gate_c, gate_d, w_gate, w_up, w_down)


def _combine_kernel(layer, cap, final, x1_ref, mod_ref, pt_ref, ys_ref, gain_ref, x2_ref):
    ys = ys_ref[...].reshape(N_EXPERTS * cap, D_MODEL)
    y = jnp.dot(pt_ref[...], ys, preferred_element_type=F32)
    x2 = x1_ref[...] + _mod_row(mod_ref, layer, 5) * y
    if final:
        x2 = x2 * lax.rsqrt(jnp.mean(x2 * x2, axis=-1, keepdims=True) + RMS_EPS) * gain_ref[...]
    x2_ref[...] = x2


def _combine(grp, layer, final, x1, mods, pt, ys, gain_final):
    n, cap = grp.seq_len, grp.capacity
    return pl.pallas_call(
        functools.partial(_combine_kernel, layer, cap, final),
        grid=(grp.n_seq,),
        in_specs=[pl.BlockSpec((n, D_MODEL), lambda b: (b, 0)),
                  pl.BlockSpec((None, DEPTH * 6, D_MODEL), lambda b: (grp.cond_of_seq(b), 0, 0)),
                  pl.BlockSpec((n, N_EXPERTS * cap), lambda b: (b, 0)),
                  pl.BlockSpec((N_EXPERTS, cap, D_MODEL), lambda b: (0, b, 0)),
                  pl.BlockSpec((1, D_MODEL), lambda b: (0, 0))],
        out_specs=pl.BlockSpec((n, D_MODEL), lambda b: (b, 0)),
        out_shape=jax.ShapeDtypeStruct(x1.shape, F32),
        compiler_params=_params(1),
        name="expert_combine",
    )(x1, mods, pt, ys, gain_final)


S5_BLOCKS = 2
S5_BLOCK_IN = C_WIDTH // S5_BLOCKS
S5_BLOCK_STATE = C_GROUPS * C_STATE // S5_BLOCKS
S5_SUBLANES = 8
S5_SCAN_COLS = 256
S5_SHIFTS = (1, 2, 4)


def _cmul(ar, ai, br, bi):
    return ar * br - ai * bi, ar * bi + ai * br


def _s5_prep_kernel(are_ref, aim_ref, ldt_ref, bre_ref, bim_ref, cre_ref, cim_ref, bmat_ref, cmat_ref, const_ref):
    n = S5_BLOCK_STATE
    row = lax.broadcasted_iota(jnp.int32, (S5_SUBLANES, n), 0)
    for d in range(N_DIR):
        a_re = jnp.minimum(are_ref[d], S5_MAX_RE)
        a_im = aim_ref[d]
        dt = jnp.exp(ldt_ref[d])
        mag = jnp.exp(a_re * dt)
        l_re = mag * jnp.cos(a_im * dt)
        l_im = mag * jnp.sin(a_im * dt)
        den = a_re * a_re + a_im * a_im
        k_re = ((l_re - 1.0) * a_re + l_im * a_im) / den
        k_im = (l_im * a_re - (l_re - 1.0) * a_im) / den
        b_re, b_im = bre_ref[...], bim_ref[...]
        bb_re, bb_im = _cmul(k_re, k_im, b_re, b_im)
        bmat_ref[d] = jnp.concatenate([bb_re, bb_im], axis=1).astype(BF16)
        pows = [(l_re, l_im)]
        for _ in range(S5_SUBLANES - 1):
            pows.append(_cmul(pows[-1][0], pows[-1][1], l_re, l_im))
        for i, s in enumerate(S5_SHIFTS):
            keep = (row >= s) if d == 0 else (row <= S5_SUBLANES - 1 - s)
            const_ref[d, 2 * i] = jnp.where(keep, pows[s - 1][0], 0.0)
            const_ref[d, 2 * i + 1] = jnp.where(keep, pows[s - 1][1], 0.0)
        lp_re = jnp.zeros((S5_SUBLANES, n), F32)
        lp_im = jnp.zeros((S5_SUBLANES, n), F32)
        for r in range(S5_SUBLANES):
            p = pows[r] if d == 0 else pows[S5_SUBLANES - 1 - r]
            lp_re = jnp.where(row == r, p[0], lp_re)
            lp_im = jnp.where(row == r, p[1], lp_im)
        const_ref[d, 6] = lp_re
        const_ref[d, 7] = lp_im
    cmat_ref[...] = jnp.concatenate([cre_ref[...], -cim_ref[...]], axis=0).astype(BF16)


def _s5_prepare(a_re, a_im, log_dt, b_re, b_im, c_re, c_im):
    n, k_in = S5_BLOCK_STATE, S5_BLOCK_IN
    gpb = C_GROUPS // S5_BLOCKS
    per_state = lambda t: t.reshape(N_DIR, S5_BLOCKS, 1, n)
    ldt = jnp.repeat(log_dt, C_STATE, axis=-1)
    eye = jnp.eye(gpb, dtype=F32)

    def expand_b(b):
        b = b.reshape(S5_BLOCKS, gpb, C_STATE, C_GROUP).transpose(0, 1, 3, 2)
        return (b[:, :, :, None, :] * eye[None, :, None, :, None]).reshape(S5_BLOCKS, k_in, n)

    def expand_c(c):
        c = c.reshape(S5_BLOCKS, gpb, C_GROUP, C_STATE).transpose(0, 1, 3, 2)
        return (c[:, :, :, None, :] * eye[None, :, None, :, None]).reshape(S5_BLOCKS, n, k_in)

    row_spec = pl.BlockSpec((N_DIR, None, 1, n), lambda k: (0, k, 0, 0))
    return pl.pallas_call(
        _s5_prep_kernel,
        grid=(S5_BLOCKS,),
        in_specs=[row_spec, row_spec, row_spec,
                  pl.BlockSpec((None, k_in, n), lambda k: (k, 0, 0)),
                  pl.BlockSpec((None, k_in, n), lambda k: (k, 0, 0)),
                  pl.BlockSpec((None, n, k_in), lambda k: (k, 0, 0)),
                  pl.BlockSpec((None, n, k_in), lambda k: (k, 0, 0))],
        out_specs=[pl.BlockSpec((N_DIR, None, k_in, 2 * n), lambda k: (0, k, 0, 0)),
                   pl.BlockSpec((None, 2 * n, k_in), lambda k: (k, 0, 0)),
                   pl.BlockSpec((N_DIR, None, 8, S5_SUBLANES, n), lambda k: (0, k, 0, 0, 0))],
        out_shape=[jax.ShapeDtypeStruct((N_DIR, S5_BLOCKS, k_in, 2 * n), BF16),
                   jax.ShapeDtypeStruct((S5_BLOCKS, 2 * n, k_in), BF16),
                   jax.ShapeDtypeStruct((N_DIR, S5_BLOCKS, 8, S5_SUBLANES, n), F32)],
        compiler_params=_params(1),
        name="s5_prepare",
    )(per_state(a_re), per_state(a_im), per_state(ldt), expand_b(b_re), expand_b(b_im), expand_c(c_re),
      expand_c(c_im))


def _s5_kernel(seq_len, has_state, u_ref, bmat_ref, cmat_ref, const_ref, *rest):
    if has_state:
        s0_ref, y_ref, fin_ref, xre_ref, xim_ref = rest
    else:
        y_ref, fin_ref, xre_ref, xim_ref = rest
    n = S5_BLOCK_STATE
    n_tiles = seq_len // S5_SUBLANES
    u = u_ref[...].astype(BF16)
    for d in range(N_DIR):
        bu = jnp.dot(u, bmat_ref[d], preferred_element_type=F32)
        xre_ref[d] = bu[:, :n]
        xim_ref[d] = bu[:, n:]

    for cb in range(n // S5_SCAN_COLS):
        cols = slice(cb * S5_SCAN_COLS, (cb + 1) * S5_SCAN_COLS)

        def scan_tile(d, i, carry):
            rows = pl.ds(pl.multiple_of(i * S5_SUBLANES, S5_SUBLANES), S5_SUBLANES)
            xr = xre_ref[d, rows, cols]
            xi = xim_ref[d, rows, cols]
            for k, s in enumerate(S5_SHIFTS):
                shift = s if d == 0 else S5_SUBLANES - s
                pr, pi = _cmul(const_ref[d, 2 * k, :, cols], const_ref[d, 2 * k + 1, :, cols],
                               pltpu.roll(xr, shift, 0), pltpu.roll(xi, shift, 0))
                xr, xi = xr + pr, xi + pi
            cr, ci = _cmul(const_ref[d, 6, :, cols], const_ref[d, 7, :, cols], carry[0], carry[1])
            xr, xi = xr + cr, xi + ci
            xre_ref[d, rows, cols] = xr
            xim_ref[d, rows, cols] = xi
            edge = S5_SUBLANES - 1 if d == 0 else 0
            shape = (S5_SUBLANES, S5_SCAN_COLS)
            return (jnp.broadcast_to(xr[edge:edge + 1], shape), jnp.broadcast_to(xi[edge:edge + 1], shape))

        def body(i, carry):
            return (scan_tile(0, i, carry[0]), scan_tile(1, n_tiles - 1 - i, carry[1]))

        shape = (S5_SUBLANES, S5_SCAN_COLS)
        if has_state:
            init = tuple((jnp.broadcast_to(s0_ref[2 * d:2 * d + 1, cols], shape),
                          jnp.broadcast_to(s0_ref[2 * d + 1:2 * d + 2, cols], shape)) for d in range(N_DIR))
        else:
            init = tuple((jnp.zeros(shape, F32), jnp.zeros(shape, F32)) for _ in range(N_DIR))
        fin = lax.fori_loop(0, n_tiles, body, init, unroll=2)
        for d in range(N_DIR):
            fin_ref[2 * d:2 * d + 1, cols] = fin[d][0][0:1]
            fin_ref[2 * d + 1:2 * d + 2, cols] = fin[d][1][0:1]

    x = jnp.concatenate([xre_ref[0] + xre_ref[1], xim_ref[0] + xim_ref[1]], axis=1).astype(BF16)
    y_ref[...] = jnp.dot(x, cmat_ref[...], preferred_element_type=F32)


def _s5_mixer(grp, proj, prep, s0):
    bmat, cmat, consts = prep
    n, k_in, seq_len = S5_BLOCK_STATE, S5_BLOCK_IN, grp.seq_len
    has_state = s0 is not None
    state_spec = pl.BlockSpec((None, None, 2 * N_DIR, n), lambda b, k: (b, k, 0, 0))
    in_specs = [pl.BlockSpec((seq_len, k_in), lambda b, k: (b, k)),
                pl.BlockSpec((N_DIR, None, k_in, 2 * n), lambda b, k: (0, k, 0, 0)),
                pl.BlockSpec((None, 2 * n, k_in), lambda b, k: (k, 0, 0)),
                pl.BlockSpec((N_DIR, None, 8, S5_SUBLANES, n), lambda b, k: (0, k, 0, 0, 0))]
    args = [proj, bmat, cmat, consts]
    if has_state:
        in_specs.append(state_spec)
        args.append(s0)
    return pl.pallas_call(
        functools.partial(_s5_kernel, seq_len, has_state),
        grid=(grp.n_seq, S5_BLOCKS),
        in_specs=in_specs,
        out_specs=[pl.BlockSpec((seq_len, k_in), lambda b, k: (b, k)), state_spec],
        out_shape=[jax.ShapeDtypeStruct((grp.n_seq * seq_len, C_WIDTH), F32),
                   jax.ShapeDtypeStruct((grp.n_seq, S5_BLOCKS, 2 * N_DIR, n), F32)],
        scratch_shapes=[pltpu.VMEM((N_DIR, seq_len, n), F32), pltpu.VMEM((N_DIR, seq_len, n), F32)],
        compiler_params=_params(2),
        name="s5_scan",
    )(*args)


def _s5_state_to_blocks(s):
    b = s.shape[0]
    s = s.reshape(b, N_DIR, S5_BLOCKS, S5_BLOCK_STATE, 2).transpose(0, 2, 1, 4, 3)
    return s.reshape(b, S5_BLOCKS, 2 * N_DIR, S5_BLOCK_STATE)


def _s5_state_from_blocks(s):
    b = s.shape[0]
    s = s.reshape(b, S5_BLOCKS, N_DIR, 2, S5_BLOCK_STATE).transpose(0, 2, 1, 4, 3)
    return s.reshape(b, N_DIR, C_GROUPS, C_STATE, 2)


ODD_IN_PAD = _round_up(sum(ODD_SPLITS), LANES)
_O_U, _O_Q, _O_K, _O_V, _O_G, _O_A = (int(v) for v in np.cumsum((0,) + ODD_SPLITS[:-1]))
GLA_QK = D_HEADS * D_DK
GLA_V = D_HEADS * D_DV
GLA_MIN_SUPER = 256
GLA_REF_ROWS = 128
GLA_SUBLANES = 8


def _split3(x):
    x1 = x.astype(BF16)
    r1 = x - x1.astype(F32)
    x2 = r1.astype(BF16)
    x3 = (r1 - x2.astype(F32)).astype(BF16)
    return x1, x2, x3


def _dot_01(m01, x):
    p1, p2, p3 = _split3(x)
    dot = lambda p: jnp.dot(m01, p, preferred_element_type=F32)
    return dot(p1) + dot(p2) + dot(p3)


def _dot_nt(a, b):
    return lax.dot_general(a, b, (((1,), (1,)), ((), ())), preferred_element_type=F32)


def _one_hot(cond):
    return jnp.where(cond, 1.0, 0.0).astype(BF16)


def _gla_kernel(seq_len, has_state, heads, u_ref, q_ref, k_ref, v_ref, g_ref, a_ref, ypre_ref, dskip_ref, wglu_ref,
                wg2_ref, bgate_ref, gain_ref, *rest):
    if has_state:
        s0_ref, outc_ref, outd_ref, p_s = rest
    else:
        outc_ref, outd_ref, fin_ref, p_s = rest
    n = seq_len

    @pl.when(pl.program_id(1) == 0)
    def _():
        y = jax.nn.gelu(ypre_ref[...] + dskip_ref[...] * u_ref[...])
        glu = jnp.dot(y.astype(BF16), wglu_ref[...], preferred_element_type=F32)
        outc_ref[...] = y * jax.nn.sigmoid(glu)

    q = q_ref[...] * (D_DK ** -0.5)
    k = k_ref[...]
    r_i = lax.broadcasted_iota(jnp.int32, (n, n), 0)
    c_i = lax.broadcasted_iota(jnp.int32, (n, n), 1)
    b = []
    for d in range(N_DIR):
        logits = jnp.dot(a_ref[...], wg2_ref[d], preferred_element_type=F32,
                         precision=lax.Precision.HIGHEST) + bgate_ref[d]
        la = (jnp.minimum(logits, 0.0) - jnp.log(1.0 + jnp.exp(-jnp.abs(logits)))) / D_TAU
        b.append(_dot_01(_one_hot((r_i >= c_i) if d == 0 else (r_i <= c_i)), la))

    lane_p = lax.broadcasted_iota(jnp.int32, (1, LANES), 1)

    def pair_lanes(x, h):
        return x[:, (h // 2) * LANES:(h // 2 + 1) * LANES]

    def head_only(x, h):
        mine = (lane_p < D_DK) if h % 2 == 0 else (lane_p >= D_DK)
        return jnp.where(mine, pair_lanes(x, h), 0.0).astype(BF16)

    for h in range(heads):
        s = _dot_nt(head_only(q, h), pair_lanes(k, h).astype(BF16))
        p_s[h] = jnp.where(r_i == c_i, 2.0 * s, 0.0)

    for lb in range(int(math.log2(n))):
        half = 1 << lb
        width = min(max(2 * half, GLA_MIN_SUPER), n)
        w_r = lax.broadcasted_iota(jnp.int32, (width, width), 0)
        w_c = lax.broadcasted_iota(jnp.int32, (width, width), 1)
        at_level = ((w_r ^ w_c) >> lb) == 1
        later = w_r > w_c
        qt, kt = [], []
        for d in range(N_DIR):
            edge = half - 1 if d == 0 else half
            if 2 * half >= GLA_SUBLANES:
                blocks = b[d].reshape(n // (2 * half), 2 * half, b[d].shape[1])
                ref = jnp.broadcast_to(blocks[:, edge:edge + 1, :], blocks.shape).reshape(b[d].shape)
            elif n <= GLA_MIN_SUPER:
                sel = c_i == ((r_i >> (lb + 1)) << (lb + 1)) + edge
                ref = _dot_01(_one_hot(sel), b[d])
            else:
                n_ref = max(n >> (lb + 1), GLA_REF_ROWS)
                blk = lax.broadcasted_iota(jnp.int32, (n_ref, n), 0)
                tok = lax.broadcasted_iota(jnp.int32, (n_ref, n), 1)
                per_block = _dot_01(_one_hot(tok == (blk << (lb + 1)) + edge), b[d])
                tok_r = lax.broadcasted_iota(jnp.int32, (n, n_ref), 0)
                blk_c = lax.broadcasted_iota(jnp.int32, (n, n_ref), 1)
                ref = _dot_01(_one_hot(blk_c == (tok_r >> (lb + 1))), per_block)
            qt.append(q * jnp.exp(jnp.minimum(b[d] - ref, 0.0)))
            kt.append(k * jnp.exp(jnp.minimum(ref - b[d], 0.0)))
        for sb in range(n // width):
            rows = slice(sb * width, (sb + 1) * width)
            for h in range(heads):
                s_f = _dot_nt(head_only(qt[0][rows], h), pair_lanes(kt[0][rows], h).astype(BF16))
                s_b = _dot_nt(head_only(qt[1][rows], h), pair_lanes(kt[1][rows], h).astype(BF16))
                p_s[h, rows, rows] = p_s[h, rows, rows] + jnp.where(at_level, jnp.where(later, s_f, s_b), 0.0)

    zeros = jnp.zeros((D_DK, D_DV), F32)
    for h in range(heads):
        vh = v_ref[:, h * D_DV:(h + 1) * D_DV].astype(BF16)
        o = jnp.dot(p_s[h].astype(BF16), vh, preferred_element_type=F32)
        if has_state:
            for d in range(N_DIR):
                qs = pair_lanes(q * jnp.exp(b[d]), h).astype(BF16)
                s0 = s0_ref[d, h]
                s0_pad = jnp.concatenate([s0, zeros] if h % 2 == 0 else [zeros, s0], axis=0).astype(BF16)
                o = o + jnp.dot(qs, s0_pad, preferred_element_type=F32)
        o = o * lax.rsqrt(jnp.mean(o * o, axis=-1, keepdims=True) + RMS_EPS) * gain_ref[...]
        gate = g_ref[:, h * D_DV:(h + 1) * D_DV]
        outd_ref[:, h * D_DV:(h + 1) * D_DV] = o * (gate * jax.nn.sigmoid(gate))

    if not has_state:
        for d in range(N_DIR):
            last = n - 1 if d == 0 else 0
            k_out_t = (k * jnp.exp(b[d][last:last + 1] - b[d])).T.astype(BF16)
            for h in range(heads):
                vh = v_ref[:, h * D_DV:(h + 1) * D_DV].astype(BF16)
                fin_ref[d, h] = jnp.dot(k_out_t[h * D_DK:(h + 1) * D_DK], vh, preferred_element_type=F32)


def _gla_mixer(grp, proj, ypre, d_skip, w_glu, w_gate2, b_gate, gain, s0):
    seq_len = grp.seq_len
    has_state = s0 is not None
    heads = D_HEADS if seq_len <= GLA_MIN_SUPER else 2
    groups = D_HEADS // heads
    n_qk, n_v = heads * D_DK, heads * D_DV
    wg2 = jnp.zeros((N_DIR, LANES, GLA_QK), F32)
    for d in range(N_DIR):
        wg2 = wg2.at[d, d * D_RANK:(d + 1) * D_RANK].set(w_gate2[d])
    full = lambda *shape: pl.BlockSpec(shape, lambda b, g: (0,) * len(shape))
    cols = lambda width, offset: pl.BlockSpec((seq_len, width), lambda b, g: (b, offset // width + g))
    fixed = lambda width, offset: pl.BlockSpec((seq_len, width), lambda b, g: (b, offset // width))
    state_spec = pl.BlockSpec((None, N_DIR, heads, D_DK, D_DV), lambda b, g: (b, 0, g, 0, 0))
    in_specs = [fixed(C_WIDTH, _O_U), cols(n_qk, _O_Q), cols(n_qk, _O_K), cols(n_v, _O_V), cols(n_v, _O_G),
                fixed(LANES, _O_A), fixed(C_WIDTH, 0),
                full(1, C_WIDTH), full(C_WIDTH, C_WIDTH),
                pl.BlockSpec((N_DIR, LANES, n_qk), lambda b, g: (0, 0, g)),
                pl.BlockSpec((N_DIR, 1, n_qk), lambda b, g: (0, 0, g)),
                full(1, D_DV)]
    args = [proj] * 6 + [ypre, d_skip[None], w_glu.astype(BF16), wg2, b_gate.reshape(N_DIR, 1, GLA_QK), gain[None]]
    n_tok = grp.n_seq * seq_len
    out_specs = [pl.BlockSpec((seq_len, C_WIDTH), lambda b, g: (b, 0)),
                 pl.BlockSpec((seq_len, n_v), lambda b, g: (b, g))]
    out_shape = [jax.ShapeDtypeStruct((n_tok, C_WIDTH), F32), jax.ShapeDtypeStruct((n_tok, GLA_V), F32)]
    if has_state:
        in_specs.append(state_spec)
        args.append(s0)
    else:
        out_specs.append(state_spec)
        out_shape.append(jax.ShapeDtypeStruct((grp.n_seq, N_DIR, D_HEADS, D_DK, D_DV), F32))
    out = pl.pallas_call(
        functools.partial(_gla_kernel, seq_len, has_state, heads),
        grid=(grp.n_seq, groups),
        in_specs=in_specs, out_specs=out_specs, out_shape=out_shape,
        scratch_shapes=[pltpu.VMEM((heads, seq_len, seq_len), F32)],
        compiler_params=_params(2),
        name="gla_mixer",
    )(*args)
    return (out[0], out[1], None) if has_state else tuple(out)


EV_A = A_HEADS * A_DK
EV_B = B_HEADS * B_DK
EV_HALF = 4 * EV_A
EVEN_IN_PAD = 2 * EV_HALF + LANES
GDN_PAIR = 2 * A_CHUNK
GDN_LEVELS = 5
GDN_SPLIT_LEVELS = 3


def _regroup_even_cols(w):
    small = 2 * N_DIR * A_HEADS
    pad = jnp.zeros((w.shape[0], EVEN_IN_PAD - 2 * EV_HALF - small), w.dtype)
    return jnp.concatenate([w[:, :EV_HALF], w[:, EV_HALF + small:], w[:, EV_HALF:EV_HALF + small], pad], axis=1)


def _block_diag2(x):
    a, b = x[:, :GDN_PAIR], x[:, GDN_PAIR:]
    z = jnp.zeros_like(a)
    return jnp.concatenate([jnp.concatenate([a, z], axis=1), jnp.concatenate([z, b], axis=1)], axis=0)


def _pair_rhs(m, split):
    hi = m.astype(BF16)
    lo = (m - hi.astype(F32)).astype(BF16) if split else None
    return _block_diag2(hi), (_block_diag2(lo) if split else None)


def _pair_mm(x, rhs, split):
    r_hi, r_lo = rhs
    dot = lambda a, b: jnp.dot(a, b, preferred_element_type=F32)
    x_hi = x.astype(BF16)
    if not split:
        return dot(x_hi, r_hi)
    x_lo = (x - x_hi.astype(F32)).astype(BF16)
    return dot(x_hi, r_hi) + (dot(x_hi, r_lo) + dot(x_lo, r_hi))


def _softplus(x):
    return jnp.maximum(x, 0.0) + jnp.log(1.0 + jnp.exp(-jnp.abs(x)))


def _silu(x):
    return x * jax.nn.sigmoid(x)


def _gdn_kernel(seq_len, has_state, proj_ref, small_ref, convw_ref, alog_ref, dtb_ref, gain_ref, *rest):
    if has_state:
        s0_ref, out_ref, q_s, k_s, v_s, gb_s, gcum_s, o_s, st_s = rest
    else:
        out_ref, fin_ref, q_s, k_s, v_s, gb_s, gcum_s, o_s, st_s = rest
    n_pairs = seq_len // GDN_PAIR
    width = 3 * EV_A

    x = proj_ref[:, 0:width]
    t_idx = lax.broadcasted_iota(jnp.int32, (seq_len, 1), 0)
    pad = A_CONV // 2
    acc = x * convw_ref[pad:pad + 1, :]
    for s in range(-pad, pad + 1):
        if s == 0:
            continue
        shifted = pltpu.roll(x, (-s) % seq_len, 0)
        inside = (t_idx + s >= 0) & (t_idx + s < seq_len)
        acc = acc + jnp.where(inside, shifted, 0.0) * convw_ref[pad + s:pad + s + 1, :]
    y = _silu(acc)
    for h in range(A_HEADS):
        cols = slice(h * A_DK, (h + 1) * A_DK)
        qh = y[:, h * A_DK:(h + 1) * A_DK]
        kh = y[:, EV_A + h * A_DK:EV_A + (h + 1) * A_DK]
        q_s[:, cols] = qh * lax.rsqrt(jnp.sum(qh * qh, axis=-1, keepdims=True) + 1e-6) * (A_DK ** -0.5)
        k_s[:, cols] = kh * lax.rsqrt(jnp.sum(kh * kh, axis=-1, keepdims=True) + 1e-6)
    v_s[...] = y[:, 2 * EV_A:3 * EV_A]

    small = small_ref[...]
    lane = lax.broadcasted_iota(jnp.int32, small.shape, 1)
    gb = jnp.where(lane < N_DIR * A_HEADS, jax.nn.sigmoid(small),
                   -jnp.exp(alog_ref[...]) * _softplus(small + dtb_ref[...]))
    gb_s[...] = gb
    r_i = lax.broadcasted_iota(jnp.int32, (seq_len, seq_len), 0)
    c_i = lax.broadcasted_iota(jnp.int32, (seq_len, seq_len), 1)
    same = (r_i // A_CHUNK) == (c_i // A_CHUNK)
    gcum_s[0] = _dot_01(jnp.where(same & (r_i >= c_i), 1.0, 0.0).astype(BF16), gb)
    gcum_s[1] = _dot_01(jnp.where(same & (r_i <= c_i), 1.0, 0.0).astype(BF16), gb)

    o_s[...] = jnp.zeros(o_s.shape, F32)
    for d in range(N_DIR):
        for h in range(A_HEADS):
            st_s[d, h] = s0_ref[d, h] if has_state else jnp.zeros((A_DK, A_DV), F32)

    pr = lax.broadcasted_iota(jnp.int32, (GDN_PAIR, GDN_PAIR), 0)
    pc = lax.broadcasted_iota(jnp.int32, (GDN_PAIR, GDN_PAIR), 1)
    p_same = (pr // A_CHUNK) == (pc // A_CHUNK)
    eye = jnp.where(pr == pc, 1.0, 0.0)
    row_p = lax.broadcasted_iota(jnp.int32, (GDN_PAIR, 1), 0)
    zeros_c = jnp.zeros((A_CHUNK, A_DV), F32)

    def pair(p, _):
        bodies = []
        for d in range(N_DIR):
            base = pl.multiple_of((p if d == 0 else n_pairs - 1 - p) * GDN_PAIR, GDN_PAIR)
            rows = pl.ds(base, GDN_PAIR)
            gbp = gb_s[rows, :]
            gp = gcum_s[d, rows, :]
            gp_t = gp.T
            tri = (pr >= pc) if d == 0 else (pr <= pc)
            strict = (pr > pc) if d == 0 else (pr < pc)
            for h in range(A_HEADS):
                cols = slice(h * A_DK, (h + 1) * A_DK)
                bi, gi = d * A_HEADS + h, N_DIR * A_HEADS + d * A_HEADS + h
                qp, kp, vp = q_s[rows, cols], k_s[rows, cols], v_s[rows, cols]
                beta = gbp[:, bi:bi + 1]
                g_col = gp[:, gi:gi + 1]
                g_row = gp_t[gi:gi + 1, :]
                decay = jnp.where(p_same & tri, jnp.exp(jnp.minimum(g_col - g_row, 0.0)), 0.0)
                kb = kp * beta
                kp16 = kp.astype(BF16)
                m = jnp.where(p_same & strict, -(_dot_nt(kb.astype(BF16), kp16) * decay), 0.0)
                e_g = jnp.exp(g_col)
                last = A_CHUNK - 1 if d == 0 else 0
                g_last = [g_col[c * A_CHUNK + last:c * A_CHUNK + last + 1] for c in range(2)]
                gl_col = jnp.where(row_p < A_CHUNK, g_last[0], g_last[1])
                bodies.append(dict(
                    d=d, h=h, base=base, cols=cols, m=m, g_last=g_last,
                    rhs=jnp.concatenate([vp * beta, kb * e_g], axis=1).astype(BF16),
                    attn=(_dot_nt(qp.astype(BF16), kp16) * decay).astype(BF16),
                    q_in=(qp * e_g).astype(BF16),
                    k_out_t=(kp * jnp.exp(gl_col - g_col)).T.astype(BF16)))

        ms = [jnp.concatenate([bodies[i]["m"], bodies[i + 1]["m"]], axis=1) for i in range(0, len(bodies), 2)]
        ts = [jnp.concatenate([eye, eye], axis=1) + m for m in ms]
        for level in range(GDN_LEVELS):
            split = level < GDN_SPLIT_LEVELS
            rhs = [_pair_rhs(m, split) for m in ms]
            ms = [_pair_mm(m, r, split) for m, r in zip(ms, rhs)]
            rhs = [_pair_rhs(m, split) for m in ms]
            ts = [t + _pair_mm(t, r, split) for t, r in zip(ts, rhs)]
        t_invs = [t[:, i * GDN_PAIR:(i + 1) * GDN_PAIR] for t in ts for i in range(2)]

        for body, t_inv in zip(bodies, t_invs):
            d, h, base, cols, g_last = body["d"], body["h"], body["base"], body["cols"], body["g_last"]
            uw = jnp.dot(t_inv.astype(BF16), body["rhs"], preferred_element_type=F32)
            u, w = uw[:, :A_DV], uw[:, A_DV:]
            s = st_s[d, h]
            for c in ((0, 1) if d == 0 else (1, 0)):
                rs = slice(c * A_CHUNK, (c + 1) * A_CHUNK)
                s16 = s.astype(BF16)
                v_new = u[rs] - jnp.dot(w[rs].astype(BF16), s16, preferred_element_type=F32)
                v_full = jnp.concatenate([v_new, zeros_c] if c == 0 else [zeros_c, v_new], axis=0).astype(BF16)
                o_c = (jnp.dot(body["q_in"][rs], s16, preferred_element_type=F32)
                       + jnp.dot(body["attn"][rs], v_full, preferred_element_type=F32))
                s = s * jnp.exp(g_last[c]) + jnp.dot(body["k_out_t"], v_full, preferred_element_type=F32)
                rows_c = pl.ds(base + c * A_CHUNK, A_CHUNK)
                o_s[rows_c, cols] = o_s[rows_c, cols] + o_c
            st_s[d, h] = s
        return 0

    lax.fori_loop(0, n_pairs, pair, 0)

    for h in range(A_HEADS):
        cols = slice(h * A_DV, (h + 1) * A_DV)
        o = o_s[:, cols]
        o = o * lax.rsqrt(jnp.mean(o * o, axis=-1, keepdims=True) + RMS_EPS) * gain_ref[...]
        out_ref[:, cols] = o * _silu(proj_ref[:, 3 * EV_A + h * A_DV:3 * EV_A + (h + 1) * A_DV])
    if not has_state:
        for d in range(N_DIR):
            for h in range(A_HEADS):
                fin_ref[d, h] = st_s[d, h]


def _gdn_mixer(grp, proj, conv_w, a_log, dt_bias, gain, s0):
    seq_len = grp.seq_len
    has_state = s0 is not None
    n_small = N_DIR * A_HEADS
    lane_row = lambda v: jnp.zeros((1, LANES), F32).at[0, n_small:2 * n_small].set(v.reshape(n_small))
    convw = jnp.zeros((8, 3 * EV_A), F32).at[:A_CONV].set(conv_w)
    full = lambda *shape: pl.BlockSpec(shape, lambda b: (0,) * len(shape))
    state_spec = pl.BlockSpec((None, N_DIR, A_HEADS, A_DK, A_DV), lambda b: (b, 0, 0, 0, 0))
    in_specs = [pl.BlockSpec((seq_len, EV_HALF), lambda b: (b, 0)),
                pl.BlockSpec((seq_len, LANES), lambda b: (b, 2 * EV_HALF // LANES)),
                full(8, 3 * EV_A), full(1, LANES), full(1, LANES), full(1, A_DV)]
    args = [proj, proj, convw, lane_row(a_log), lane_row(dt_bias), gain[None]]
    out_spec = pl.BlockSpec((seq_len, EV_A), lambda b: (b, 0))
    out_shape = jax.ShapeDtypeStruct((grp.n_seq * seq_len, EV_A), F32)
    if has_state:
        in_specs.append(state_spec)
        args.append(s0)
        out_specs, out_shapes = out_spec, out_shape
    else:
        out_specs = [out_spec, state_spec]
        out_shapes = [out_shape, jax.ShapeDtypeStruct((grp.n_seq, N_DIR, A_HEADS, A_DK, A_DV), F32)]
    tok = lambda w: pltpu.VMEM((seq_len, w), F32)
    out = pl.pallas_call(
        functools.partial(_gdn_kernel, seq_len, has_state),
        grid=(grp.n_seq,),
        in_specs=in_specs, out_specs=out_specs, out_shape=out_shapes,
        scratch_shapes=[tok(EV_A), tok(EV_A), tok(EV_A), tok(LANES), pltpu.VMEM((N_DIR, seq_len, LANES), F32),
                        tok(EV_A), pltpu.VMEM((N_DIR, A_HEADS, A_DK, A_DV), F32)],
        compiler_params=_params(1),
        name="gdn_mixer",
    )(*args)
    return (out, None) if has_state else (out[0], out[1])


def _rope_tables(n_tokens):
    lane = np.arange(EV_B)
    axis = (lane % B_DK) // (2 * ROPE_FREQS)
    half = (lane % (2 * ROPE_FREQS)) // ROPE_FREQS
    freq = ROPE_BASE ** (-(lane % ROPE_FREQS).astype(np.float32) / ROPE_FREQS)
    tok = jnp.arange(n_tokens, dtype=F32)
    pos = jnp.where(jnp.asarray(axis)[None, :] == 0, jnp.floor(tok / GRID_W)[:, None], (tok % GRID_W)[:, None])
    ang = pos * jnp.asarray(freq, F32)[None, :]
    sign = jnp.asarray(np.where(half == 0, -1.0, 1.0), F32)[None, :]
    return jnp.cos(ang), jnp.sin(ang) * sign


def _dot3_lhs(x, m):
    p1, p2, p3 = _split3(x)
    dot = lambda p: jnp.dot(p, m, preferred_element_type=F32)
    return dot(p1) + dot(p2) + dot(p3)


def _ret_kernel(seq_len, has_state, rope, proj_ref, lgd_ref, gain_ref, *rest):
    rest = list(rest)
    cos_ref, sin_ref = (rest.pop(0), rest.pop(0)) if rope else (None, None)
    s0_ref = rest.pop(0) if has_state else None
    out_ref = rest.pop(0)
    fin_ref = None if has_state else rest.pop(0)

    q = proj_ref[:, 0:EV_B]
    k = proj_ref[:, EV_B:2 * EV_B]
    if rope:
        lane = lax.broadcasted_iota(jnp.int32, (seq_len, EV_B), 1)
        first = (lane % (2 * ROPE_FREQS)) < ROPE_FREQS

        def rotate(x):
            partner = jnp.where(first, pltpu.roll(x, EV_B - ROPE_FREQS, 1), pltpu.roll(x, ROPE_FREQS, 1))
            return x * cos_ref[...] + partner * sin_ref[...]

        q, k = rotate(q), rotate(k)
    q = q * (B_DK ** -0.5)
    log_gamma = -jnp.exp(lgd_ref[...])
    i_col = lax.broadcasted_iota(jnp.int32, (seq_len, 1), 0).astype(F32)
    dist = (lax.broadcasted_iota(jnp.int32, (seq_len, seq_len), 0)
            - lax.broadcasted_iota(jnp.int32, (seq_len, seq_len), 1)).astype(F32)
    lane_p = lax.broadcasted_iota(jnp.int32, (1, LANES), 1)
    h_avg = jnp.where((lax.broadcasted_iota(jnp.int32, (LANES, LANES), 0) // B_DV)
                      == (lax.broadcasted_iota(jnp.int32, (LANES, LANES), 1) // B_DV), 1.0 / B_DV, 0.0).astype(BF16)

    for pair in range(B_HEADS // 2):
        cols = slice(pair * LANES, (pair + 1) * LANES)
        qp, kp = q[:, cols], k[:, cols]
        vp = proj_ref[:, 2 * EV_B + pair * LANES:2 * EV_B + (pair + 1) * LANES]
        kp16, vp16 = kp.astype(BF16), vp.astype(BF16)
        o = jnp.zeros((seq_len, LANES), F32)
        lg = [[log_gamma[d:d + 1, 2 * pair + e:2 * pair + e + 1] for e in range(2)] for d in range(N_DIR)]
        for e in range(2):
            mine = (lane_p < B_DK) if e == 0 else (lane_p >= B_DK)
            scores = _dot_nt(jnp.where(mine, qp, 0.0).astype(BF16), kp16)
            weight = (jnp.where(dist >= 0.0, jnp.exp(jnp.maximum(dist, 0.0) * lg[0][e]), 0.0)
                      + jnp.where(dist <= 0.0, jnp.exp(jnp.maximum(-dist, 0.0) * lg[1][e]), 0.0))
            o = o + jnp.dot((scores * weight).astype(BF16), jnp.where(mine, vp, 0.0).astype(BF16),
                            preferred_element_type=F32)
        if has_state:
            for d in range(N_DIR):
                steps = (i_col + 1.0) if d == 0 else (seq_len - i_col)
                xi = jnp.where(lane_p < B_DK, jnp.exp(steps * lg[d][0]), jnp.exp(steps * lg[d][1]))
                zeros = jnp.zeros((B_DK, B_DV), F32)
                s_pair = jnp.concatenate(
                    [jnp.concatenate([s0_ref[d, 2 * pair], zeros], axis=1),
                     jnp.concatenate([zeros, s0_ref[d, 2 * pair + 1]], axis=1)], axis=0)
                o = o + jnp.dot((qp * xi).astype(BF16), s_pair.astype(BF16), preferred_element_type=F32)
        else:
            for d in range(N_DIR):
                steps = (seq_len - 1.0 - i_col) if d == 0 else i_col
                zeta = jnp.where(lane_p < B_DK, jnp.exp(steps * lg[d][0]), jnp.exp(steps * lg[d][1]))
                kz_t = (kp * zeta).T.astype(BF16)
                both = jnp.dot(kz_t, vp16, preferred_element_type=F32)
                fin_ref[d, 2 * pair] = both[:B_DK, :B_DV]
                fin_ref[d, 2 * pair + 1] = both[B_DK:, B_DV:]
        mu = _dot3_lhs(o, h_avg)
        cen = o - mu
        var = _dot3_lhs(cen * cen, h_avg)
        normed = cen * lax.rsqrt(var + RMS_EPS) * gain_ref[:, cols]
        out_ref[:, cols] = normed * _silu(proj_ref[:, 3 * EV_B + pair * LANES:3 * EV_B + (pair + 1) * LANES])


def _ret_mixer(grp, proj, log_decay, gain, s0, rope_tables):
    seq_len = grp.seq_len
    has_state = s0 is not None
    rope = rope_tables is not None
    lgd = jnp.zeros((8, LANES), F32).at[:N_DIR, :B_HEADS].set(log_decay)
    full = lambda *shape: pl.BlockSpec(shape, lambda b: (0,) * len(shape))
    state_spec = pl.BlockSpec((None, N_DIR, B_HEADS, B_DK, B_DV), lambda b: (b, 0, 0, 0, 0))
    in_specs = [pl.BlockSpec((seq_len, EV_HALF), lambda b: (b, 1)), full(8, LANES), full(1, EV_B)]
    args = [proj, lgd, gain[None]]
    if rope:
        in_specs += [full(seq_len, EV_B), full(seq_len, EV_B)]
        args += list(rope_tables)
    out_spec = pl.BlockSpec((seq_len, EV_B), lambda b: (b, 0))
    out_shape = jax.ShapeDtypeStruct((grp.n_seq * seq_len, EV_B), F32)
    if has_state:
        in_specs.append(state_spec)
        args.append(s0)
        out_specs, out_shapes = out_spec, out_shape
    else:
        out_specs = [out_spec, state_spec]
        out_shapes = [out_shape, jax.ShapeDtypeStruct((grp.n_seq, N_DIR, B_HEADS, B_DK, B_DV), F32)]
    out = pl.pallas_call(
        functools.partial(_ret_kernel, seq_len, has_state, rope),
        grid=(grp.n_seq,),
        in_specs=in_specs, out_specs=out_specs, out_shape=out_shapes,
        compiler_params=_params(1),
        name="retention_mixer",
    )(*args)
    return (out, None) if has_state else (out[0], out[1])


def _pad_cols(w, n):
    return jnp.pad(w, ((0, 0), (0, n - w.shape[1])))


def kernel(x_prompt, x_sample, state_gdn, state_ret, state_s5, state_gla, c, c_ctx, ada_w, ada_b, norm_mix,
           norm_ffn, norm_final, ev_w_in, ev_conv, gdn_a_log, gdn_dt_bias, gdn_gain, ret_log_decay, ret_gain,
           ev_w_out, od_w_in, s5_a_re, s5_a_im, s5_log_dt, s5_b_re, s5_b_im, s5_c_re, s5_c_im, s5_d, s5_w_glu,
           gla_w_gate2, gla_b_gate, gla_gain, od_w_out, moe_router, moe_w_gate, moe_w_up, moe_w_down):
    cond = jnp.zeros((COND_ROWS, D_MODEL), F32).at[0].set(c_ctx).at[1:N_COND].set(c)
    mods = _modulation(cond, ada_w, ada_b)
    rope_tables = _rope_tables(DEC_SEQ)

    xs = {CTX: x_prompt.reshape(BATCH * SEQ, D_MODEL), DEC: x_sample.reshape(DEC_BATCH * DEC_SEQ, D_MODEL)}
    new_states = {"gdn": [], "ret": [], "s5": [], "gla": []}

    for layer in range(DEPTH):
        j = layer // 2
        even = layer % 2 == 0
        if even:
            w_in = _regroup_even_cols(ev_w_in[j]).astype(BF16)
            w_out = ev_w_out[j].astype(BF16)
            w_outs = [w_out[:EV_A], w_out[EV_A:]]
        else:
            w_in = _pad_cols(od_w_in[j], ODD_IN_PAD).astype(BF16)
            w_out = od_w_out[j].astype(BF16)
            w_outs = [w_out[:C_WIDTH], w_out[C_WIDTH:]]
            s5_prep = _s5_prepare(s5_a_re[j], s5_a_im[j], s5_log_dt[j], s5_b_re[j], s5_b_im[j], s5_c_re[j],
                                  s5_c_im[j])
        w_router = _pad_cols(moe_router[layer], LANES)
        routed = {}
        for grp in (CTX, DEC):
            x = xs[grp]
            ctx = grp.is_context
            proj = _in_projection(grp, layer, x, mods, norm_mix[layer][None], w_in)
            if even:
                out_a, fin_a = _gdn_mixer(grp, proj, ev_conv[j], gdn_a_log[j], gdn_dt_bias[j], gdn_gain[j],
                                          None if ctx else state_gdn[:, j])
                out_b, fin_b = _ret_mixer(grp, proj, ret_log_decay[j], ret_gain[j],
                                          None if ctx else state_ret[:, j], None if ctx else rope_tables)
                mixes = [out_a, out_b]
                if ctx:
                    new_states["gdn"].append(fin_a)
                    new_states["ret"].append(fin_b)
            else:
                ypre, fin_c = _s5_mixer(grp, proj, s5_prep, None if ctx else _s5_state_to_blocks(state_s5[:, j]))
                out_c, out_d, fin_d = _gla_mixer(grp, proj, ypre, s5_d[j], s5_w_glu[j], gla_w_gate2[j],
                                                 gla_b_gate[j], gla_gain[j], None if ctx else state_gla[:, j])
                mixes = [out_c, out_d]
                if ctx:
                    new_states["s5"].append(_s5_state_from_blocks(fin_c))
                    new_states["gla"].append(fin_d)
            x1, h2, aff = _out_projection(grp, layer, mixes, x, mods, w_outs, norm_ffn[layer][None], w_router)
            xs_g, pt, gate = _route(grp, aff, h2)
            routed[grp] = (x1, xs_g, pt, gate)
        ys_c, ys_d = _experts(layer, routed[CTX][1], routed[DEC][1], routed[CTX][3], routed[DEC][3],
                              moe_w_gate, moe_w_up, moe_w_down)
        for grp, ys in ((CTX, ys_c), (DEC, ys_d)):
            x1, _, pt, _ = routed[grp]
            xs[grp] = _combine(grp, layer, layer == DEPTH - 1, x1, mods, pt, ys, norm_final[None])

    y_prompt = xs[CTX].reshape(BATCH, SEQ, D_MODEL)
    y_sample = xs[DEC].reshape(DEC_BATCH, DEC_SEQ, D_MODEL)
    return (y_prompt, y_sample, jnp.stack(new_states["gdn"], axis=1), jnp.stack(new_states["ret"], axis=1),
            jnp.stack(new_states["s5"], axis=1), jnp.stack(new_states["gla"], axis=1))
```

```python
import functools
import math
from typing import NamedTuple

import numpy as np
import jax
import jax.numpy as jnp
from jax import lax
from jax.experimental import pallas as pl
from jax.experimental.pallas import tpu as pltpu

F32 = jnp.float32
BF16 = jnp.bfloat16

D_MODEL = 1024
BATCH = 32
SEQ = 256
DEPTH = 4
DEC_BATCH = 2
DEC_SEQ = 1024
GRID_W = 64
N_DIR = 2
A_HEADS, A_DK, A_DV, A_CONV, A_CHUNK = 4, 128, 128, 5, 64
B_HEADS, B_DK, B_DV, B_CHUNK = 8, 64, 64, 64
ROPE_FREQS = B_DK // 4
ROPE_BASE = 10000.0
C_GROUP, C_GROUPS, C_STATE = 16, 32, 64
C_WIDTH = C_GROUPS * C_GROUP
S5_MAX_RE = -1e-4
D_HEADS, D_DK, D_DV, D_RANK, D_TAU, D_CHUNK = 4, 64, 128, 16, 16.0, 16
N_EXPERTS = 16
EXPERT_FF = 1024
EC_CAPACITY_FACTOR = 2
RMS_EPS = 1e-6

EVEN_SPLITS = (A_HEADS * A_DK, A_HEADS * A_DK, A_HEADS * A_DV, A_HEADS * A_DV, N_DIR * A_HEADS, N_DIR * A_HEADS,
               B_HEADS * B_DK, B_HEADS * B_DK, B_HEADS * B_DV, B_HEADS * B_DV)
ODD_SPLITS = (C_WIDTH, D_HEADS * D_DK, D_HEADS * D_DK, D_HEADS * D_DV, D_HEADS * D_DV, N_DIR * D_RANK)

LANES = 128
TOKEN_TILE = 256
N_COND = 1 + DEC_BATCH
COND_ROWS = 8
MOD_COL_TILE = 1536
VMEM_LIMIT = 56 * 1024 * 1024


class Group(NamedTuple):
    n_seq: int
    seq_len: int
    is_context: bool

    @property
    def tiles_per_seq(self):
        return self.seq_len // TOKEN_TILE

    @property
    def n_tiles(self):
        return self.n_seq * self.tiles_per_seq

    @property
    def capacity(self):
        return EC_CAPACITY_FACTOR * self.seq_len // N_EXPERTS

    def cond_of_tile(self, i):
        return 0 if self.is_context else 1 + i // self.tiles_per_seq

    def cond_of_seq(self, b):
        return 0 if self.is_context else 1 + b


CTX = Group(BATCH, SEQ, True)
DEC = Group(DEC_BATCH, DEC_SEQ, False)


def _round_up(n, m):
    return (n + m - 1) // m * m


def _params(n_axes):
    return pltpu.CompilerParams(dimension_semantics=("arbitrary",) * n_axes, vmem_limit_bytes=VMEM_LIMIT)


def _mm3(a, b):
    a1 = a.astype(BF16)
    a2 = (a - a1.astype(F32)).astype(BF16)
    b1 = b.astype(BF16)
    b2 = (b - b1.astype(F32)).astype(BF16)
    dot = lambda x, y: jnp.dot(x, y, preferred_element_type=F32)
    return dot(a1, b1) + (dot(a1, b2) + dot(a2, b1))


def _mod_kernel(c_ref, w_ref, b_ref, o_ref):
    c = c_ref[...]
    s = c * jax.nn.sigmoid(c)
    o_ref[0] = jnp.dot(s.astype(BF16), w_ref[0].astype(BF16), preferred_element_type=F32) + b_ref[0]


def _modulation(cond, ada_w, ada_b):
    n_out = 6 * D_MODEL
    out = pl.pallas_call(
        _mod_kernel,
        grid=(DEPTH, n_out // MOD_COL_TILE),
        in_specs=[pl.BlockSpec((COND_ROWS, D_MODEL), lambda l, j: (0, 0)),
                  pl.BlockSpec((1, D_MODEL, MOD_COL_TILE), lambda l, j: (l, 0, j)),
                  pl.BlockSpec((1, 1, MOD_COL_TILE), lambda l, j: (l, 0, j))],
        out_specs=pl.BlockSpec((1, COND_ROWS, MOD_COL_TILE), lambda l, j: (l, 0, j)),
        out_shape=jax.ShapeDtypeStruct((DEPTH, COND_ROWS, n_out), F32),
        compiler_params=_params(2),
        name="adaln_modulation",
    )(cond, ada_w, ada_b.reshape(DEPTH, 1, n_out))
    out = out[:, :N_COND].reshape(DEPTH, N_COND, 6, D_MODEL)
    return out.transpose(1, 0, 2, 3).reshape(N_COND, DEPTH * 6, D_MODEL)


def _norm_mod(x, gain, shift, scale):
    y = x * lax.rsqrt(jnp.mean(x * x, axis=-1, keepdims=True) + RMS_EPS)
    return (y * gain) * (1.0 + scale) + shift


def _mod_row(mod_ref, layer, k):
    r = 6 * layer + k
    return mod_ref[r:r + 1, :]


def _inproj_kernel(layer, x_ref, mod_ref, gain_ref, w_ref, o_ref):
    h = _norm_mod(x_ref[...], gain_ref[...], _mod_row(mod_ref, layer, 0), _mod_row(mod_ref, layer, 1))
    o_ref[...] = jnp.dot(h.astype(BF16), w_ref[...], preferred_element_type=F32)


def _in_projection(grp, layer, x, mods, gain, w):
    n_tok = grp.n_seq * grp.seq_len
    n_out = w.shape[1]
    return pl.pallas_call(
        functools.partial(_inproj_kernel, layer),
        grid=(grp.n_tiles,),
        in_specs=[pl.BlockSpec((TOKEN_TILE, D_MODEL), lambda i: (i, 0)),
                  pl.BlockSpec((None, DEPTH * 6, D_MODEL), lambda i: (grp.cond_of_tile(i), 0, 0)),
                  pl.BlockSpec((1, D_MODEL), lambda i: (0, 0)),
                  pl.BlockSpec((D_MODEL, n_out), lambda i: (0, 0))],
        out_specs=pl.BlockSpec((TOKEN_TILE, n_out), lambda i: (i, 0)),
        out_shape=jax.ShapeDtypeStruct((n_tok, n_out), F32),
        compiler_params=_params(1),
        name="in_projection",
    )(x, mods, gain, w)


def _outproj_kernel(layer, n_parts, *refs):
    mix_refs, wout_refs = refs[:n_parts], refs[n_parts:2 * n_parts]
    x_ref, mod_ref, gain_ref, wr_ref, x1_ref, h2_ref, aff_ref = refs[2 * n_parts:]
    y = jnp.dot(mix_refs[0][...].astype(BF16), wout_refs[0][...], preferred_element_type=F32)
    for m_ref, w_ref in zip(mix_refs[1:], wout_refs[1:]):
        y = y + jnp.dot(m_ref[...].astype(BF16), w_ref[...], preferred_element_type=F32)
    x1 = x_ref[...] + _mod_row(mod_ref, layer, 2) * y
    x1_ref[...] = x1
    h2 = _norm_mod(x1, gain_ref[...], _mod_row(mod_ref, layer, 3), _mod_row(mod_ref, layer, 4))
    h2_ref[...] = h2.astype(BF16)
    logits = _mm3(h2, wr_ref[...])
    lane = lax.broadcasted_iota(jnp.int32, logits.shape, 1)
    logits = jnp.where(lane < N_EXPERTS, logits, -jnp.inf)
    e = jnp.exp(logits - jnp.max(logits, axis=-1, keepdims=True))
    aff_ref[...] = e / jnp.sum(e, axis=-1, keepdims=True)


def _out_projection(grp, layer, mixes, x, mods, w_outs, gain, w_router):
    n_tok = grp.n_seq * grp.seq_len
    tile = lambda width: pl.BlockSpec((TOKEN_TILE, width), lambda i: (i, 0))
    return pl.pallas_call(
        functools.partial(_outproj_kernel, layer, len(mixes)),
        grid=(grp.n_tiles,),
        in_specs=[tile(m.shape[1]) for m in mixes]
                 + [pl.BlockSpec(w.shape, lambda i: (0, 0)) for w in w_outs]
                 + [tile(D_MODEL),
                    pl.BlockSpec((None, DEPTH * 6, D_MODEL), lambda i: (grp.cond_of_tile(i), 0, 0)),
                    pl.BlockSpec((1, D_MODEL), lambda i: (0, 0)),
                    pl.BlockSpec((D_MODEL, LANES), lambda i: (0, 0))],
        out_specs=[tile(D_MODEL), tile(D_MODEL), tile(LANES)],
        out_shape=[jax.ShapeDtypeStruct((n_tok, D_MODEL), F32),
                   jax.ShapeDtypeStruct((n_tok, D_MODEL), BF16),
                   jax.ShapeDtypeStruct((n_tok, LANES), F32)],
        compiler_params=_params(1),
        name="out_projection_router",
    )(*mixes, *w_outs, x, mods, gain, w_router)


def _route_kernel(n, cap, aff_ref, h2_ref, xs_ref, pt_ref, gate_ref):
    aff = aff_ref[...]
    aff_t = aff.T
    t_sub = lax.broadcasted_iota(jnp.int32, (n, n), 0)
    t_lane = lax.broadcasted_iota(jnp.int32, (n, n), 1)
    earlier = t_sub < t_lane
    sel_rows = []
    for e in range(N_EXPERTS):
        col = aff[:, e:e + 1]
        row = aff_t[e:e + 1, :]
        beats = (col > row) | ((col == row) & earlier)
        rank = jnp.sum(jnp.where(beats, 1.0, 0.0), axis=0, keepdims=True)
        sel_rows.append(jnp.where(rank < cap, 1.0, 0.0))
    sel = jnp.concatenate(sel_rows, axis=0)
    pos = jnp.dot(sel.astype(BF16), jnp.where(earlier, 1.0, 0.0).astype(BF16), preferred_element_type=F32)
    e_idx = lax.broadcasted_iota(jnp.int32, (N_EXPERTS, n), 0)
    slot = jnp.where(sel > 0.0, pos.astype(jnp.int32) + e_idx * cap, -1)
    slot_pad = jnp.concatenate([slot, jnp.full((LANES - N_EXPERTS, n), -1, jnp.int32)], axis=0)
    slot_t = slot_pad.astype(F32).T.astype(jnp.int32)
    per_block = LANES // cap
    s_lane = lax.broadcasted_iota(jnp.int32, (n, LANES), 1)
    for blk in range(N_EXPERTS // per_block):
        hit = jnp.zeros((n, LANES), F32)
        for e in range(blk * per_block, (blk + 1) * per_block):
            hit = hit + jnp.where(slot_t[:, e:e + 1] == s_lane + blk * LANES, 1.0, 0.0)
        pt_ref[:, blk * LANES:(blk + 1) * LANES] = hit.astype(BF16)
    c_sub = lax.broadcasted_iota(jnp.int32, (cap, n), 0)
    picks = []
    for e in range(N_EXPERTS):
        p_e = jnp.where(slot[e:e + 1, :] == c_sub + e * cap, 1.0, 0.0)
        picks.append(p_e.astype(BF16))
        gate = jnp.sum(p_e * aff_t[e:e + 1, :], axis=1, keepdims=True)
        gate_ref[e] = jnp.broadcast_to(gate, (cap, LANES))
    xs = jnp.dot(jnp.concatenate(picks, axis=0), h2_ref[...], preferred_element_type=F32).astype(BF16)
    for e in range(N_EXPERTS):
        xs_ref[e] = xs[e * cap:(e + 1) * cap]


def _route(grp, aff, h2):
    n, cap = grp.seq_len, grp.capacity
    slots = N_EXPERTS * cap
    return pl.pallas_call(
        functools.partial(_route_kernel, n, cap),
        grid=(grp.n_seq,),
        in_specs=[pl.BlockSpec((n, LANES), lambda b: (b, 0)),
                  pl.BlockSpec((n, D_MODEL), lambda b: (b, 0))],
        out_specs=[pl.BlockSpec((N_EXPERTS, cap, D_MODEL), lambda b: (0, b, 0)),
                   pl.BlockSpec((n, slots), lambda b: (b, 0)),
                   pl.BlockSpec((N_EXPERTS, cap, LANES), lambda b: (0, b, 0))],
        out_shape=[jax.ShapeDtypeStruct((N_EXPERTS, grp.n_seq * cap, D_MODEL), BF16),
                   jax.ShapeDtypeStruct((grp.n_seq * n, slots), BF16),
                   jax.ShapeDtypeStruct((N_EXPERTS, grp.n_seq * cap, LANES), F32)],
        compiler_params=_params(1),
        name="expert_choice_route",
    )(aff, h2)


FF_TILE = 512


def _expert_kernel(xc_ref, xd_ref, gc_ref, gd_ref, wg_ref, wu_ref, wd_ref, yc_ref, yd_ref, accc_ref, accd_ref):
    f = pl.program_id(1)
    wg = wg_ref[...].astype(BF16)
    wu = wu_ref[...].astype(BF16)
    wd = wd_ref[...].astype(BF16)

    def ffn(x_ref, acc_ref):
        x = x_ref[...]
        a = jnp.dot(x, wg, preferred_element_type=F32)
        u = jnp.dot(x, wu, preferred_element_type=F32)
        hid = (a * jax.nn.sigmoid(a)) * u
        y = jnp.dot(hid.astype(BF16), wd, preferred_element_type=F32)

        @pl.when(f == 0)
        def _():
            acc_ref[...] = y

        @pl.when(f != 0)
        def _():
            acc_ref[...] += y

    ffn(xc_ref, accc_ref)
    ffn(xd_ref, accd_ref)

    @pl.when(f == pl.num_programs(1) - 1)
    def _():
        yc_ref[...] = (accc_ref[...] * gc_ref[:, 0:1]).astype(BF16)
        yd_ref[...] = (accd_ref[...] * gd_ref[:, 0:1]).astype(BF16)


def _experts(layer, xs_c, xs_d, gate_c, gate_d, w_gate, w_up, w_down):
    rc, rd = xs_c.shape[1], xs_d.shape[1]
    per_e = lambda rows, width: pl.BlockSpec((None, rows, width), lambda e, f: (e, 0, 0))
    return pl.pallas_call(
        _expert_kernel,
        grid=(N_EXPERTS, EXPERT_FF // FF_TILE),
        in_specs=[per_e(rc, D_MODEL), per_e(rd, D_MODEL), per_e(rc, LANES), per_e(rd, LANES),
                  pl.BlockSpec((None, None, D_MODEL, FF_TILE), lambda e, f: (layer, e, 0, f)),
                  pl.BlockSpec((None, None, D_MODEL, FF_TILE), lambda e, f: (layer, e, 0, f)),
                  pl.BlockSpec((None, None, FF_TILE, D_MODEL), lambda e, f: (layer, e, f, 0))],
        out_specs=[per_e(rc, D_MODEL), per_e(rd, D_MODEL)],
        out_shape=[jax.ShapeDtypeStruct(xs_c.shape, BF16), jax.ShapeDtypeStruct(xs_d.shape, BF16)],
        scratch_shapes=[pltpu.VMEM((rc, D_MODEL), F32), pltpu.VMEM((rd, D_MODEL), F32)],
        compiler_params=_params(2),
        name="expert_swiglu",
    )(xs_c, xs_d, gate_c, gate_d, w_gate, w_up, w_down)


def _combine_kernel(layer, cap, final, x1_ref, mod_ref, pt_ref, ys_ref, gain_ref, x2_ref):
    ys = ys_ref[...].reshape(N_EXPERTS * cap, D_MODEL)
    y = jnp.dot(pt_ref[...], ys, preferred_element_type=F32)
    x2 = x1_ref[...] + _mod_row(mod_ref, layer, 5) * y
    if final:
        x2 = x2 * lax.rsqrt(jnp.mean(x2 * x2, axis=-1, keepdims=True) + RMS_EPS) * gain_ref[...]
    x2_ref[...] = x2


def _combine(grp, layer, final, x1, mods, pt, ys, gain_final):
    n, cap = grp.seq_len, grp.capacity
    return pl.pallas_call(
        functools.partial(_combine_kernel, layer, cap, final),
        grid=(grp.n_seq,),
        in_specs=[pl.BlockSpec((n, D_MODEL), lambda b: (b, 0)),
                  pl.BlockSpec((None, DEPTH * 6, D_MODEL), lambda b: (grp.cond_of_seq(b), 0, 0)),
                  pl.BlockSpec((n, N_EXPERTS * cap), lambda b: (b, 0)),
                  pl.BlockSpec((N_EXPERTS, cap, D_MODEL), lambda b: (0, b, 0)),
                  pl.BlockSpec((1, D_MODEL), lambda b: (0, 0))],
        out_specs=pl.BlockSpec((n, D_MODEL), lambda b: (b, 0)),
        out_shape=jax.ShapeDtypeStruct(x1.shape, F32),
        compiler_params=_params(1),
        name="expert_combine",
    )(x1, mods, pt, ys, gain_final)


S5_BLOCKS = 2
S5_BLOCK_IN = C_WIDTH // S5_BLOCKS
S5_BLOCK_STATE = C_GROUPS * C_STATE // S5_BLOCKS
S5_SUBLANES = 8
S5_SCAN_COLS = 256
S5_SHIFTS = (1, 2, 4)


def _cmul(ar, ai, br, bi):
    return ar * br - ai * bi, ar * bi + ai * br


def _s5_prep_kernel(are_ref, aim_ref, ldt_ref, bre_ref, bim_ref, cre_ref, cim_ref, bmat_ref, cmat_ref, const_ref):
    n = S5_BLOCK_STATE
    row = lax.broadcasted_iota(jnp.int32, (S5_SUBLANES, n), 0)
    for d in range(N_DIR):
        a_re = jnp.minimum(are_ref[d], S5_MAX_RE)
        a_im = aim_ref[d]
        dt = jnp.exp(ldt_ref[d])
        mag = jnp.exp(a_re * dt)
        l_re = mag * jnp.cos(a_im * dt)
        l_im = mag * jnp.sin(a_im * dt)
        den = a_re * a_re + a_im * a_im
        k_re = ((l_re - 1.0) * a_re + l_im * a_im) / den
        k_im = (l_im * a_re - (l_re - 1.0) * a_im) / den
        b_re, b_im = bre_ref[...], bim_ref[...]
        bb_re, bb_im = _cmul(k_re, k_im, b_re, b_im)
        bmat_ref[d] = jnp.concatenate([bb_re, bb_im], axis=1).astype(BF16)
        pows = [(l_re, l_im)]
        for _ in range(S5_SUBLANES - 1):
            pows.append(_cmul(pows[-1][0], pows[-1][1], l_re, l_im))
        for i, s in enumerate(S5_SHIFTS):
            keep = (row >= s) if d == 0 else (row <= S5_SUBLANES - 1 - s)
            const_ref[d, 2 * i] = jnp.where(keep, pows[s - 1][0], 0.0)
            const_ref[d, 2 * i + 1] = jnp.where(keep, pows[s - 1][1], 0.0)
        lp_re = jnp.zeros((S5_SUBLANES, n), F32)
        lp_im = jnp.zeros((S5_SUBLANES, n), F32)
        for r in range(S5_SUBLANES):
            p = pows[r] if d == 0 else pows[S5_SUBLANES - 1 - r]
            lp_re = jnp.where(row == r, p[0], lp_re)
            lp_im = jnp.where(row == r, p[1], lp_im)
        const_ref[d, 6] = lp_re
        const_ref[d, 7] = lp_im
    cmat_ref[...] = jnp.concatenate([cre_ref[...], -cim_ref[...]], axis=0).astype(BF16)


def _s5_prepare(a_re, a_im, log_dt, b_re, b_im, c_re, c_im):
    n, k_in = S5_BLOCK_STATE, S5_BLOCK_IN
    gpb = C_GROUPS // S5_BLOCKS
    per_state = lambda t: t.reshape(N_DIR, S5_BLOCKS, 1, n)
    ldt = jnp.repeat(log_dt, C_STATE, axis=-1)
    eye = jnp.eye(gpb, dtype=F32)

    def expand_b(b):
        b = b.reshape(S5_BLOCKS, gpb, C_STATE, C_GROUP).transpose(0, 1, 3, 2)
        return (b[:, :, :, None, :] * eye[None, :, None, :, None]).reshape(S5_BLOCKS, k_in, n)

    def expand_c(c):
        c = c.reshape(S5_BLOCKS, gpb, C_GROUP, C_STATE).transpose(0, 1, 3, 2)
        return (c[:, :, :, None, :] * eye[None, :, None, :, None]).reshape(S5_BLOCKS, n, k_in)

    row_spec = pl.BlockSpec((N_DIR, None, 1, n), lambda k: (0, k, 0, 0))
    return pl.pallas_call(
        _s5_prep_kernel,
        grid=(S5_BLOCKS,),
        in_specs=[row_spec, row_spec, row_spec,
                  pl.BlockSpec((None, k_in, n), lambda k: (k, 0, 0)),
                  pl.BlockSpec((None, k_in, n), lambda k: (k, 0, 0)),
                  pl.BlockSpec((None, n, k_in), lambda k: (k, 0, 0)),
                  pl.BlockSpec((None, n, k_in), lambda k: (k, 0, 0))],
        out_specs=[pl.BlockSpec((N_DIR, None, k_in, 2 * n), lambda k: (0, k, 0, 0)),
                   pl.BlockSpec((None, 2 * n, k_in), lambda k: (k, 0, 0)),
                   pl.BlockSpec((N_DIR, None, 8, S5_SUBLANES, n), lambda k: (0, k, 0, 0, 0))],
        out_shape=[jax.ShapeDtypeStruct((N_DIR, S5_BLOCKS, k_in, 2 * n), BF16),
                   jax.ShapeDtypeStruct((S5_BLOCKS, 2 * n, k_in), BF16),
                   jax.ShapeDtypeStruct((N_DIR, S5_BLOCKS, 8, S5_SUBLANES, n), F32)],
        compiler_params=_params(1),
        name="s5_prepare",
    )(per_state(a_re), per_state(a_im), per_state(ldt), expand_b(b_re), expand_b(b_im), expand_c(c_re),
      expand_c(c_im))


def _s5_kernel(seq_len, has_state, u_ref, bmat_ref, cmat_ref, const_ref, *rest):
    if has_state:
        s0_ref, y_ref, fin_ref, xre_ref, xim_ref = rest
    else:
        y_ref, fin_ref, xre_ref, xim_ref = rest
    n = S5_BLOCK_STATE
    n_tiles = seq_len // S5_SUBLANES
    u = u_ref[...].astype(BF16)
    for d in range(N_DIR):
        bu = jnp.dot(u, bmat_ref[d], preferred_element_type=F32)
        xre_ref[d] = bu[:, :n]
        xim_ref[d] = bu[:, n:]

    for cb in range(n // S5_SCAN_COLS):
        cols = slice(cb * S5_SCAN_COLS, (cb + 1) * S5_SCAN_COLS)

        def scan_tile(d, i, carry):
            rows = pl.ds(pl.multiple_of(i * S5_SUBLANES, S5_SUBLANES), S5_SUBLANES)
            xr = xre_ref[d, rows, cols]
            xi = xim_ref[d, rows, cols]
            for k, s in enumerate(S5_SHIFTS):
                shift = s if d == 0 else S5_SUBLANES - s
                pr, pi = _cmul(const_ref[d, 2 * k, :, cols], const_ref[d, 2 * k + 1, :, cols],
                               pltpu.roll(xr, shift, 0), pltpu.roll(xi, shift, 0))
                xr, xi = xr + pr, xi + pi
            cr, ci = _cmul(const_ref[d, 6, :, cols], const_ref[d, 7, :, cols], carry[0], carry[1])
            xr, xi = xr + cr, xi + ci
            xre_ref[d, rows, cols] = xr
            xim_ref[d, rows, cols] = xi
            edge = S5_SUBLANES - 1 if d == 0 else 0
            shape = (S5_SUBLANES, S5_SCAN_COLS)
            return (jnp.broadcast_to(xr[edge:edge + 1], shape), jnp.broadcast_to(xi[edge:edge + 1], shape))

        def body(i, carry):
            return (scan_tile(0, i, carry[0]), scan_tile(1, n_tiles - 1 - i, carry[1]))

        shape = (S5_SUBLANES, S5_SCAN_COLS)
        if has_state:
            init = tuple((jnp.broadcast_to(s0_ref[2 * d:2 * d + 1, cols], shape),
                          jnp.broadcast_to(s0_ref[2 * d + 1:2 * d + 2, cols], shape)) for d in range(N_DIR))
        else:
            init = tuple((jnp.zeros(shape, F32), jnp.zeros(shape, F32)) for _ in range(N_DIR))
        fin = lax.fori_loop(0, n_tiles, body, init)
        for d in range(N_DIR):
            fin_ref[2 * d:2 * d + 1, cols] = fin[d][0][0:1]
            fin_ref[2 * d + 1:2 * d + 2, cols] = fin[d][1][0:1]

    x = jnp.concatenate([xre_ref[0] + xre_ref[1], xim_ref[0] + xim_ref[1]], axis=1).astype(BF16)
    y_ref[...] = jnp.dot(x, cmat_ref[...], preferred_element_type=F32)


def _s5_mixer(grp, proj, prep, s0):
    bmat, cmat, consts = prep
    n, k_in, seq_len = S5_BLOCK_STATE, S5_BLOCK_IN, grp.seq_len
    has_state = s0 is not None
    state_spec = pl.BlockSpec((None, None, 2 * N_DIR, n), lambda k, b: (b, k, 0, 0))
    in_specs = [pl.BlockSpec((seq_len, k_in), lambda k, b: (b, k)),
                pl.BlockSpec((N_DIR, None, k_in, 2 * n), lambda k, b: (0, k, 0, 0)),
                pl.BlockSpec((None, 2 * n, k_in), lambda k, b: (k, 0, 0)),
                pl.BlockSpec((N_DIR, None, 8, S5_SUBLANES, n), lambda k, b: (0, k, 0, 0, 0))]
    args = [proj, bmat, cmat, consts]
    if has_state:
        in_specs.append(state_spec)
        args.append(s0)
    return pl.pallas_call(
        functools.partial(_s5_kernel, seq_len, has_state),
        grid=(S5_BLOCKS, grp.n_seq),
        in_specs=in_specs,
        out_specs=[pl.BlockSpec((seq_len, k_in), lambda k, b: (b, k)), state_spec],
        out_shape=[jax.ShapeDtypeStruct((grp.n_seq * seq_len, C_WIDTH), F32),
                   jax.ShapeDtypeStruct((grp.n_seq, S5_BLOCKS, 2 * N_DIR, n), F32)],
        scratch_shapes=[pltpu.VMEM((N_DIR, seq_len, n), F32), pltpu.VMEM((N_DIR, seq_len, n), F32)],
        compiler_params=_params(2),
        name="s5_scan",
    )(*args)


def _s5_state_to_blocks(s):
    b = s.shape[0]
    s = s.reshape(b, N_DIR, S5_BLOCKS, S5_BLOCK_STATE, 2).transpose(0, 2, 1, 4, 3)
    return s.reshape(b, S5_BLOCKS, 2 * N_DIR, S5_BLOCK_STATE)


def _s5_state_from_blocks(s):
    b = s.shape[0]
    s = s.reshape(b, S5_BLOCKS, N_DIR, 2, S5_BLOCK_STATE).transpose(0, 2, 1, 4, 3)
    return s.reshape(b, N_DIR, C_GROUPS, C_STATE, 2)


ODD_IN_PAD = _round_up(sum(ODD_SPLITS), LANES)
_O_U, _O_Q, _O_K, _O_V, _O_G, _O_A = (int(v) for v in np.cumsum((0,) + ODD_SPLITS[:-1]))
GLA_QK = D_HEADS * D_DK
GLA_V = D_HEADS * D_DV
GLA_MIN_SUPER = 256
GLA_REF_ROWS = 128
GLA_SUBLANES = 8


def _split3(x):
    x1 = x.astype(BF16)
    r1 = x - x1.astype(F32)
    x2 = r1.astype(BF16)
    x3 = (r1 - x2.astype(F32)).astype(BF16)
    return x1, x2, x3


def _dot_01(m01, x):
    p1, p2, p3 = _split3(x)
    dot = lambda p: jnp.dot(m01, p, preferred_element_type=F32)
    return dot(p1) + dot(p2) + dot(p3)


def _dot_nt(a, b):
    return lax.dot_general(a, b, (((1,), (1,)), ((), ())), preferred_element_type=F32)


def _one_hot(cond):
    return jnp.where(cond, 1.0, 0.0).astype(BF16)


def _gla_kernel(seq_len, has_state, heads, u_ref, q_ref, k_ref, v_ref, g_ref, a_ref, ypre_ref, dskip_ref, wglu_ref,
                wg2_ref, bgate_ref, gain_ref, *rest):
    if has_state:
        s0_ref, outc_ref, outd_ref, p_s = rest
    else:
        outc_ref, outd_ref, fin_ref, p_s = rest
    n = seq_len

    @pl.when(pl.program_id(1) == 0)
    def _():
        y = jax.nn.gelu(ypre_ref[...] + dskip_ref[...] * u_ref[...])
        glu = jnp.dot(y.astype(BF16), wglu_ref[...], preferred_element_type=F32)
        outc_ref[...] = y * jax.nn.sigmoid(glu)

    q = q_ref[...] * (D_DK ** -0.5)
    k = k_ref[...]
    r_i = lax.broadcasted_iota(jnp.int32, (n, n), 0)
    c_i = lax.broadcasted_iota(jnp.int32, (n, n), 1)
    b = []
    for d in range(N_DIR):
        logits = jnp.dot(a_ref[...], wg2_ref[d], preferred_element_type=F32,
                         precision=lax.Precision.HIGHEST) + bgate_ref[d]
        la = (jnp.minimum(logits, 0.0) - jnp.log(1.0 + jnp.exp(-jnp.abs(logits)))) / D_TAU
        b.append(_dot_01(_one_hot((r_i >= c_i) if d == 0 else (r_i <= c_i)), la))

    lane_p = lax.broadcasted_iota(jnp.int32, (1, LANES), 1)

    def pair_lanes(x, h):
        return x[:, (h // 2) * LANES:(h // 2 + 1) * LANES]

    def head_only(x, h):
        mine = (lane_p < D_DK) if h % 2 == 0 else (lane_p >= D_DK)
        return jnp.where(mine, pair_lanes(x, h), 0.0).astype(BF16)

    for h in range(heads):
        s = _dot_nt(head_only(q, h), pair_lanes(k, h).astype(BF16))
        p_s[h] = jnp.where(r_i == c_i, 2.0 * s, 0.0)

    for lb in range(int(math.log2(n))):
        half = 1 << lb
        width = min(max(2 * half, GLA_MIN_SUPER), n)
        w_r = lax.broadcasted_iota(jnp.int32, (width, width), 0)
        w_c = lax.broadcasted_iota(jnp.int32, (width, width), 1)
        at_level = ((w_r ^ w_c) >> lb) == 1
        later = w_r > w_c
        qt, kt = [], []
        for d in range(N_DIR):
            edge = half - 1 if d == 0 else half
            if 2 * half >= GLA_SUBLANES:
                blocks = b[d].reshape(n // (2 * half), 2 * half, b[d].shape[1])
                ref = jnp.broadcast_to(blocks[:, edge:edge + 1, :], blocks.shape).reshape(b[d].shape)
            elif n <= GLA_MIN_SUPER:
                sel = c_i == ((r_i >> (lb + 1)) << (lb + 1)) + edge
                ref = _dot_01(_one_hot(sel), b[d])
            else:
                n_ref = max(n >> (lb + 1), GLA_REF_ROWS)
                blk = lax.broadcasted_iota(jnp.int32, (n_ref, n), 0)
                tok = lax.broadcasted_iota(jnp.int32, (n_ref, n), 1)
                per_block = _dot_01(_one_hot(tok == (blk << (lb + 1)) + edge), b[d])
                tok_r = lax.broadcasted_iota(jnp.int32, (n, n_ref), 0)
                blk_c = lax.broadcasted_iota(jnp.int32, (n, n_ref), 1)
                ref = _dot_01(_one_hot(blk_c == (tok_r >> (lb + 1))), per_block)
            qt.append(q * jnp.exp(jnp.minimum(b[d] - ref, 0.0)))
            kt.append(k * jnp.exp(jnp.minimum(ref - b[d], 0.0)))
        for sb in range(n // width):
            rows = slice(sb * width, (sb + 1) * width)
            for h in range(heads):
                s_f = _dot_nt(head_only(qt[0][rows], h), pair_lanes(kt[0][rows], h).astype(BF16))
                s_b = _dot_nt(head_only(qt[1][rows], h), pair_lanes(kt[1][rows], h).astype(BF16))
                p_s[h, rows, rows] = p_s[h, rows, rows] + jnp.where(at_level, jnp.where(later, s_f, s_b), 0.0)

    zeros = jnp.zeros((D_DK, D_DV), F32)
    for h in range(heads):
        vh = v_ref[:, h * D_DV:(h + 1) * D_DV].astype(BF16)
        o = jnp.dot(p_s[h].astype(BF16), vh, preferred_element_type=F32)
        if has_state:
            for d in range(N_DIR):
                qs = pair_lanes(q * jnp.exp(b[d]), h).astype(BF16)
                s0 = s0_ref[d, h]
                s0_pad = jnp.concatenate([s0, zeros] if h % 2 == 0 else [zeros, s0], axis=0).astype(BF16)
                o = o + jnp.dot(qs, s0_pad, preferred_element_type=F32)
        o = o * lax.rsqrt(jnp.mean(o * o, axis=-1, keepdims=True) + RMS_EPS) * gain_ref[...]
        gate = g_ref[:, h * D_DV:(h + 1) * D_DV]
        outd_ref[:, h * D_DV:(h + 1) * D_DV] = o * (gate * jax.nn.sigmoid(gate))

    if not has_state:
        for d in range(N_DIR):
            last = n - 1 if d == 0 else 0
            k_out_t = (k * jnp.exp(b[d][last:last + 1] - b[d])).T.astype(BF16)
            for h in range(heads):
                vh = v_ref[:, h * D_DV:(h + 1) * D_DV].astype(BF16)
                fin_ref[d, h] = jnp.dot(k_out_t[h * D_DK:(h + 1) * D_DK], vh, preferred_element_type=F32)


def _gla_mixer(grp, proj, ypre, d_skip, w_glu, w_gate2, b_gate, gain, s0):
    seq_len = grp.seq_len
    has_state = s0 is not None
    heads = D_HEADS if seq_len <= GLA_MIN_SUPER else 2
    groups = D_HEADS // heads
    n_qk, n_v = heads * D_DK, heads * D_DV
    wg2 = jnp.zeros((N_DIR, LANES, GLA_QK), F32)
    for d in range(N_DIR):
        wg2 = wg2.at[d, d * D_RANK:(d + 1) * D_RANK].set(w_gate2[d])
    full = lambda *shape: pl.BlockSpec(shape, lambda b, g: (0,) * len(shape))
    cols = lambda width, offset: pl.BlockSpec((seq_len, width), lambda b, g: (b, offset // width + g))
    fixed = lambda width, offset: pl.BlockSpec((seq_len, width), lambda b, g: (b, offset // width))
    state_spec = pl.BlockSpec((None, N_DIR, heads, D_DK, D_DV), lambda b, g: (b, 0, g, 0, 0))
    in_specs = [fixed(C_WIDTH, _O_U), cols(n_qk, _O_Q), cols(n_qk, _O_K), cols(n_v, _O_V), cols(n_v, _O_G),
                fixed(LANES, _O_A), fixed(C_WIDTH, 0),
                full(1, C_WIDTH), full(C_WIDTH, C_WIDTH),
                pl.BlockSpec((N_DIR, LANES, n_qk), lambda b, g: (0, 0, g)),
                pl.BlockSpec((N_DIR, 1, n_qk), lambda b, g: (0, 0, g)),
                full(1, D_DV)]
    args = [proj] * 6 + [ypre, d_skip[None], w_glu.astype(BF16), wg2, b_gate.reshape(N_DIR, 1, GLA_QK), gain[None]]
    n_tok = grp.n_seq * seq_len
    out_specs = [pl.BlockSpec((seq_len, C_WIDTH), lambda b, g: (b, 0)),
                 pl.BlockSpec((seq_len, n_v), lambda b, g: (b, g))]
    out_shape = [jax.ShapeDtypeStruct((n_tok, C_WIDTH), F32), jax.ShapeDtypeStruct((n_tok, GLA_V), F32)]
    if has_state:
        in_specs.append(state_spec)
        args.append(s0)
    else:
        out_specs.append(state_spec)
        out_shape.append(jax.ShapeDtypeStruct((grp.n_seq, N_DIR, D_HEADS, D_DK, D_DV), F32))
    out = pl.pallas_call(
        functools.partial(_gla_kernel, seq_len, has_state, heads),
        grid=(grp.n_seq, groups),
        in_specs=in_specs, out_specs=out_specs, out_shape=out_shape,
        scratch_shapes=[pltpu.VMEM((heads, seq_len, seq_len), F32)],
        compiler_params=_params(2),
        name="gla_mixer",
    )(*args)
    return (out[0], out[1], None) if has_state else tuple(out)


EV_A = A_HEADS * A_DK
EV_B = B_HEADS * B_DK
EV_HALF = 4 * EV_A
EVEN_IN_PAD = 2 * EV_HALF + LANES
GDN_PAIR = 2 * A_CHUNK
GDN_LEVELS = 5


def _regroup_even_cols(w):
    small = 2 * N_DIR * A_HEADS
    pad = jnp.zeros((w.shape[0], EVEN_IN_PAD - 2 * EV_HALF - small), w.dtype)
    return jnp.concatenate([w[:, :EV_HALF], w[:, EV_HALF + small:], w[:, EV_HALF:EV_HALF + small], pad], axis=1)


def _softplus(x):
    return jnp.maximum(x, 0.0) + jnp.log(1.0 + jnp.exp(-jnp.abs(x)))


def _silu(x):
    return x * jax.nn.sigmoid(x)


def _gdn_kernel(seq_len, has_state, proj_ref, small_ref, convw_ref, alog_ref, dtb_ref, gain_ref, *rest):
    if has_state:
        s0_ref, out_ref, q_s, k_s, v_s, gb_s, gcum_s, o_s, st_s = rest
    else:
        out_ref, fin_ref, q_s, k_s, v_s, gb_s, gcum_s, o_s, st_s = rest
    n_pairs = seq_len // GDN_PAIR
    width = 3 * EV_A

    x = proj_ref[:, 0:width]
    t_idx = lax.broadcasted_iota(jnp.int32, (seq_len, 1), 0)
    pad = A_CONV // 2
    acc = x * convw_ref[pad:pad + 1, :]
    for s in range(-pad, pad + 1):
        if s == 0:
            continue
        shifted = pltpu.roll(x, (-s) % seq_len, 0)
        inside = (t_idx + s >= 0) & (t_idx + s < seq_len)
        acc = acc + jnp.where(inside, shifted, 0.0) * convw_ref[pad + s:pad + s + 1, :]
    y = _silu(acc)
    for h in range(A_HEADS):
        cols = slice(h * A_DK, (h + 1) * A_DK)
        qh = y[:, h * A_DK:(h + 1) * A_DK]
        kh = y[:, EV_A + h * A_DK:EV_A + (h + 1) * A_DK]
        q_s[:, cols] = qh * lax.rsqrt(jnp.sum(qh * qh, axis=-1, keepdims=True) + 1e-6) * (A_DK ** -0.5)
        k_s[:, cols] = kh * lax.rsqrt(jnp.sum(kh * kh, axis=-1, keepdims=True) + 1e-6)
    v_s[...] = y[:, 2 * EV_A:3 * EV_A]

    small = small_ref[...]
    lane = lax.broadcasted_iota(jnp.int32, small.shape, 1)
    gb = jnp.where(lane < N_DIR * A_HEADS, jax.nn.sigmoid(small),
                   -jnp.exp(alog_ref[...]) * _softplus(small + dtb_ref[...]))
    gb_s[...] = gb
    r_i = lax.broadcasted_iota(jnp.int32, (seq_len, seq_len), 0)
    c_i = lax.broadcasted_iota(jnp.int32, (seq_len, seq_len), 1)
    same = (r_i // A_CHUNK) == (c_i // A_CHUNK)
    gcum_s[0] = _dot_01(jnp.where(same & (r_i >= c_i), 1.0, 0.0).astype(BF16), gb)
    gcum_s[1] = _dot_01(jnp.where(same & (r_i <= c_i), 1.0, 0.0).astype(BF16), gb)

    o_s[...] = jnp.zeros(o_s.shape, F32)
    for d in range(N_DIR):
        for h in range(A_HEADS):
            st_s[d, h] = s0_ref[d, h] if has_state else jnp.zeros((A_DK, A_DV), F32)

    pr = lax.broadcasted_iota(jnp.int32, (GDN_PAIR, GDN_PAIR), 0)
    pc = lax.broadcasted_iota(jnp.int32, (GDN_PAIR, GDN_PAIR), 1)
    p_same = (pr // A_CHUNK) == (pc // A_CHUNK)
    eye = jnp.where(pr == pc, 1.0, 0.0)
    row_p = lax.broadcasted_iota(jnp.int32, (GDN_PAIR, 1), 0)
    zeros_c = jnp.zeros((A_CHUNK, A_DV), F32)

    def pair(p, _):
        bodies = []
        for d in range(N_DIR):
            base = pl.multiple_of((p if d == 0 else n_pairs - 1 - p) * GDN_PAIR, GDN_PAIR)
            rows = pl.ds(base, GDN_PAIR)
            gbp = gb_s[rows, :]
            gp = gcum_s[d, rows, :]
            gp_t = gp.T
            tri = (pr >= pc) if d == 0 else (pr <= pc)
            strict = (pr > pc) if d == 0 else (pr < pc)
            for h in range(A_HEADS):
                cols = slice(h * A_DK, (h + 1) * A_DK)
                bi, gi = d * A_HEADS + h, N_DIR * A_HEADS + d * A_HEADS + h
                qp, kp, vp = q_s[rows, cols], k_s[rows, cols], v_s[rows, cols]
                beta = gbp[:, bi:bi + 1]
                g_col = gp[:, gi:gi + 1]
                g_row = gp_t[gi:gi + 1, :]
                decay = jnp.where(p_same & tri, jnp.exp(jnp.minimum(g_col - g_row, 0.0)), 0.0)
                kb = kp * beta
                kp16 = kp.astype(BF16)
                m = jnp.where(p_same & strict, -(_dot_nt(kb.astype(BF16), kp16) * decay), 0.0)
                e_g = jnp.exp(g_col)
                last = A_CHUNK - 1 if d == 0 else 0
                g_last = [g_col[c * A_CHUNK + last:c * A_CHUNK + last + 1] for c in range(2)]
                gl_col = jnp.where(row_p < A_CHUNK, g_last[0], g_last[1])
                bodies.append(dict(
                    d=d, h=h, base=base, cols=cols, m=m, g_last=g_last,
                    rhs=jnp.concatenate([vp * beta, kb * e_g], axis=1).astype(BF16),
                    attn=(_dot_nt(qp.astype(BF16), kp16) * decay).astype(BF16),
                    q_in=(qp * e_g).astype(BF16),
                    k_out_t=(kp * jnp.exp(gl_col - g_col)).T.astype(BF16)))

        ms = [body["m"] for body in bodies]
        t_invs = [eye + m for m in ms]
        for _ in range(GDN_LEVELS):
            ms = [_mm3(m, m) for m in ms]
            t_invs = [t + _mm3(t, m) for t, m in zip(t_invs, ms)]

        for body, t_inv in zip(bodies, t_invs):
            d, h, base, cols, g_last = body["d"], body["h"], body["base"], body["cols"], body["g_last"]
            uw = jnp.dot(t_inv.astype(BF16), body["rhs"], preferred_element_type=F32)
            u, w = uw[:, :A_DV], uw[:, A_DV:]
            s = st_s[d, h]
            for c in ((0, 1) if d == 0 else (1, 0)):
                rs = slice(c * A_CHUNK, (c + 1) * A_CHUNK)
                s16 = s.astype(BF16)
                v_new = u[rs] - jnp.dot(w[rs].astype(BF16), s16, preferred_element_type=F32)
                v_full = jnp.concatenate([v_new, zeros_c] if c == 0 else [zeros_c, v_new], axis=0).astype(BF16)
                o_c = (jnp.dot(body["q_in"][rs], s16, preferred_element_type=F32)
                       + jnp.dot(body["attn"][rs], v_full, preferred_element_type=F32))
                s = s * jnp.exp(g_last[c]) + jnp.dot(body["k_out_t"], v_full, preferred_element_type=F32)
                rows_c = pl.ds(base + c * A_CHUNK, A_CHUNK)
                o_s[rows_c, cols] = o_s[rows_c, cols] + o_c
            st_s[d, h] = s
        return 0

    lax.fori_loop(0, n_pairs, pair, 0)

    for h in range(A_HEADS):
        cols = slice(h * A_DV, (h + 1) * A_DV)
        o = o_s[:, cols]
        o = o * lax.rsqrt(jnp.mean(o * o, axis=-1, keepdims=True) + RMS_EPS) * gain_ref[...]
        out_ref[:, cols] = o * _silu(proj_ref[:, 3 * EV_A + h * A_DV:3 * EV_A + (h + 1) * A_DV])
    if not has_state:
        for d in range(N_DIR):
            for h in range(A_HEADS):
                fin_ref[d, h] = st_s[d, h]


def _gdn_mixer(grp, proj, conv_w, a_log, dt_bias, gain, s0):
    seq_len = grp.seq_len
    has_state = s0 is not None
    n_small = N_DIR * A_HEADS
    lane_row = lambda v: jnp.zeros((1, LANES), F32).at[0, n_small:2 * n_small].set(v.reshape(n_small))
    convw = jnp.zeros((8, 3 * EV_A), F32).at[:A_CONV].set(conv_w)
    full = lambda *shape: pl.BlockSpec(shape, lambda b: (0,) * len(shape))
    state_spec = pl.BlockSpec((None, N_DIR, A_HEADS, A_DK, A_DV), lambda b: (b, 0, 0, 0, 0))
    in_specs = [pl.BlockSpec((seq_len, EV_HALF), lambda b: (b, 0)),
                pl.BlockSpec((seq_len, LANES), lambda b: (b, 2 * EV_HALF // LANES)),
                full(8, 3 * EV_A), full(1, LANES), full(1, LANES), full(1, A_DV)]
    args = [proj, proj, convw, lane_row(a_log), lane_row(dt_bias), gain[None]]
    out_spec = pl.BlockSpec((seq_len, EV_A), lambda b: (b, 0))
    out_shape = jax.ShapeDtypeStruct((grp.n_seq * seq_len, EV_A), F32)
    if has_state:
        in_specs.append(state_spec)
        args.append(s0)
        out_specs, out_shapes = out_spec, out_shape
    else:
        out_specs = [out_spec, state_spec]
        out_shapes = [out_shape, jax.ShapeDtypeStruct((grp.n_seq, N_DIR, A_HEADS, A_DK, A_DV), F32)]
    tok = lambda w: pltpu.VMEM((seq_len, w), F32)
    out = pl.pallas_call(
        functools.partial(_gdn_kernel, seq_len, has_state),
        grid=(grp.n_seq,),
        in_specs=in_specs, out_specs=out_specs, out_shape=out_shapes,
        scratch_shapes=[tok(EV_A), tok(EV_A), tok(EV_A), tok(LANES), pltpu.VMEM((N_DIR, seq_len, LANES), F32),
                        tok(EV_A), pltpu.VMEM((N_DIR, A_HEADS, A_DK, A_DV), F32)],
        compiler_params=_params(1),
        name="gdn_mixer",
    )(*args)
    return (out, None) if has_state else (out[0], out[1])


def _rope_tables(n_tokens):
    lane = np.arange(EV_B)
    axis = (lane % B_DK) // (2 * ROPE_FREQS)
    half = (lane % (2 * ROPE_FREQS)) // ROPE_FREQS
    freq = ROPE_BASE ** (-(lane % ROPE_FREQS).astype(np.float32) / ROPE_FREQS)
    tok = jnp.arange(n_tokens, dtype=F32)
    pos = jnp.where(jnp.asarray(axis)[None, :] == 0, jnp.floor(tok / GRID_W)[:, None], (tok % GRID_W)[:, None])
    ang = pos * jnp.asarray(freq, F32)[None, :]
    sign = jnp.asarray(np.where(half == 0, -1.0, 1.0), F32)[None, :]
    return jnp.cos(ang), jnp.sin(ang) * sign


def _dot2_lhs(x, m):
    hi = x.astype(BF16)
    lo = (x - hi.astype(F32)).astype(BF16)
    return jnp.dot(hi, m, preferred_element_type=F32) + jnp.dot(lo, m, preferred_element_type=F32)


def _ret_kernel(seq_len, has_state, rope, proj_ref, lgd_ref, gain_ref, *rest):
    rest = list(rest)
    cos_ref, sin_ref = (rest.pop(0), rest.pop(0)) if rope else (None, None)
    s0_ref = rest.pop(0) if has_state else None
    out_ref = rest.pop(0)
    fin_ref = None if has_state else rest.pop(0)

    q = proj_ref[:, 0:EV_B]
    k = proj_ref[:, EV_B:2 * EV_B]
    if rope:
        lane = lax.broadcasted_iota(jnp.int32, (seq_len, EV_B), 1)
        first = (lane % (2 * ROPE_FREQS)) < ROPE_FREQS

        def rotate(x):
            partner = jnp.where(first, pltpu.roll(x, EV_B - ROPE_FREQS, 1), pltpu.roll(x, ROPE_FREQS, 1))
            return x * cos_ref[...] + partner * sin_ref[...]

        q, k = rotate(q), rotate(k)
    q = q * (B_DK ** -0.5)
    log_gamma = -jnp.exp(lgd_ref[...])
    i_col = lax.broadcasted_iota(jnp.int32, (seq_len, 1), 0).astype(F32)
    dist = (lax.broadcasted_iota(jnp.int32, (seq_len, seq_len), 0)
            - lax.broadcasted_iota(jnp.int32, (seq_len, seq_len), 1)).astype(F32)
    lane_p = lax.broadcasted_iota(jnp.int32, (1, LANES), 1)
    h_avg = jnp.where((lax.broadcasted_iota(jnp.int32, (LANES, LANES), 0) // B_DV)
                      == (lax.broadcasted_iota(jnp.int32, (LANES, LANES), 1) // B_DV), 1.0 / B_DV, 0.0).astype(BF16)

    for pair in range(B_HEADS // 2):
        cols = slice(pair * LANES, (pair + 1) * LANES)
        qp, kp = q[:, cols], k[:, cols]
        vp = proj_ref[:, 2 * EV_B + pair * LANES:2 * EV_B + (pair + 1) * LANES]
        kp16, vp16 = kp.astype(BF16), vp.astype(BF16)
        o = jnp.zeros((seq_len, LANES), F32)
        lg = [[log_gamma[d:d + 1, 2 * pair + e:2 * pair + e + 1] for e in range(2)] for d in range(N_DIR)]
        for e in range(2):
            mine = (lane_p < B_DK) if e == 0 else (lane_p >= B_DK)
            scores = _dot_nt(jnp.where(mine, qp, 0.0).astype(BF16), kp16)
            weight = (jnp.where(dist >= 0.0, jnp.exp(jnp.maximum(dist, 0.0) * lg[0][e]), 0.0)
                      + jnp.where(dist <= 0.0, jnp.exp(jnp.maximum(-dist, 0.0) * lg[1][e]), 0.0))
            o = o + jnp.dot((scores * weight).astype(BF16), jnp.where(mine, vp, 0.0).astype(BF16),
                            preferred_element_type=F32)
        if has_state:
            for d in range(N_DIR):
                steps = (i_col + 1.0) if d == 0 else (seq_len - i_col)
                xi = jnp.where(lane_p < B_DK, jnp.exp(steps * lg[d][0]), jnp.exp(steps * lg[d][1]))
                zeros = jnp.zeros((B_DK, B_DV), F32)
                s_pair = jnp.concatenate(
                    [jnp.concatenate([s0_ref[d, 2 * pair], zeros], axis=1),
                     jnp.concatenate([zeros, s0_ref[d, 2 * pair + 1]], axis=1)], axis=0)
                o = o + jnp.dot((qp * xi).astype(BF16), s_pair.astype(BF16), preferred_element_type=F32)
        else:
            for d in range(N_DIR):
                steps = (seq_len - 1.0 - i_col) if d == 0 else i_col
                zeta = jnp.where(lane_p < B_DK, jnp.exp(steps * lg[d][0]), jnp.exp(steps * lg[d][1]))
                kz_t = (kp * zeta).T.astype(BF16)
                both = jnp.dot(kz_t, vp16, preferred_element_type=F32)
                fin_ref[d, 2 * pair] = both[:B_DK, :B_DV]
                fin_ref[d, 2 * pair + 1] = both[B_DK:, B_DV:]
        mu = _dot2_lhs(o, h_avg)
        cen = o - mu
        var = _dot2_lhs(cen * cen, h_avg)
        normed = cen * lax.rsqrt(var + RMS_EPS) * gain_ref[:, cols]
        out_ref[:, cols] = normed * _silu(proj_ref[:, 3 * EV_B + pair * LANES:3 * EV_B + (pair + 1) * LANES])


def _ret_mixer(grp, proj, log_decay, gain, s0, rope_tables):
    seq_len = grp.seq_len
    has_state = s0 is not None
    rope = rope_tables is not None
    lgd = jnp.zeros((8, LANES), F32).at[:N_DIR, :B_HEADS].set(log_decay)
    full = lambda *shape: pl.BlockSpec(shape, lambda b: (0,) * len(shape))
    state_spec = pl.BlockSpec((None, N_DIR, B_HEADS, B_DK, B_DV), lambda b: (b, 0, 0, 0, 0))
    in_specs = [pl.BlockSpec((seq_len, EV_HALF), lambda b: (b, 1)), full(8, LANES), full(1, EV_B)]
    args = [proj, lgd, gain[None]]
    if rope:
        in_specs += [full(seq_len, EV_B), full(seq_len, EV_B)]
        args += list(rope_tables)
    out_spec = pl.BlockSpec((seq_len, EV_B), lambda b: (b, 0))
    out_shape = jax.ShapeDtypeStruct((grp.n_seq * seq_len, EV_B), F32)
    if has_state:
        in_specs.append(state_spec)
        args.append(s0)
        out_specs, out_shapes = out_spec, out_shape
    else:
        out_specs = [out_spec, state_spec]
        out_shapes = [out_shape, jax.ShapeDtypeStruct((grp.n_seq, N_DIR, B_HEADS, B_DK, B_DV), F32)]
    out = pl.pallas_call(
        functools.partial(_ret_kernel, seq_len, has_state, rope),
        grid=(grp.n_seq,),
        in_specs=in_specs, out_specs=out_specs, out_shape=out_shapes,
        compiler_params=_params(1),
        name="retention_mixer",
    )(*args)
    return (out, None) if has_state else (out[0], out[1])


def _pad_cols(w, n):
    return jnp.pad(w, ((0, 0), (0, n - w.shape[1])))


def kernel(x_prompt, x_sample, state_gdn, state_ret, state_s5, state_gla, c, c_ctx, ada_w, ada_b, norm_mix,
           norm_ffn, norm_final, ev_w_in, ev_conv, gdn_a_log, gdn_dt_bias, gdn_gain, ret_log_decay, ret_gain,
           ev_w_out, od_w_in, s5_a_re, s5_a_im, s5_log_dt, s5_b_re, s5_b_im, s5_c_re, s5_c_im, s5_d, s5_w_glu,
           gla_w_gate2, gla_b_gate, gla_gain, od_w_out, moe_router, moe_w_gate, moe_w_up, moe_w_down):
    cond = jnp.zeros((COND_ROWS, D_MODEL), F32).at[0].set(c_ctx).at[1:N_COND].set(c)
    mods = _modulation(cond, ada_w, ada_b)
    rope_tables = _rope_tables(DEC_SEQ)

    xs = {CTX: x_prompt.reshape(BATCH * SEQ, D_MODEL), DEC: x_sample.reshape(DEC_BATCH * DEC_SEQ, D_MODEL)}
    new_states = {"gdn": [], "ret": [], "s5": [], "gla": []}

    for layer in range(DEPTH):
        j = layer // 2
        even = layer % 2 == 0
        if even:
            w_in = _regroup_even_cols(ev_w_in[j]).astype(BF16)
            w_out = ev_w_out[j].astype(BF16)
            w_outs = [w_out[:EV_A], w_out[EV_A:]]
        else:
            w_in = _pad_cols(od_w_in[j], ODD_IN_PAD).astype(BF16)
            w_out = od_w_out[j].astype(BF16)
            w_outs = [w_out[:C_WIDTH], w_out[C_WIDTH:]]
            s5_prep = _s5_prepare(s5_a_re[j], s5_a_im[j], s5_log_dt[j], s5_b_re[j], s5_b_im[j], s5_c_re[j],
                                  s5_c_im[j])
        w_router = _pad_cols(moe_router[layer], LANES)
        routed = {}
        for grp in (CTX, DEC):
            x = xs[grp]
            ctx = grp.is_context
            proj = _in_projection(grp, layer, x, mods, norm_mix[layer][None], w_in)
            if even:
                out_a, fin_a = _gdn_mixer(grp, proj, ev_conv[j], gdn_a_log[j], gdn_dt_bias[j], gdn_gain[j],
                                          None if ctx else state_gdn[:, j])
                out_b, fin_b = _ret_mixer(grp, proj, ret_log_decay[j], ret_gain[j],
                                          None if ctx else state_ret[:, j], None if ctx else rope_tables)
                mixes = [out_a, out_b]
                if ctx:
                    new_states["gdn"].append(fin_a)
                    new_states["ret"].append(fin_b)
            else:
                ypre, fin_c = _s5_mixer(grp, proj, s5_prep, None if ctx else _s5_state_to_blocks(state_s5[:, j]))
                out_c, out_d, fin_d = _gla_mixer(grp, proj, ypre, s5_d[j], s5_w_glu[j], gla_w_gate2[j],
                                                 gla_b_gate[j], gla_gain[j], None if ctx else state_gla[:, j])
                mixes = [out_c, out_d]
                if ctx:
                    new_states["s5"].append(_s5_state_from_blocks(fin_c))
                    new_states["gla"].append(fin_d)
            x1, h2, aff = _out_projection(grp, layer, mixes, x, mods, w_outs, norm_ffn[layer][None], w_router)
            xs_g, pt, gate = _route(grp, aff, h2)
            routed[grp] = (x1, xs_g, pt, gate)
        ys_c, ys_d = _experts(layer, routed[CTX][1], routed[DEC][1], routed[CTX][3], routed[DEC][3],
                              moe_w_gate, moe_w_up, moe_w_down)
        for grp, ys in ((CTX, ys_c), (DEC, ys_d)):
            x1, _, pt, _ = routed[grp]
            xs[grp] = _combine(grp, layer, layer == DEPTH - 1, x1, mods, pt, ys, norm_final[None])

    y_prompt = xs[CTX].reshape(BATCH, SEQ, D_MODEL)
    y_sample = xs[DEC].reshape(DEC_BATCH, DEC_SEQ, D_MODEL)
    return (y_prompt, y_sample, jnp.stack(new_states["gdn"], axis=1), jnp.stack(new_states["ret"], axis=1),
            jnp.stack(new_states["s5"], axis=1), jnp.stack(new_states["gla"], axis=1))
```

```python
import functools
import math
from typing import NamedTuple

import numpy as np
import jax
import jax.numpy as jnp
from jax import lax
from jax.experimental import pallas as pl
from jax.experimental.pallas import tpu as pltpu

F32 = jnp.float32
BF16 = jnp.bfloat16

D_MODEL = 1024
BATCH = 32
SEQ = 256
DEPTH = 4
DEC_BATCH = 2
DEC_SEQ = 1024
GRID_W = 64
N_DIR = 2
A_HEADS, A_DK, A_DV, A_CONV, A_CHUNK = 4, 128, 128, 5, 64
B_HEADS, B_DK, B_DV, B_CHUNK = 8, 64, 64, 64
ROPE_FREQS = B_DK // 4
ROPE_BASE = 10000.0
C_GROUP, C_GROUPS, C_STATE = 16, 32, 64
C_WIDTH = C_GROUPS * C_GROUP
S5_MAX_RE = -1e-4
D_HEADS, D_DK, D_DV, D_RANK, D_TAU, D_CHUNK = 4, 64, 128, 16, 16.0, 16
N_EXPERTS = 16
EXPERT_FF = 1024
EC_CAPACITY_FACTOR = 2
RMS_EPS = 1e-6

EVEN_SPLITS = (A_HEADS * A_DK, A_HEADS * A_DK, A_HEADS * A_DV, A_HEADS * A_DV, N_DIR * A_HEADS, N_DIR * A_HEADS,
               B_HEADS * B_DK, B_HEADS * B_DK, B_HEADS * B_DV, B_HEADS * B_DV)
ODD_SPLITS = (C_WIDTH, D_HEADS * D_DK, D_HEADS * D_DK, D_HEADS * D_DV, D_HEADS * D_DV, N_DIR * D_RANK)

LANES = 128
TOKEN_TILE = 256
N_COND = 1 + DEC_BATCH
COND_ROWS = 8
MOD_COL_TILE = 1536
VMEM_LIMIT = 56 * 1024 * 1024


class Group(NamedTuple):
    n_seq: int
    seq_len: int
    is_context: bool

    @property
    def tiles_per_seq(self):
        return self.seq_len // TOKEN_TILE

    @property
    def n_tiles(self):
        return self.n_seq * self.tiles_per_seq

    @property
    def capacity(self):
        return EC_CAPACITY_FACTOR * self.seq_len // N_EXPERTS

    def cond_of_tile(self, i):
        return 0 if self.is_context else 1 + i // self.tiles_per_seq

    def cond_of_seq(self, b):
        return 0 if self.is_context else 1 + b


CTX = Group(BATCH, SEQ, True)
DEC = Group(DEC_BATCH, DEC_SEQ, False)


def _round_up(n, m):
    return (n + m - 1) // m * m


def _params(n_axes):
    return pltpu.CompilerParams(dimension_semantics=("arbitrary",) * n_axes, vmem_limit_bytes=VMEM_LIMIT)


def _mm3(a, b):
    a1 = a.astype(BF16)
    a2 = (a - a1.astype(F32)).astype(BF16)
    b1 = b.astype(BF16)
    b2 = (b - b1.astype(F32)).astype(BF16)
    dot = lambda x, y: jnp.dot(x, y, preferred_element_type=F32)
    return dot(a1, b1) + (dot(a1, b2) + dot(a2, b1))


def _mod_kernel(c_ref, w_ref, b_ref, o_ref):
    c = c_ref[...]
    s = c * jax.nn.sigmoid(c)
    o_ref[0] = jnp.dot(s.astype(BF16), w_ref[0].astype(BF16), preferred_element_type=F32) + b_ref[0]


def _modulation(cond, ada_w, ada_b):
    n_out = 6 * D_MODEL
    out = pl.pallas_call(
        _mod_kernel,
        grid=(DEPTH, n_out // MOD_COL_TILE),
        in_specs=[pl.BlockSpec((COND_ROWS, D_MODEL), lambda l, j: (0, 0)),
                  pl.BlockSpec((1, D_MODEL, MOD_COL_TILE), lambda l, j: (l, 0, j)),
                  pl.BlockSpec((1, 1, MOD_COL_TILE), lambda l, j: (l, 0, j))],
        out_specs=pl.BlockSpec((1, COND_ROWS, MOD_COL_TILE), lambda l, j: (l, 0, j)),
        out_shape=jax.ShapeDtypeStruct((DEPTH, COND_ROWS, n_out), F32),
        compiler_params=_params(2),
        name="adaln_modulation",
    )(cond, ada_w, ada_b.reshape(DEPTH, 1, n_out))
    out = out[:, :N_COND].reshape(DEPTH, N_COND, 6, D_MODEL)
    return out.transpose(1, 0, 2, 3).reshape(N_COND, DEPTH * 6, D_MODEL)


def _norm_mod(x, gain, shift, scale):
    y = x * lax.rsqrt(jnp.mean(x * x, axis=-1, keepdims=True) + RMS_EPS)
    return (y * gain) * (1.0 + scale) + shift


def _mod_row(mod_ref, layer, k):
    r = 6 * layer + k
    return mod_ref[r:r + 1, :]


def _inproj_kernel(layer, x_ref, mod_ref, gain_ref, w_ref, o_ref):
    h = _norm_mod(x_ref[...], gain_ref[...], _mod_row(mod_ref, layer, 0), _mod_row(mod_ref, layer, 1))
    o_ref[...] = jnp.dot(h.astype(BF16), w_ref[...], preferred_element_type=F32)


def _in_projection(grp, layer, x, mods, gain, w):
    n_tok = grp.n_seq * grp.seq_len
    n_out = w.shape[1]
    return pl.pallas_call(
        functools.partial(_inproj_kernel, layer),
        grid=(grp.n_tiles,),
        in_specs=[pl.BlockSpec((TOKEN_TILE, D_MODEL), lambda i: (i, 0)),
                  pl.BlockSpec((None, DEPTH * 6, D_MODEL), lambda i: (grp.cond_of_tile(i), 0, 0)),
                  pl.BlockSpec((1, D_MODEL), lambda i: (0, 0)),
                  pl.BlockSpec((D_MODEL, n_out), lambda i: (0, 0))],
        out_specs=pl.BlockSpec((TOKEN_TILE, n_out), lambda i: (i, 0)),
        out_shape=jax.ShapeDtypeStruct((n_tok, n_out), F32),
        compiler_params=_params(1),
        name="in_projection",
    )(x, mods, gain, w)


def _outproj_kernel(layer, n_parts, *refs):
    mix_refs, wout_refs = refs[:n_parts], refs[n_parts:2 * n_parts]
    x_ref, mod_ref, gain_ref, wr_ref, x1_ref, h2_ref, aff_ref = refs[2 * n_parts:]
    y = jnp.dot(mix_refs[0][...].astype(BF16), wout_refs[0][...], preferred_element_type=F32)
    for m_ref, w_ref in zip(mix_refs[1:], wout_refs[1:]):
        y = y + jnp.dot(m_ref[...].astype(BF16), w_ref[...], preferred_element_type=F32)
    x1 = x_ref[...] + _mod_row(mod_ref, layer, 2) * y
    x1_ref[...] = x1
    h2 = _norm_mod(x1, gain_ref[...], _mod_row(mod_ref, layer, 3), _mod_row(mod_ref, layer, 4))
    h2_ref[...] = h2.astype(BF16)
    logits = _mm3(h2, wr_ref[...])
    lane = lax.broadcasted_iota(jnp.int32, logits.shape, 1)
    logits = jnp.where(lane < N_EXPERTS, logits, -jnp.inf)
    e = jnp.exp(logits - jnp.max(logits, axis=-1, keepdims=True))
    aff_ref[...] = e / jnp.sum(e, axis=-1, keepdims=True)


def _out_projection(grp, layer, mixes, x, mods, w_outs, gain, w_router):
    n_tok = grp.n_seq * grp.seq_len
    tile = lambda width: pl.BlockSpec((TOKEN_TILE, width), lambda i: (i, 0))
    return pl.pallas_call(
        functools.partial(_outproj_kernel, layer, len(mixes)),
        grid=(grp.n_tiles,),
        in_specs=[tile(m.shape[1]) for m in mixes]
                 + [pl.BlockSpec(w.shape, lambda i: (0, 0)) for w in w_outs]
                 + [tile(D_MODEL),
                    pl.BlockSpec((None, DEPTH * 6, D_MODEL), lambda i: (grp.cond_of_tile(i), 0, 0)),
                    pl.BlockSpec((1, D_MODEL), lambda i: (0, 0)),
                    pl.BlockSpec((D_MODEL, LANES), lambda i: (0, 0))],
        out_specs=[tile(D_MODEL), tile(D_MODEL), tile(LANES)],
        out_shape=[jax.ShapeDtypeStruct((n_tok, D_MODEL), F32),
                   jax.ShapeDtypeStruct((n_tok, D_MODEL), BF16),
                   jax.ShapeDtypeStruct((n_tok, LANES), F32)],
        compiler_params=_params(1),
        name="out_projection_router",
    )(*mixes, *w_outs, x, mods, gain, w_router)


def _route_kernel(n, cap, aff_ref, h2_ref, xs_ref, pt_ref, gate_ref):
    aff = aff_ref[...]
    aff_t = aff.T
    t_sub = lax.broadcasted_iota(jnp.int32, (n, n), 0)
    t_lane = lax.broadcasted_iota(jnp.int32, (n, n), 1)
    earlier = t_sub < t_lane
    sel_rows = []
    for e in range(N_EXPERTS):
        col = aff[:, e:e + 1]
        row = aff_t[e:e + 1, :]
        beats = (col > row) | ((col == row) & earlier)
        rank = jnp.sum(jnp.where(beats, 1.0, 0.0), axis=0, keepdims=True)
        sel_rows.append(jnp.where(rank < cap, 1.0, 0.0))
    sel = jnp.concatenate(sel_rows, axis=0)
    pos = jnp.dot(sel.astype(BF16), jnp.where(earlier, 1.0, 0.0).astype(BF16), preferred_element_type=F32)
    e_idx = lax.broadcasted_iota(jnp.int32, (N_EXPERTS, n), 0)
    slot = jnp.where(sel > 0.0, pos.astype(jnp.int32) + e_idx * cap, -1)
    slot_pad = jnp.concatenate([slot, jnp.full((LANES - N_EXPERTS, n), -1, jnp.int32)], axis=0)
    slot_t = slot_pad.astype(F32).T.astype(jnp.int32)
    per_block = LANES // cap
    s_lane = lax.broadcasted_iota(jnp.int32, (n, LANES), 1)
    for blk in range(N_EXPERTS // per_block):
        hit = jnp.zeros((n, LANES), F32)
        for e in range(blk * per_block, (blk + 1) * per_block):
            hit = hit + jnp.where(slot_t[:, e:e + 1] == s_lane + blk * LANES, 1.0, 0.0)
        pt_ref[:, blk * LANES:(blk + 1) * LANES] = hit.astype(BF16)
    c_sub = lax.broadcasted_iota(jnp.int32, (cap, n), 0)
    picks = []
    for e in range(N_EXPERTS):
        p_e = jnp.where(slot[e:e + 1, :] == c_sub + e * cap, 1.0, 0.0)
        picks.append(p_e.astype(BF16))
        gate = jnp.sum(p_e * aff_t[e:e + 1, :], axis=1, keepdims=True)
        gate_ref[e] = jnp.broadcast_to(gate, (cap, LANES))
    xs = jnp.dot(jnp.concatenate(picks, axis=0), h2_ref[...], preferred_element_type=F32).astype(BF16)
    for e in range(N_EXPERTS):
        xs_ref[e] = xs[e * cap:(e + 1) * cap]


def _route(grp, aff, h2):
    n, cap = grp.seq_len, grp.capacity
    slots = N_EXPERTS * cap
    return pl.pallas_call(
        functools.partial(_route_kernel, n, cap),
        grid=(grp.n_seq,),
        in_specs=[pl.BlockSpec((n, LANES), lambda b: (b, 0)),
                  pl.BlockSpec((n, D_MODEL), lambda b: (b, 0))],
        out_specs=[pl.BlockSpec((N_EXPERTS, cap, D_MODEL), lambda b: (0, b, 0)),
                   pl.BlockSpec((n, slots), lambda b: (b, 0)),
                   pl.BlockSpec((N_EXPERTS, cap, LANES), lambda b: (0, b, 0))],
        out_shape=[jax.ShapeDtypeStruct((N_EXPERTS, grp.n_seq * cap, D_MODEL), BF16),
                   jax.ShapeDtypeStruct((grp.n_seq * n, slots), BF16),
                   jax.ShapeDtypeStruct((N_EXPERTS, grp.n_seq * cap, LANES), F32)],
        compiler_params=_params(1),
        name="expert_choice_route",
    )(aff, h2)


FF_TILE = 512


def _expert_kernel(xc_ref, xd_ref, gc_ref, gd_ref, wg_ref, wu_ref, wd_ref, yc_ref, yd_ref, accc_ref, accd_ref):
    f = pl.program_id(1)
    wg = wg_ref[...].astype(BF16)
    wu = wu_ref[...].astype(BF16)
    wd = wd_ref[...].astype(BF16)

    def ffn(x_ref, acc_ref):
        x = x_ref[...]
        a = jnp.dot(x, wg, preferred_element_type=F32)
        u = jnp.dot(x, wu, preferred_element_type=F32)
        hid = (a * jax.nn.sigmoid(a)) * u
        y = jnp.dot(hid.astype(BF16), wd, preferred_element_type=F32)

        @pl.when(f == 0)
        def _():
            acc_ref[...] = y

        @pl.when(f != 0)
        def _():
            acc_ref[...] += y

    ffn(xc_ref, accc_ref)
    ffn(xd_ref, accd_ref)

    @pl.when(f == pl.num_programs(1) - 1)
    def _():
        yc_ref[...] = (accc_ref[...] * gc_ref[:, 0:1]).astype(BF16)
        yd_ref[...] = (accd_ref[...] * gd_ref[:, 0:1]).astype(BF16)


def _experts(layer, xs_c, xs_d, gate_c, gate_d, w_gate, w_up, w_down):
    rc, rd = xs_c.shape[1], xs_d.shape[1]
    per_e = lambda rows, width: pl.BlockSpec((None, rows, width), lambda e, f: (e, 0, 0))
    return pl.pallas_call(
        _expert_kernel,
        grid=(N_EXPERTS, EXPERT_FF // FF_TILE),
        in_specs=[per_e(rc, D_MODEL), per_e(rd, D_MODEL), per_e(rc, LANES), per_e(rd, LANES),
                  pl.BlockSpec((None, None, D_MODEL, FF_TILE), lambda e, f: (layer, e, 0, f)),
                  pl.BlockSpec((None, None, D_MODEL, FF_TILE), lambda e, f: (layer, e, 0, f)),
                  pl.BlockSpec((None, None, FF_TILE, D_MODEL), lambda e, f: (layer, e, f, 0))],
        out_specs=[per_e(rc, D_MODEL), per_e(rd, D_MODEL)],
        out_shape=[jax.ShapeDtypeStruct(xs_c.shape, BF16), jax.ShapeDtypeStruct(xs_d.shape, BF16)],
        scratch_shapes=[pltpu.VMEM((rc, D_MODEL), F32), pltpu.VMEM((rd, D_MODEL), F32)],
        compiler_params=_params(2),
        name="expert_swiglu",
    )(xs_c, xs_d, gate_c, gate_d, w_gate, w_up, w_down)


def _combine_kernel(layer, cap, final, x1_ref, mod_ref, pt_ref, ys_ref, gain_ref, x2_ref):
    ys = ys_ref[...].reshape(N_EXPERTS * cap, D_MODEL)
    y = jnp.dot(pt_ref[...], ys, preferred_element_type=F32)
    x2 = x1_ref[...] + _mod_row(mod_ref, layer, 5) * y
    if final:
        x2 = x2 * lax.rsqrt(jnp.mean(x2 * x2, axis=-1, keepdims=True) + RMS_EPS) * gain_ref[...]
    x2_ref[...] = x2


def _combine(grp, layer, final, x1, mods, pt, ys, gain_final):
    n, cap = grp.seq_len, grp.capacity
    return pl.pallas_call(
        functools.partial(_combine_kernel, layer, cap, final),
        grid=(grp.n_seq,),
        in_specs=[pl.BlockSpec((n, D_MODEL), lambda b: (b, 0)),
                  pl.BlockSpec((None, DEPTH * 6, D_MODEL), lambda b: (grp.cond_of_seq(b), 0, 0)),
                  pl.BlockSpec((n, N_EXPERTS * cap), lambda b: (b, 0)),
                  pl.BlockSpec((N_EXPERTS, cap, D_MODEL), lambda b: (0, b, 0)),
                  pl.BlockSpec((1, D_MODEL), lambda b: (0, 0))],
        out_specs=pl.BlockSpec((n, D_MODEL), lambda b: (b, 0)),
        out_shape=jax.ShapeDtypeStruct(x1.shape, F32),
        compiler_params=_params(1),
        name="expert_combine",
    )(x1, mods, pt, ys, gain_final)


S5_BLOCKS = 2
S5_BLOCK_IN = C_WIDTH // S5_BLOCKS
S5_BLOCK_STATE = C_GROUPS * C_STATE // S5_BLOCKS
S5_SUBLANES = 8
S5_SCAN_COLS = 256
S5_SHIFTS = (1, 2, 4)


def _cmul(ar, ai, br, bi):
    return ar * br - ai * bi, ar * bi + ai * br


def _s5_prep_kernel(are_ref, aim_ref, ldt_ref, bre_ref, bim_ref, cre_ref, cim_ref, bmat_ref, cmat_ref, const_ref):
    n = S5_BLOCK_STATE
    row = lax.broadcasted_iota(jnp.int32, (S5_SUBLANES, n), 0)
    for d in range(N_DIR):
        a_re = jnp.minimum(are_ref[d], S5_MAX_RE)
        a_im = aim_ref[d]
        dt = jnp.exp(ldt_ref[d])
        mag = jnp.exp(a_re * dt)
        l_re = mag * jnp.cos(a_im * dt)
        l_im = mag * jnp.sin(a_im * dt)
        den = a_re * a_re + a_im * a_im
        k_re = ((l_re - 1.0) * a_re + l_im * a_im) / den
        k_im = (l_im * a_re - (l_re - 1.0) * a_im) / den
        b_re, b_im = bre_ref[...], bim_ref[...]
        bb_re, bb_im = _cmul(k_re, k_im, b_re, b_im)
        bmat_ref[d] = jnp.concatenate([bb_re, bb_im], axis=1).astype(BF16)
        pows = [(l_re, l_im)]
        for _ in range(S5_SUBLANES - 1):
            pows.append(_cmul(pows[-1][0], pows[-1][1], l_re, l_im))
        for i, s in enumerate(S5_SHIFTS):
            keep = (row >= s) if d == 0 else (row <= S5_SUBLANES - 1 - s)
            const_ref[d, 2 * i] = jnp.where(keep, pows[s - 1][0], 0.0)
            const_ref[d, 2 * i + 1] = jnp.where(keep, pows[s - 1][1], 0.0)
        lp_re = jnp.zeros((S5_SUBLANES, n), F32)
        lp_im = jnp.zeros((S5_SUBLANES, n), F32)
        for r in range(S5_SUBLANES):
            p = pows[r] if d == 0 else pows[S5_SUBLANES - 1 - r]
            lp_re = jnp.where(row == r, p[0], lp_re)
            lp_im = jnp.where(row == r, p[1], lp_im)
        const_ref[d, 6] = lp_re
        const_ref[d, 7] = lp_im
    cmat_ref[...] = jnp.concatenate([cre_ref[...], -cim_ref[...]], axis=0).astype(BF16)


def _s5_prepare(a_re, a_im, log_dt, b_re, b_im, c_re, c_im):
    n, k_in = S5_BLOCK_STATE, S5_BLOCK_IN
    gpb = C_GROUPS // S5_BLOCKS
    per_state = lambda t: t.reshape(N_DIR, S5_BLOCKS, 1, n)
    ldt = jnp.repeat(log_dt, C_STATE, axis=-1)
    eye = jnp.eye(gpb, dtype=F32)

    def expand_b(b):
        b = b.reshape(S5_BLOCKS, gpb, C_STATE, C_GROUP).transpose(0, 1, 3, 2)
        return (b[:, :, :, None, :] * eye[None, :, None, :, None]).reshape(S5_BLOCKS, k_in, n)

    def expand_c(c):
        c = c.reshape(S5_BLOCKS, gpb, C_GROUP, C_STATE).transpose(0, 1, 3, 2)
        return (c[:, :, :, None, :] * eye[None, :, None, :, None]).reshape(S5_BLOCKS, n, k_in)

    row_spec = pl.BlockSpec((N_DIR, None, 1, n), lambda k: (0, k, 0, 0))
    return pl.pallas_call(
        _s5_prep_kernel,
        grid=(S5_BLOCKS,),
        in_specs=[row_spec, row_spec, row_spec,
                  pl.BlockSpec((None, k_in, n), lambda k: (k, 0, 0)),
                  pl.BlockSpec((None, k_in, n), lambda k: (k, 0, 0)),
                  pl.BlockSpec((None, n, k_in), lambda k: (k, 0, 0)),
                  pl.BlockSpec((None, n, k_in), lambda k: (k, 0, 0))],
        out_specs=[pl.BlockSpec((N_DIR, None, k_in, 2 * n), lambda k: (0, k, 0, 0)),
                   pl.BlockSpec((None, 2 * n, k_in), lambda k: (k, 0, 0)),
                   pl.BlockSpec((N_DIR, None, 8, S5_SUBLANES, n), lambda k: (0, k, 0, 0, 0))],
        out_shape=[jax.ShapeDtypeStruct((N_DIR, S5_BLOCKS, k_in, 2 * n), BF16),
                   jax.ShapeDtypeStruct((S5_BLOCKS, 2 * n, k_in), BF16),
                   jax.ShapeDtypeStruct((N_DIR, S5_BLOCKS, 8, S5_SUBLANES, n), F32)],
        compiler_params=_params(1),
        name="s5_prepare",
    )(per_state(a_re), per_state(a_im), per_state(ldt), expand_b(b_re), expand_b(b_im), expand_c(c_re),
      expand_c(c_im))


def _s5_kernel(seq_len, has_state, u_ref, bmat_ref, cmat_ref, const_ref, *rest):
    if has_state:
        s0_ref, y_ref, fin_ref, xre_ref, xim_ref = rest
    else:
        y_ref, fin_ref, xre_ref, xim_ref = rest
    n = S5_BLOCK_STATE
    n_tiles = seq_len // S5_SUBLANES
    u = u_ref[...].astype(BF16)
    for d in range(N_DIR):
        xre_ref[d] = jnp.dot(u, bmat_ref[d, :, :n], preferred_element_type=F32)
        xim_ref[d] = jnp.dot(u, bmat_ref[d, :, n:], preferred_element_type=F32)

    for cb in range(n // S5_SCAN_COLS):
        cols = slice(cb * S5_SCAN_COLS, (cb + 1) * S5_SCAN_COLS)

        def scan_tile(d, i, carry):
            rows = pl.ds(pl.multiple_of(i * S5_SUBLANES, S5_SUBLANES), S5_SUBLANES)
            xr = xre_ref[d, rows, cols]
            xi = xim_ref[d, rows, cols]
            for k, s in enumerate(S5_SHIFTS):
                shift = s if d == 0 else S5_SUBLANES - s
                pr, pi = _cmul(const_ref[d, 2 * k, :, cols], const_ref[d, 2 * k + 1, :, cols],
                               pltpu.roll(xr, shift, 0), pltpu.roll(xi, shift, 0))
                xr, xi = xr + pr, xi + pi
            cr, ci = _cmul(const_ref[d, 6, :, cols], const_ref[d, 7, :, cols], carry[0], carry[1])
            xr, xi = xr + cr, xi + ci
            xre_ref[d, rows, cols] = xr
            xim_ref[d, rows, cols] = xi
            edge = S5_SUBLANES - 1 if d == 0 else 0
            shape = (S5_SUBLANES, S5_SCAN_COLS)
            return (jnp.broadcast_to(xr[edge:edge + 1], shape), jnp.broadcast_to(xi[edge:edge + 1], shape))

        def body(i, carry):
            return (scan_tile(0, i, carry[0]), scan_tile(1, n_tiles - 1 - i, carry[1]))

        shape = (S5_SUBLANES, S5_SCAN_COLS)
        if has_state:
            init = tuple((jnp.broadcast_to(s0_ref[2 * d:2 * d + 1, cols], shape),
                          jnp.broadcast_to(s0_ref[2 * d + 1:2 * d + 2, cols], shape)) for d in range(N_DIR))
        else:
            init = tuple((jnp.zeros(shape, F32), jnp.zeros(shape, F32)) for _ in range(N_DIR))
        fin = lax.fori_loop(0, n_tiles, body, init)
        for d in range(N_DIR):
            fin_ref[2 * d:2 * d + 1, cols] = fin[d][0][0:1]
            fin_ref[2 * d + 1:2 * d + 2, cols] = fin[d][1][0:1]

    x_re = (xre_ref[0] + xre_ref[1]).astype(BF16)
    x_im = (xim_ref[0] + xim_ref[1]).astype(BF16)
    y_ref[...] = (jnp.dot(x_re, cmat_ref[:n], preferred_element_type=F32)
                  + jnp.dot(x_im, cmat_ref[n:], preferred_element_type=F32))


def _s5_mixer(grp, proj, prep, s0):
    bmat, cmat, consts = prep
    n, k_in, seq_len = S5_BLOCK_STATE, S5_BLOCK_IN, grp.seq_len
    has_state = s0 is not None
    state_spec = pl.BlockSpec((None, None, 2 * N_DIR, n), lambda k, b: (b, k, 0, 0))
    in_specs = [pl.BlockSpec((seq_len, k_in), lambda k, b: (b, k)),
                pl.BlockSpec((N_DIR, None, k_in, 2 * n), lambda k, b: (0, k, 0, 0)),
                pl.BlockSpec((None, 2 * n, k_in), lambda k, b: (k, 0, 0)),
                pl.BlockSpec((N_DIR, None, 8, S5_SUBLANES, n), lambda k, b: (0, k, 0, 0, 0))]
    args = [proj, bmat, cmat, consts]
    if has_state:
        in_specs.append(state_spec)
        args.append(s0)
    return pl.pallas_call(
        functools.partial(_s5_kernel, seq_len, has_state),
        grid=(S5_BLOCKS, grp.n_seq),
        in_specs=in_specs,
        out_specs=[pl.BlockSpec((seq_len, k_in), lambda k, b: (b, k)), state_spec],
        out_shape=[jax.ShapeDtypeStruct((grp.n_seq * seq_len, C_WIDTH), F32),
                   jax.ShapeDtypeStruct((grp.n_seq, S5_BLOCKS, 2 * N_DIR, n), F32)],
        scratch_shapes=[pltpu.VMEM((N_DIR, seq_len, n), F32), pltpu.VMEM((N_DIR, seq_len, n), F32)],
        compiler_params=_params(2),
        name="s5_scan",
    )(*args)


def _s5_state_to_blocks(s):
    b = s.shape[0]
    s = s.reshape(b, N_DIR, S5_BLOCKS, S5_BLOCK_STATE, 2).transpose(0, 2, 1, 4, 3)
    return s.reshape(b, S5_BLOCKS, 2 * N_DIR, S5_BLOCK_STATE)


def _s5_state_from_blocks(s):
    b = s.shape[0]
    s = s.reshape(b, S5_BLOCKS, N_DIR, 2, S5_BLOCK_STATE).transpose(0, 2, 1, 4, 3)
    return s.reshape(b, N_DIR, C_GROUPS, C_STATE, 2)


ODD_IN_PAD = _round_up(sum(ODD_SPLITS), LANES)
_O_U, _O_Q, _O_K, _O_V, _O_G, _O_A = (int(v) for v in np.cumsum((0,) + ODD_SPLITS[:-1]))
GLA_QK = D_HEADS * D_DK
GLA_V = D_HEADS * D_DV
GLA_MIN_SUPER = 256
GLA_REF_ROWS = 128
GLA_SUBLANES = 8


def _split3(x):
    x1 = x.astype(BF16)
    r1 = x - x1.astype(F32)
    x2 = r1.astype(BF16)
    x3 = (r1 - x2.astype(F32)).astype(BF16)
    return x1, x2, x3


def _dot_01(m01, x):
    p1, p2, p3 = _split3(x)
    dot = lambda p: jnp.dot(m01, p, preferred_element_type=F32)
    return dot(p1) + dot(p2) + dot(p3)


def _dot_nt(a, b):
    return lax.dot_general(a, b, (((1,), (1,)), ((), ())), preferred_element_type=F32)


def _one_hot(cond):
    return jnp.where(cond, 1.0, 0.0).astype(BF16)


def _gla_kernel(seq_len, has_state, heads, u_ref, q_ref, k_ref, v_ref, g_ref, a_ref, ypre_ref, dskip_ref, wglu_ref,
                wg2_ref, bgate_ref, gain_ref, *rest):
    if has_state:
        s0_ref, outc_ref, outd_ref, p_s = rest
    else:
        outc_ref, outd_ref, fin_ref, p_s = rest
    n = seq_len

    @pl.when(pl.program_id(1) == 0)
    def _():
        y = jax.nn.gelu(ypre_ref[...] + dskip_ref[...] * u_ref[...])
        glu = jnp.dot(y.astype(BF16), wglu_ref[...], preferred_element_type=F32)
        outc_ref[...] = y * jax.nn.sigmoid(glu)

    q = q_ref[...] * (D_DK ** -0.5)
    k = k_ref[...]
    r_i = lax.broadcasted_iota(jnp.int32, (n, n), 0)
    c_i = lax.broadcasted_iota(jnp.int32, (n, n), 1)
    b = []
    for d in range(N_DIR):
        logits = jnp.dot(a_ref[...], wg2_ref[d], preferred_element_type=F32,
                         precision=lax.Precision.HIGHEST) + bgate_ref[d]
        la = (jnp.minimum(logits, 0.0) - jnp.log(1.0 + jnp.exp(-jnp.abs(logits)))) / D_TAU
        b.append(_dot_01(_one_hot((r_i >= c_i) if d == 0 else (r_i <= c_i)), la))

    lane_p = lax.broadcasted_iota(jnp.int32, (1, LANES), 1)

    def pair_lanes(x, h):
        return x[:, (h // 2) * LANES:(h // 2 + 1) * LANES]

    def head_only(x, h):
        mine = (lane_p < D_DK) if h % 2 == 0 else (lane_p >= D_DK)
        return jnp.where(mine, pair_lanes(x, h), 0.0).astype(BF16)

    for h in range(heads):
        s = _dot_nt(head_only(q, h), pair_lanes(k, h).astype(BF16))
        p_s[h] = jnp.where(r_i == c_i, 2.0 * s, 0.0)

    for lb in range(int(math.log2(n))):
        half = 1 << lb
        width = min(max(2 * half, GLA_MIN_SUPER), n)
        w_r = lax.broadcasted_iota(jnp.int32, (width, width), 0)
        w_c = lax.broadcasted_iota(jnp.int32, (width, width), 1)
        at_level = ((w_r ^ w_c) >> lb) == 1
        later = w_r > w_c
        qt, kt = [], []
        for d in range(N_DIR):
            edge = half - 1 if d == 0 else half
            if 2 * half >= GLA_SUBLANES:
                blocks = b[d].reshape(n // (2 * half), 2 * half, b[d].shape[1])
                ref = jnp.broadcast_to(blocks[:, edge:edge + 1, :], blocks.shape).reshape(b[d].shape)
            elif n <= GLA_MIN_SUPER:
                sel = c_i == ((r_i >> (lb + 1)) << (lb + 1)) + edge
                ref = _dot_01(_one_hot(sel), b[d])
            else:
                n_ref = max(n >> (lb + 1), GLA_REF_ROWS)
                blk = lax.broadcasted_iota(jnp.int32, (n_ref, n), 0)
                tok = lax.broadcasted_iota(jnp.int32, (n_ref, n), 1)
                per_block = _dot_01(_one_hot(tok == (blk << (lb + 1)) + edge), b[d])
                tok_r = lax.broadcasted_iota(jnp.int32, (n, n_ref), 0)
                blk_c = lax.broadcasted_iota(jnp.int32, (n, n_ref), 1)
                ref = _dot_01(_one_hot(blk_c == (tok_r >> (lb + 1))), per_block)
            qt.append(q * jnp.exp(jnp.minimum(b[d] - ref, 0.0)))
            kt.append(k * jnp.exp(jnp.minimum(ref - b[d], 0.0)))
        for sb in range(n // width):
            rows = slice(sb * width, (sb + 1) * width)
            for h in range(heads):
                s_f = _dot_nt(head_only(qt[0][rows], h), pair_lanes(kt[0][rows], h).astype(BF16))
                s_b = _dot_nt(head_only(qt[1][rows], h), pair_lanes(kt[1][rows], h).astype(BF16))
                p_s[h, rows, rows] = p_s[h, rows, rows] + jnp.where(at_level, jnp.where(later, s_f, s_b), 0.0)

    zeros = jnp.zeros((D_DK, D_DV), F32)
    for h in range(heads):
        vh = v_ref[:, h * D_DV:(h + 1) * D_DV].astype(BF16)
        o = jnp.dot(p_s[h].astype(BF16), vh, preferred_element_type=F32)
        if has_state:
            for d in range(N_DIR):
                qs = pair_lanes(q * jnp.exp(b[d]), h).astype(BF16)
                s0 = s0_ref[d, h]
                s0_pad = jnp.concatenate([s0, zeros] if h % 2 == 0 else [zeros, s0], axis=0).astype(BF16)
                o = o + jnp.dot(qs, s0_pad, preferred_element_type=F32)
        o = o * lax.rsqrt(jnp.mean(o * o, axis=-1, keepdims=True) + RMS_EPS) * gain_ref[...]
        gate = g_ref[:, h * D_DV:(h + 1) * D_DV]
        outd_ref[:, h * D_DV:(h + 1) * D_DV] = o * (gate * jax.nn.sigmoid(gate))

    if not has_state:
        for d in range(N_DIR):
            last = n - 1 if d == 0 else 0
            k_out_t = (k * jnp.exp(b[d][last:last + 1] - b[d])).T.astype(BF16)
            for h in range(heads):
                vh = v_ref[:, h * D_DV:(h + 1) * D_DV].astype(BF16)
                fin_ref[d, h] = jnp.dot(k_out_t[h * D_DK:(h + 1) * D_DK], vh, preferred_element_type=F32)


def _gla_mixer(grp, proj, ypre, d_skip, w_glu, w_gate2, b_gate, gain, s0):
    seq_len = grp.seq_len
    has_state = s0 is not None
    heads = D_HEADS if seq_len <= GLA_MIN_SUPER else 2
    groups = D_HEADS // heads
    n_qk, n_v = heads * D_DK, heads * D_DV
    wg2 = jnp.zeros((N_DIR, LANES, GLA_QK), F32)
    for d in range(N_DIR):
        wg2 = wg2.at[d, d * D_RANK:(d + 1) * D_RANK].set(w_gate2[d])
    full = lambda *shape: pl.BlockSpec(shape, lambda b, g: (0,) * len(shape))
    cols = lambda width, offset: pl.BlockSpec((seq_len, width), lambda b, g: (b, offset // width + g))
    fixed = lambda width, offset: pl.BlockSpec((seq_len, width), lambda b, g: (b, offset // width))
    state_spec = pl.BlockSpec((None, N_DIR, heads, D_DK, D_DV), lambda b, g: (b, 0, g, 0, 0))
    in_specs = [fixed(C_WIDTH, _O_U), cols(n_qk, _O_Q), cols(n_qk, _O_K), cols(n_v, _O_V), cols(n_v, _O_G),
                fixed(LANES, _O_A), fixed(C_WIDTH, 0),
                full(1, C_WIDTH), full(C_WIDTH, C_WIDTH),
                pl.BlockSpec((N_DIR, LANES, n_qk), lambda b, g: (0, 0, g)),
                pl.BlockSpec((N_DIR, 1, n_qk), lambda b, g: (0, 0, g)),
                full(1, D_DV)]
    args = [proj] * 6 + [ypre, d_skip[None], w_glu.astype(BF16), wg2, b_gate.reshape(N_DIR, 1, GLA_QK), gain[None]]
    n_tok = grp.n_seq * seq_len
    out_specs = [pl.BlockSpec((seq_len, C_WIDTH), lambda b, g: (b, 0)),
                 pl.BlockSpec((seq_len, n_v), lambda b, g: (b, g))]
    out_shape = [jax.ShapeDtypeStruct((n_tok, C_WIDTH), F32), jax.ShapeDtypeStruct((n_tok, GLA_V), F32)]
    if has_state:
        in_specs.append(state_spec)
        args.append(s0)
    else:
        out_specs.append(state_spec)
        out_shape.append(jax.ShapeDtypeStruct((grp.n_seq, N_DIR, D_HEADS, D_DK, D_DV), F32))
    out = pl.pallas_call(
        functools.partial(_gla_kernel, seq_len, has_state, heads),
        grid=(grp.n_seq, groups),
        in_specs=in_specs, out_specs=out_specs, out_shape=out_shape,
        scratch_shapes=[pltpu.VMEM((heads, seq_len, seq_len), F32)],
        compiler_params=_params(2),
        name="gla_mixer",
    )(*args)
    return (out[0], out[1], None) if has_state else tuple(out)


EV_A = A_HEADS * A_DK
EV_B = B_HEADS * B_DK
EV_HALF = 4 * EV_A
EVEN_IN_PAD = 2 * EV_HALF + LANES
GDN_PAIR = 2 * A_CHUNK
GDN_LEVELS = 5


def _regroup_even_cols(w):
    small = 2 * N_DIR * A_HEADS
    pad = jnp.zeros((w.shape[0], EVEN_IN_PAD - 2 * EV_HALF - small), w.dtype)
    return jnp.concatenate([w[:, :EV_HALF], w[:, EV_HALF + small:], w[:, EV_HALF:EV_HALF + small], pad], axis=1)


def _softplus(x):
    return jnp.maximum(x, 0.0) + jnp.log(1.0 + jnp.exp(-jnp.abs(x)))


def _silu(x):
    return x * jax.nn.sigmoid(x)


def _gdn_kernel(seq_len, has_state, proj_ref, small_ref, convw_ref, alog_ref, dtb_ref, gain_ref, *rest):
    if has_state:
        s0_ref, out_ref, q_s, k_s, v_s, gb_s, gcum_s, o_s, st_s = rest
    else:
        out_ref, fin_ref, q_s, k_s, v_s, gb_s, gcum_s, o_s, st_s = rest
    n_pairs = seq_len // GDN_PAIR
    width = 3 * EV_A

    x = proj_ref[:, 0:width]
    t_idx = lax.broadcasted_iota(jnp.int32, (seq_len, 1), 0)
    pad = A_CONV // 2
    acc = x * convw_ref[pad:pad + 1, :]
    for s in range(-pad, pad + 1):
        if s == 0:
            continue
        shifted = pltpu.roll(x, (-s) % seq_len, 0)
        inside = (t_idx + s >= 0) & (t_idx + s < seq_len)
        acc = acc + jnp.where(inside, shifted, 0.0) * convw_ref[pad + s:pad + s + 1, :]
    y = _silu(acc)
    for h in range(A_HEADS):
        cols = slice(h * A_DK, (h + 1) * A_DK)
        qh = y[:, h * A_DK:(h + 1) * A_DK]
        kh = y[:, EV_A + h * A_DK:EV_A + (h + 1) * A_DK]
        q_s[:, cols] = qh * lax.rsqrt(jnp.sum(qh * qh, axis=-1, keepdims=True) + 1e-6) * (A_DK ** -0.5)
        k_s[:, cols] = kh * lax.rsqrt(jnp.sum(kh * kh, axis=-1, keepdims=True) + 1e-6)
    v_s[...] = y[:, 2 * EV_A:3 * EV_A]

    small = small_ref[...]
    lane = lax.broadcasted_iota(jnp.int32, small.shape, 1)
    gb = jnp.where(lane < N_DIR * A_HEADS, jax.nn.sigmoid(small),
                   -jnp.exp(alog_ref[...]) * _softplus(small + dtb_ref[...]))
    gb_s[...] = gb
    r_i = lax.broadcasted_iota(jnp.int32, (seq_len, seq_len), 0)
    c_i = lax.broadcasted_iota(jnp.int32, (seq_len, seq_len), 1)
    same = (r_i // A_CHUNK) == (c_i // A_CHUNK)
    gcum_s[0] = _dot_01(jnp.where(same & (r_i >= c_i), 1.0, 0.0).astype(BF16), gb)
    gcum_s[1] = _dot_01(jnp.where(same & (r_i <= c_i), 1.0, 0.0).astype(BF16), gb)

    o_s[...] = jnp.zeros(o_s.shape, F32)
    for d in range(N_DIR):
        for h in range(A_HEADS):
            st_s[d, h] = s0_ref[d, h] if has_state else jnp.zeros((A_DK, A_DV), F32)

    pr = lax.broadcasted_iota(jnp.int32, (GDN_PAIR, GDN_PAIR), 0)
    pc = lax.broadcasted_iota(jnp.int32, (GDN_PAIR, GDN_PAIR), 1)
    p_same = (pr // A_CHUNK) == (pc // A_CHUNK)
    eye = jnp.where(pr == pc, 1.0, 0.0)
    row_p = lax.broadcasted_iota(jnp.int32, (GDN_PAIR, 1), 0)
    zeros_c = jnp.zeros((A_CHUNK, A_DV), F32)

    def pair(p, _):
        bodies = []
        for d in range(N_DIR):
            base = pl.multiple_of((p if d == 0 else n_pairs - 1 - p) * GDN_PAIR, GDN_PAIR)
            rows = pl.ds(base, GDN_PAIR)
            gbp = gb_s[rows, :]
            gp = gcum_s[d, rows, :]
            gp_t = gp.T
            tri = (pr >= pc) if d == 0 else (pr <= pc)
            strict = (pr > pc) if d == 0 else (pr < pc)
            for h in range(A_HEADS):
                cols = slice(h * A_DK, (h + 1) * A_DK)
                bi, gi = d * A_HEADS + h, N_DIR * A_HEADS + d * A_HEADS + h
                qp, kp, vp = q_s[rows, cols], k_s[rows, cols], v_s[rows, cols]
                beta = gbp[:, bi:bi + 1]
                g_col = gp[:, gi:gi + 1]
                g_row = gp_t[gi:gi + 1, :]
                decay = jnp.where(p_same & tri, jnp.exp(jnp.minimum(g_col - g_row, 0.0)), 0.0)
                kb = kp * beta
                kp16 = kp.astype(BF16)
                m = jnp.where(p_same & strict, -(_dot_nt(kb.astype(BF16), kp16) * decay), 0.0)
                e_g = jnp.exp(g_col)
                last = A_CHUNK - 1 if d == 0 else 0
                g_last = [g_col[c * A_CHUNK + last:c * A_CHUNK + last + 1] for c in range(2)]
                gl_col = jnp.where(row_p < A_CHUNK, g_last[0], g_last[1])
                bodies.append(dict(
                    d=d, h=h, base=base, cols=cols, m=m, g_last=g_last,
                    rhs=jnp.concatenate([vp * beta, kb * e_g], axis=1).astype(BF16),
                    attn=(_dot_nt(qp.astype(BF16), kp16) * decay).astype(BF16),
                    q_in=(qp * e_g).astype(BF16),
                    k_out_t=(kp * jnp.exp(gl_col - g_col)).T.astype(BF16)))

        ms = [body["m"] for body in bodies]
        t_invs = [eye + m for m in ms]
        for _ in range(GDN_LEVELS):
            ms = [_mm3(m, m) for m in ms]
            t_invs = [t + _mm3(t, m) for t, m in zip(t_invs, ms)]

        for body, t_inv in zip(bodies, t_invs):
            d, h, base, cols, g_last = body["d"], body["h"], body["base"], body["cols"], body["g_last"]
            uw = jnp.dot(t_inv.astype(BF16), body["rhs"], preferred_element_type=F32)
            u, w = uw[:, :A_DV], uw[:, A_DV:]
            s = st_s[d, h]
            for c in ((0, 1) if d == 0 else (1, 0)):
                rs = slice(c * A_CHUNK, (c + 1) * A_CHUNK)
                s16 = s.astype(BF16)
                v_new = u[rs] - jnp.dot(w[rs].astype(BF16), s16, preferred_element_type=F32)
                v_full = jnp.concatenate([v_new, zeros_c] if c == 0 else [zeros_c, v_new], axis=0).astype(BF16)
                o_c = (jnp.dot(body["q_in"][rs], s16, preferred_element_type=F32)
                       + jnp.dot(body["attn"][rs], v_full, preferred_element_type=F32))
                s = s * jnp.exp(g_last[c]) + jnp.dot(body["k_out_t"], v_full, preferred_element_type=F32)
                rows_c = pl.ds(base + c * A_CHUNK, A_CHUNK)
                o_s[rows_c, cols] = o_s[rows_c, cols] + o_c
            st_s[d, h] = s
        return 0

    lax.fori_loop(0, n_pairs, pair, 0)

    for h in range(A_HEADS):
        cols = slice(h * A_DV, (h + 1) * A_DV)
        o = o_s[:, cols]
        o = o * lax.rsqrt(jnp.mean(o * o, axis=-1, keepdims=True) + RMS_EPS) * gain_ref[...]
        out_ref[:, cols] = o * _silu(proj_ref[:, 3 * EV_A + h * A_DV:3 * EV_A + (h + 1) * A_DV])
    if not has_state:
        for d in range(N_DIR):
            for h in range(A_HEADS):
                fin_ref[d, h] = st_s[d, h]


def _gdn_mixer(grp, proj, conv_w, a_log, dt_bias, gain, s0):
    seq_len = grp.seq_len
    has_state = s0 is not None
    n_small = N_DIR * A_HEADS
    lane_row = lambda v: jnp.zeros((1, LANES), F32).at[0, n_small:2 * n_small].set(v.reshape(n_small))
    convw = jnp.zeros((8, 3 * EV_A), F32).at[:A_CONV].set(conv_w)
    full = lambda *shape: pl.BlockSpec(shape, lambda b: (0,) * len(shape))
    state_spec = pl.BlockSpec((None, N_DIR, A_HEADS, A_DK, A_DV), lambda b: (b, 0, 0, 0, 0))
    in_specs = [pl.BlockSpec((seq_len, EV_HALF), lambda b: (b, 0)),
                pl.BlockSpec((seq_len, LANES), lambda b: (b, 2 * EV_HALF // LANES)),
                full(8, 3 * EV_A), full(1, LANES), full(1, LANES), full(1, A_DV)]
    args = [proj, proj, convw, lane_row(a_log), lane_row(dt_bias), gain[None]]
    out_spec = pl.BlockSpec((seq_len, EV_A), lambda b: (b, 0))
    out_shape = jax.ShapeDtypeStruct((grp.n_seq * seq_len, EV_A), F32)
    if has_state:
        in_specs.append(state_spec)
        args.append(s0)
        out_specs, out_shapes = out_spec, out_shape
    else:
        out_specs = [out_spec, state_spec]
        out_shapes = [out_shape, jax.ShapeDtypeStruct((grp.n_seq, N_DIR, A_HEADS, A_DK, A_DV), F32)]
    tok = lambda w: pltpu.VMEM((seq_len, w), F32)
    out = pl.pallas_call(
        functools.partial(_gdn_kernel, seq_len, has_state),
        grid=(grp.n_seq,),
        in_specs=in_specs, out_specs=out_specs, out_shape=out_shapes,
        scratch_shapes=[tok(EV_A), tok(EV_A), tok(EV_A), tok(LANES), pltpu.VMEM((N_DIR, seq_len, LANES), F32),
                        tok(EV_A), pltpu.VMEM((N_DIR, A_HEADS, A_DK, A_DV), F32)],
        compiler_params=_params(1),
        name="gdn_mixer",
    )(*args)
    return (out, None) if has_state else (out[0], out[1])


def _rope_tables(n_tokens):
    lane = np.arange(EV_B)
    axis = (lane % B_DK) // (2 * ROPE_FREQS)
    half = (lane % (2 * ROPE_FREQS)) // ROPE_FREQS
    freq = ROPE_BASE ** (-(lane % ROPE_FREQS).astype(np.float32) / ROPE_FREQS)
    tok = jnp.arange(n_tokens, dtype=F32)
    pos = jnp.where(jnp.asarray(axis)[None, :] == 0, jnp.floor(tok / GRID_W)[:, None], (tok % GRID_W)[:, None])
    ang = pos * jnp.asarray(freq, F32)[None, :]
    sign = jnp.asarray(np.where(half == 0, -1.0, 1.0), F32)[None, :]
    return jnp.cos(ang), jnp.sin(ang) * sign


def _dot2_lhs(x, m):
    hi = x.astype(BF16)
    lo = (x - hi.astype(F32)).astype(BF16)
    return jnp.dot(hi, m, preferred_element_type=F32) + jnp.dot(lo, m, preferred_element_type=F32)


def _ret_kernel(seq_len, has_state, rope, proj_ref, lgd_ref, gain_ref, *rest):
    rest = list(rest)
    cos_ref, sin_ref = (rest.pop(0), rest.pop(0)) if rope else (None, None)
    s0_ref = rest.pop(0) if has_state else None
    out_ref = rest.pop(0)
    fin_ref = None if has_state else rest.pop(0)

    q = proj_ref[:, 0:EV_B]
    k = proj_ref[:, EV_B:2 * EV_B]
    if rope:
        lane = lax.broadcasted_iota(jnp.int32, (seq_len, EV_B), 1)
        first = (lane % (2 * ROPE_FREQS)) < ROPE_FREQS

        def rotate(x):
            partner = jnp.where(first, pltpu.roll(x, EV_B - ROPE_FREQS, 1), pltpu.roll(x, ROPE_FREQS, 1))
            return x * cos_ref[...] + partner * sin_ref[...]

        q, k = rotate(q), rotate(k)
    q = q * (B_DK ** -0.5)
    log_gamma = -jnp.exp(lgd_ref[...])
    i_col = lax.broadcasted_iota(jnp.int32, (seq_len, 1), 0).astype(F32)
    dist = (lax.broadcasted_iota(jnp.int32, (seq_len, seq_len), 0)
            - lax.broadcasted_iota(jnp.int32, (seq_len, seq_len), 1)).astype(F32)
    lane_p = lax.broadcasted_iota(jnp.int32, (1, LANES), 1)
    h_avg = jnp.where((lax.broadcasted_iota(jnp.int32, (LANES, LANES), 0) // B_DV)
                      == (lax.broadcasted_iota(jnp.int32, (LANES, LANES), 1) // B_DV), 1.0 / B_DV, 0.0).astype(BF16)

    for pair in range(B_HEADS // 2):
        cols = slice(pair * LANES, (pair + 1) * LANES)
        qp, kp = q[:, cols], k[:, cols]
        vp = proj_ref[:, 2 * EV_B + pair * LANES:2 * EV_B + (pair + 1) * LANES]
        kp16, vp16 = kp.astype(BF16), vp.astype(BF16)
        o = jnp.zeros((seq_len, LANES), F32)
        lg = [[log_gamma[d:d + 1, 2 * pair + e:2 * pair + e + 1] for e in range(2)] for d in range(N_DIR)]
        for e in range(2):
            mine = (lane_p < B_DK) if e == 0 else (lane_p >= B_DK)
            scores = _dot_nt(jnp.where(mine, qp, 0.0).astype(BF16), kp16)
            weight = (jnp.where(dist >= 0.0, jnp.exp(jnp.maximum(dist, 0.0) * lg[0][e]), 0.0)
                      + jnp.where(dist <= 0.0, jnp.exp(jnp.maximum(-dist, 0.0) * lg[1][e]), 0.0))
            o = o + jnp.dot((scores * weight).astype(BF16), jnp.where(mine, vp, 0.0).astype(BF16),
                            preferred_element_type=F32)
        if has_state:
            for d in range(N_DIR):
                steps = (i_col + 1.0) if d == 0 else (seq_len - i_col)
                xi = jnp.where(lane_p < B_DK, jnp.exp(steps * lg[d][0]), jnp.exp(steps * lg[d][1]))
                zeros = jnp.zeros((B_DK, B_DV), F32)
                s_pair = jnp.concatenate(
                    [jnp.concatenate([s0_ref[d, 2 * pair], zeros], axis=1),
                     jnp.concatenate([zeros, s0_ref[d, 2 * pair + 1]], axis=1)], axis=0)
                o = o + jnp.dot((qp * xi).astype(BF16), s_pair.astype(BF16), preferred_element_type=F32)
        else:
            for d in range(N_DIR):
                steps = (seq_len - 1.0 - i_col) if d == 0 else i_col
                zeta = jnp.where(lane_p < B_DK, jnp.exp(steps * lg[d][0]), jnp.exp(steps * lg[d][1]))
                kz_t = (kp * zeta).T.astype(BF16)
                both = jnp.dot(kz_t, vp16, preferred_element_type=F32)
                fin_ref[d, 2 * pair] = both[:B_DK, :B_DV]
                fin_ref[d, 2 * pair + 1] = both[B_DK:, B_DV:]
        mu = _dot2_lhs(o, h_avg)
        cen = o - mu
        var = _dot2_lhs(cen * cen, h_avg)
        normed = cen * lax.rsqrt(var + RMS_EPS) * gain_ref[:, cols]
        out_ref[:, cols] = normed * _silu(proj_ref[:, 3 * EV_B + pair * LANES:3 * EV_B + (pair + 1) * LANES])


def _ret_mixer(grp, proj, log_decay, gain, s0, rope_tables):
    seq_len = grp.seq_len
    has_state = s0 is not None
    rope = rope_tables is not None
    lgd = jnp.zeros((8, LANES), F32).at[:N_DIR, :B_HEADS].set(log_decay)
    full = lambda *shape: pl.BlockSpec(shape, lambda b: (0,) * len(shape))
    state_spec = pl.BlockSpec((None, N_DIR, B_HEADS, B_DK, B_DV), lambda b: (b, 0, 0, 0, 0))
    in_specs = [pl.BlockSpec((seq_len, EV_HALF), lambda b: (b, 1)), full(8, LANES), full(1, EV_B)]
    args = [proj, lgd, gain[None]]
    if rope:
        in_specs += [full(seq_len, EV_B), full(seq_len, EV_B)]
        args += list(rope_tables)
    out_spec = pl.BlockSpec((seq_len, EV_B), lambda b: (b, 0))
    out_shape = jax.ShapeDtypeStruct((grp.n_seq * seq_len, EV_B), F32)
    if has_state:
        in_specs.append(state_spec)
        args.append(s0)
        out_specs, out_shapes = out_spec, out_shape
    else:
        out_specs = [out_spec, state_spec]
        out_shapes = [out_shape, jax.ShapeDtypeStruct((grp.n_seq, N_DIR, B_HEADS, B_DK, B_DV), F32)]
    out = pl.pallas_call(
        functools.partial(_ret_kernel, seq_len, has_state, rope),
        grid=(grp.n_seq,),
        in_specs=in_specs, out_specs=out_specs, out_shape=out_shapes,
        compiler_params=_params(1),
        name="retention_mixer",
    )(*args)
    return (out, None) if has_state else (out[0], out[1])


def _pad_cols(w, n):
    return jnp.pad(w, ((0, 0), (0, n - w.shape[1])))


def kernel(x_prompt, x_sample, state_gdn, state_ret, state_s5, state_gla, c, c_ctx, ada_w, ada_b, norm_mix,
           norm_ffn, norm_final, ev_w_in, ev_conv, gdn_a_log, gdn_dt_bias, gdn_gain, ret_log_decay, ret_gain,
           ev_w_out, od_w_in, s5_a_re, s5_a_im, s5_log_dt, s5_b_re, s5_b_im, s5_c_re, s5_c_im, s5_d, s5_w_glu,
           gla_w_gate2, gla_b_gate, gla_gain, od_w_out, moe_router, moe_w_gate, moe_w_up, moe_w_down):
    cond = jnp.zeros((COND_ROWS, D_MODEL), F32).at[0].set(c_ctx).at[1:N_COND].set(c)
    mods = _modulation(cond, ada_w, ada_b)
    rope_tables = _rope_tables(DEC_SEQ)

    xs = {CTX: x_prompt.reshape(BATCH * SEQ, D_MODEL), DEC: x_sample.reshape(DEC_BATCH * DEC_SEQ, D_MODEL)}
    new_states = {"gdn": [], "ret": [], "s5": [], "gla": []}

    for layer in range(DEPTH):
        j = layer // 2
        even = layer % 2 == 0
        if even:
            w_in = _regroup_even_cols(ev_w_in[j]).astype(BF16)
            w_out = ev_w_out[j].astype(BF16)
            w_outs = [w_out[:EV_A], w_out[EV_A:]]
        else:
            w_in = _pad_cols(od_w_in[j], ODD_IN_PAD).astype(BF16)
            w_out = od_w_out[j].astype(BF16)
            w_outs = [w_out[:C_WIDTH], w_out[C_WIDTH:]]
            s5_prep = _s5_prepare(s5_a_re[j], s5_a_im[j], s5_log_dt[j], s5_b_re[j], s5_b_im[j], s5_c_re[j],
                                  s5_c_im[j])
        w_router = _pad_cols(moe_router[layer], LANES)
        routed = {}
        for grp in (CTX, DEC):
            x = xs[grp]
            ctx = grp.is_context
            proj = _in_projection(grp, layer, x, mods, norm_mix[layer][None], w_in)
            if even:
                out_a, fin_a = _gdn_mixer(grp, proj, ev_conv[j], gdn_a_log[j], gdn_dt_bias[j], gdn_gain[j],
                                          None if ctx else state_gdn[:, j])
                out_b, fin_b = _ret_mixer(grp, proj, ret_log_decay[j], ret_gain[j],
                                          None if ctx else state_ret[:, j], None if ctx else rope_tables)
                mixes = [out_a, out_b]
                if ctx:
                    new_states["gdn"].append(fin_a)
                    new_states["ret"].append(fin_b)
            else:
                ypre, fin_c = _s5_mixer(grp, proj, s5_prep, None if ctx else _s5_state_to_blocks(state_s5[:, j]))
                out_c, out_d, fin_d = _gla_mixer(grp, proj, ypre, s5_d[j], s5_w_glu[j], gla_w_gate2[j],
                                                 gla_b_gate[j], gla_gain[j], None if ctx else state_gla[:, j])
                mixes = [out_c, out_d]
                if ctx:
                    new_states["s5"].append(_s5_state_from_blocks(fin_c))
                    new_states["gla"].append(fin_d)
            x1, h2, aff = _out_projection(grp, layer, mixes, x, mods, w_outs, norm_ffn[layer][None], w_router)
            xs_g, pt, gate = _route(grp, aff, h2)
            routed[grp] = (x1, xs_g, pt, gate)
        ys_c, ys_d = _experts(layer, routed[CTX][1], routed[DEC][1], routed[CTX][3], routed[DEC][3],
                              moe_w_gate, moe_w_up, moe_w_down)
        for grp, ys in ((CTX, ys_c), (DEC, ys_d)):
            x1, _, pt, _ = routed[grp]
            xs[grp] = _combine(grp, layer, layer == DEPTH - 1, x1, mods, pt, ys, norm_final[None])

    y_prompt = xs[CTX].reshape(BATCH, SEQ, D_MODEL)
    y_sample = xs[DEC].reshape(DEC_BATCH, DEC_SEQ, D_MODEL)
    return (y_prompt, y_sample, jnp.stack(new_states["gdn"], axis=1), jnp.stack(new_states["ret"], axis=1),
            jnp.stack(new_states["s5"], axis=1), jnp.stack(new_states["gla"], axis=1))
```
